```python
import math
import jax, jax.numpy as jnp
from jax import lax
import numpy as np

D_MODEL = 4096
BATCH = 4
SEQ = 2048
DEPTH = 1

ATTN_HEADS = 16
HEAD_DIM = 128
ATTN_WIDTH = ATTN_HEADS * HEAD_DIM
MOBA_BLOCK = 256
MOBA_TOPK = 3
Q_CHUNK = 16
REL_BUCKETS = 32
REL_MAX_DIST = 128
CONV_GROUPS = 16
CONV_GROUP_DIM = 128
CONV_WIDTH = CONV_GROUPS * CONV_GROUP_DIM
CONV_K = 3
IN_PROJ_WIDTH = 3 * ATTN_WIDTH + 3 * CONV_WIDTH + 2 * D_MODEL
IN_SPLITS = (ATTN_WIDTH, 2 * ATTN_WIDTH, 3 * ATTN_WIDTH,
             3 * ATTN_WIDTH + CONV_WIDTH, 3 * ATTN_WIDTH + 2 * CONV_WIDTH,
             3 * ATTN_WIDTH + 3 * CONV_WIDTH, 3 * ATTN_WIDTH + 3 * CONV_WIDTH + D_MODEL)
PEER_HEADS = 8
PEER_N_KEYS = 128
PEER_N_EXPERTS = PEER_N_KEYS * PEER_N_KEYS
PEER_QUERY_DIM = 256
PEER_HALF = PEER_QUERY_DIM // 2
PEER_TOPK = 16
PEER_TOKEN_CHUNK = 64
NORM_EPS = 1e-6

kernel_name = "hybrid_conv_moba_peer_block"


def rms_norm(x, g):
    x32 = x.astype(jnp.float32)
    y = x32 * lax.rsqrt(jnp.mean(x32 * x32, axis=-1, keepdims=True) + NORM_EPS)
    return y.astype(x.dtype) * g


def t5_bucket(dist):
    max_exact = REL_BUCKETS // 2
    d32 = jnp.maximum(dist, 1).astype(jnp.float32)
    large = max_exact + (jnp.log(d32 / max_exact) / math.log(REL_MAX_DIST / max_exact)
                         * (REL_BUCKETS - max_exact)).astype(jnp.int32)
    large = jnp.minimum(large, REL_BUCKETS - 1)
    return jnp.where(dist < max_exact, dist, large)


def short_conv_mixer(cb, cc, cu, conv_w, conv_b):
    gated = cc * cu
    y = lax.conv_general_dilated(gated, conv_w.astype(gated.dtype), window_strides=(1,),
                                 padding=[(CONV_K - 1, 0)],
                                 dimension_numbers=('NWC', 'WIO', 'NWC'),
                                 feature_group_count=CONV_WIDTH)
    return cb * (y + conv_b)


def moba_attention(q, k, v, rel_table):
    B, T = q.shape[0], q.shape[1]
    L = MOBA_BLOCK
    nb = -(-T // L)
    t_pad = nb * L
    pad = ((0, 0), (0, t_pad - T), (0, 0), (0, 0))
    q, k, v = [jnp.pad(a, pad).transpose(0, 2, 1, 3) for a in (q, k, v)]
    k_blocks = k.reshape(B, ATTN_HEADS, nb, L, HEAD_DIM)
    v_blocks = v.reshape(B, ATTN_HEADS, nb, L, HEAD_DIM)
    k_mean = jnp.mean(k_blocks.astype(jnp.float32), axis=3)
    gate = jnp.einsum('bhtd,bhnd->bhtn', q.astype(jnp.float32), k_mean)
    pos = jnp.arange(t_pad, dtype=jnp.int32)
    q_block = pos // L
    fully_past = jnp.arange(nb, dtype=jnp.int32)[None, :] < q_block[:, None]
    gate = jnp.where(fully_past, gate, -jnp.inf)
    k_sel = min(MOBA_TOPK, nb)
    _, top_idx = lax.top_k(gate, k_sel)
    top_idx = top_idx.astype(jnp.int32)
    top_valid = top_idx < q_block[:, None]
    own = jnp.broadcast_to(q_block[:, None], (B, ATTN_HEADS, t_pad, 1))
    sel = jnp.concatenate([top_idx, own], axis=-1)
    valid = jnp.concatenate([top_valid, jnp.ones_like(own, dtype=bool)], axis=-1)

    n_chunks = t_pad // Q_CHUNK

    def to_chunks(a):
        return jnp.moveaxis(a.reshape(B, ATTN_HEADS, n_chunks, Q_CHUNK, a.shape[-1]), 2, 0)

    b_idx = jnp.arange(B)[:, None, None, None]
    h_idx = jnp.arange(ATTN_HEADS)[None, :, None, None]
    key_off = jnp.arange(L, dtype=jnp.int32)
    scale = HEAD_DIM ** -0.5

    def attend_chunk(args):
        q_c, sel_c, valid_c, c = args
        k_g = k_blocks[b_idx, h_idx, sel_c]
        v_g = v_blocks[b_idx, h_idx, sel_c]
        logits = jnp.einsum('bhqd,bhqskd->bhqsk', q_c, k_g).astype(jnp.float32) * scale
        q_pos = c * Q_CHUNK + jnp.arange(Q_CHUNK, dtype=jnp.int32)
        k_pos = sel_c[..., None] * L + key_off
        dist = q_pos[None, None, :, None, None] - k_pos
        bias = rel_table[t5_bucket(jnp.maximum(dist, 0)), h_idx[..., None]]
        allowed = valid_c[..., None] & (dist >= 0)
        logits = jnp.where(allowed, logits + bias.astype(jnp.float32), -jnp.inf)
        probs = jax.nn.softmax(logits.reshape(B, ATTN_HEADS, Q_CHUNK, -1), axis=-1)
        probs = probs.reshape(logits.shape).astype(v_g.dtype)
        return jnp.einsum('bhqsk,bhqskd->bhqd', probs, v_g)

    out = lax.map(attend_chunk, (to_chunks(q), to_chunks(sel), to_chunks(valid),
                                 jnp.arange(n_chunks, dtype=jnp.int32)))
    out = jnp.moveaxis(out, 0, 2).reshape(B, ATTN_HEADS, t_pad, HEAD_DIM)
    out = out.transpose(0, 2, 1, 3)[:, :T]
    return out.reshape(B, T, ATTN_WIDTH)


def peer_ffn(x, w_q, sub_keys, u_table, v_table):
    B, T, D = x.shape
    xf = x.reshape(B * T, D)
    q = (xf @ w_q).reshape(-1, PEER_HEADS, PEER_QUERY_DIM).astype(jnp.float32)
    keys = sub_keys.astype(jnp.float32)
    s1 = jnp.einsum('nhd,hkd->nhk', q[..., :PEER_HALF], keys[:, 0])
    s2 = jnp.einsum('nhd,hkd->nhk', q[..., PEER_HALF:], keys[:, 1])
    v1, i1 = lax.top_k(s1, PEER_TOPK)
    v2, i2 = lax.top_k(s2, PEER_TOPK)
    cand = (v1[..., :, None] + v2[..., None, :]).reshape(-1, PEER_HEADS, PEER_TOPK * PEER_TOPK)
    score, c = lax.top_k(cand, PEER_TOPK)
    e1 = jnp.take_along_axis(i1, c // PEER_TOPK, axis=-1)
    e2 = jnp.take_along_axis(i2, c % PEER_TOPK, axis=-1)
    experts = (e1 * PEER_N_KEYS + e2).astype(jnp.int32)
    gates = jax.nn.softmax(score, axis=-1)
    n_chunks = xf.shape[0] // PEER_TOKEN_CHUNK

    def chunk(args):
        x_c, e_c, g_c = args
        u = u_table[e_c]
        a = jnp.einsum('cd,chkd->chk', x_c, u)
        act = jax.nn.gelu(a.astype(jnp.float32), approximate=False) * g_c
        return jnp.einsum('chk,chkd->cd', act.astype(x_c.dtype), v_table[e_c])

    y = lax.map(chunk, (xf.reshape(n_chunks, PEER_TOKEN_CHUNK, D),
                        experts.reshape(n_chunks, PEER_TOKEN_CHUNK, PEER_HEADS, PEER_TOPK),
                        gates.reshape(n_chunks, PEER_TOKEN_CHUNK, PEER_HEADS, PEER_TOPK)))
    return y.reshape(B, T, D)


def setup_inputs(seed: int = 0) -> dict:
    key = jax.random.key(seed)
    ks = jax.random.split(key, 17)
    f32 = jnp.float32
    D = D_MODEL
    nrm = lambda k, shape, s: jax.random.normal(k, shape, f32) * s
    return {
        "x": nrm(ks[0], (BATCH, SEQ, D), 1.0),
        "norm_mix": 1.0 + nrm(ks[1], (DEPTH, D), 0.02),
        "w_in": nrm(ks[2], (DEPTH, D, IN_PROJ_WIDTH), D ** -0.5),
        "conv_w": nrm(ks[3], (DEPTH, CONV_K, 1, CONV_WIDTH), CONV_K ** -0.5),
        "conv_b": nrm(ks[4], (DEPTH, CONV_WIDTH), 0.01),
        "w_br_attn": nrm(ks[5], (DEPTH, ATTN_WIDTH, D), ATTN_WIDTH ** -0.5),
        "w_br_conv": nrm(ks[6], (DEPTH, CONV_WIDTH, D), CONV_WIDTH ** -0.5),
        "b_gate": nrm(ks[7], (DEPTH, 2, D), 0.01),
        "rel_bias": nrm(ks[8], (REL_BUCKETS, ATTN_HEADS), 0.1),
        "w_out": nrm(ks[9], (DEPTH, D, D), D ** -0.5),
        "norm_ffn": 1.0 + nrm(ks[10], (DEPTH, D), 0.02),
        "peer_w_q": nrm(ks[11], (DEPTH, D, PEER_HEADS * PEER_QUERY_DIM), D ** -0.5),
        "peer_sub_keys": nrm(ks[12], (DEPTH, PEER_HEADS, 2, PEER_N_KEYS, PEER_HALF), PEER_HALF ** -0.5),
        "peer_u": nrm(ks[13], (DEPTH, PEER_N_EXPERTS, D), D ** -0.5),
        "peer_v": nrm(ks[14], (DEPTH, PEER_N_EXPERTS, D), (PEER_HEADS * PEER_TOPK) ** -0.5),
        "norm_final": 1.0 + nrm(ks[15], (D,), 0.02),
    }


def reference(x, norm_mix, w_in, conv_w, conv_b, w_br_attn, w_br_conv, b_gate, rel_bias,
              w_out, norm_ffn, peer_w_q, peer_sub_keys, peer_u, peer_v, norm_final):
    B, T, _ = x.shape
    h = x
    for l in range(DEPTH):
        hn = rms_norm(h, norm_mix[l])
        proj = hn @ w_in[l]
        q, k, v, cb, cc, cu, g_attn, g_conv = jnp.split(proj, IN_SPLITS, axis=-1)
        heads = lambda a: a.reshape(B, T, ATTN_HEADS, HEAD_DIM)
        z_attn = moba_attention(heads(q), heads(k), heads(v), rel_bias) @ w_br_attn[l]
        z_conv = short_conv_mixer(cb, cc, cu, conv_w[l], conv_b[l]) @ w_br_conv[l]
        merged = (jax.nn.sigmoid(g_attn + b_gate[l, 0]) * z_attn
                  + jax.nn.sigmoid(g_conv + b_gate[l, 1]) * z_conv)
        h = h + merged @ w_out[l]
        h = h + peer_ffn(rms_norm(h, norm_ffn[l]), peer_w_q[l], peer_sub_keys[l],
                         peer_u[l], peer_v[l])
    return rms_norm(h, norm_final)
```

```python
import functools
import math

import numpy as np
import jax
import jax.numpy as jnp
from jax import lax
from jax.experimental import pallas as pl
from jax.experimental.pallas import tpu as pltpu

F32 = jnp.float32
BF16 = jnp.bfloat16

D_MODEL = 4096
ATTN_HEADS = 16
HEAD_DIM = 128
ATTN_WIDTH = ATTN_HEADS * HEAD_DIM
MOBA_BLOCK = 256
MOBA_TOPK = 3
REL_BUCKETS = 32
REL_MAX_DIST = 128
CONV_WIDTH = 2048
CONV_K = 3
PEER_HEADS = 8
PEER_N_KEYS = 128
PEER_HALF = 128
PEER_TOPK = 16
NORM_EPS = 1e-6

V7X_VMEM_BYTES = 64 * 1024 * 1024
VMEM_LIMIT = V7X_VMEM_BYTES - 8 * 1024 * 1024
LANES = 128
NT_DIMS = (((1,), (1,)), ((), ()))


def _params(semantics):
    return pltpu.CompilerParams(dimension_semantics=semantics, vmem_limit_bytes=VMEM_LIMIT)


def _rmsnorm_kernel(x_ref, g_ref, o_ref):
    x = x_ref[...]
    ms = jnp.mean(x * x, axis=-1, keepdims=True)
    o_ref[...] = (x * lax.rsqrt(ms + NORM_EPS) * g_ref[...]).astype(o_ref.dtype)


def _rmsnorm_sum_kernel(x_ref, y_ref, g_ref, o_ref):
    x = x_ref[...] + y_ref[...]
    ms = jnp.mean(x * x, axis=-1, keepdims=True)
    o_ref[...] = (x * lax.rsqrt(ms + NORM_EPS) * g_ref[...]).astype(o_ref.dtype)


def rmsnorm(x, g, out_dtype, add=None, rows=256):
    n, d = x.shape
    rows = min(rows, n)
    row_spec = pl.BlockSpec((rows, d), lambda i: (i, 0))
    g_spec = pl.BlockSpec((1, d), lambda i: (0, 0))
    if add is None:
        kern, ins, specs = _rmsnorm_kernel, (x, g.reshape(1, d)), [row_spec, g_spec]
    else:
        kern, ins, specs = _rmsnorm_sum_kernel, (x, add, g.reshape(1, d)), [row_spec, row_spec, g_spec]
    return pl.pallas_call(
        kern, grid=(n // rows,), in_specs=specs, out_specs=row_spec,
        out_shape=jax.ShapeDtypeStruct((n, d), out_dtype),
        compiler_params=_params(("parallel",)), name="rmsnorm",
    )(*ins)


def _matmul_kernel(a_ref, b_ref, o_ref):
    o_ref[...] = jnp.dot(a_ref[...], b_ref[...], preferred_element_type=F32).astype(o_ref.dtype)


def _matmul_res_kernel(a_ref, b_ref, r_ref, o_ref):
    acc = jnp.dot(a_ref[...], b_ref[...], preferred_element_type=F32)
    o_ref[...] = (r_ref[...] + acc).astype(o_ref.dtype)


def matmul(a, b, out_dtype, residual=None, tm=1024, tn=1024, name="matmul"):
    m, k = a.shape
    _, n = b.shape
    tm, tn = min(tm, m), min(tn, n)
    a_spec = pl.BlockSpec((tm, k), lambda j, i: (i, 0))
    b_spec = pl.BlockSpec((k, tn), lambda j, i: (0, j))
    o_spec = pl.BlockSpec((tm, tn), lambda j, i: (i, j))
    if residual is None:
        kern, ins, specs = _matmul_kernel, (a, b), [a_spec, b_spec]
    else:
        kern, ins, specs = _matmul_res_kernel, (a, b, residual), [a_spec, b_spec, o_spec]
    return pl.pallas_call(
        kern, grid=(n // tn, m // tm), in_specs=specs, out_specs=o_spec,
        out_shape=jax.ShapeDtypeStruct((m, n), out_dtype),
        compiler_params=_params(("parallel", "parallel")), name=name,
    )(*ins)


def _t5_bucket_table(max_dist):
    dist = np.arange(max_dist, dtype=np.int32)
    max_exact = REL_BUCKETS // 2
    d32 = np.maximum(dist, 1).astype(np.float32)
    ratio = (np.log(d32 / np.float32(max_exact)) / np.float32(math.log(REL_MAX_DIST / max_exact))
             * np.float32(REL_BUCKETS - max_exact))
    large = max_exact + ratio.astype(np.int32)
    large = np.minimum(large, REL_BUCKETS - 1)
    return np.where(dist < max_exact, dist, large).astype(np.int32)


def _moba_bucket_tile():
    L = MOBA_BLOCK
    qi = np.arange(L)[:, None]
    col = np.arange(2 * L)[None, :]
    dist = np.maximum(qi - col + L, 0)
    return _t5_bucket_table(2 * L)[dist]


def _moba_kernel(tab_ref, q_ref, k_ref, v_ref, bucket_ref, o_ref, bias_scr, km_scr, *, n_blocks):
    L = MOBA_BLOCK
    h = pl.program_id(0)
    b = pl.program_id(1)
    qb = pl.program_id(2)

    @pl.when((b == 0) & (qb == 0))
    def _():
        bk = bucket_ref[...]
        acc = jnp.zeros(bk.shape, F32)
        for kk in range(REL_BUCKETS):
            acc = jnp.where(bk == kk, tab_ref[kk, h], acc)
        bias_scr[...] = acc

    @pl.when(qb == 0)
    def _():
        km_scr[...] = jnp.zeros(km_scr.shape, F32)
        for n in range(n_blocks):
            kn = k_ref[n * L:(n + 1) * L, :].astype(F32)
            km_scr[n:n + 1, :] = jnp.mean(kn, axis=0, keepdims=True)

    q = q_ref[...]
    gate = lax.dot_general(q.astype(F32), km_scr[...], NT_DIMS,
                           precision=lax.Precision.HIGHEST, preferred_element_type=F32)
    lane = lax.broadcasted_iota(jnp.int32, gate.shape, 1)
    neg_inf = jnp.float32(-jnp.inf)
    gm = jnp.where(lane < qb, gate, neg_inf)
    rank = jnp.zeros(gate.shape, F32)
    for m in range(n_blocks):
        col = gm[:, m:m + 1]
        beats = (col > gm) | ((col == gm) & (lane > m))
        rank = rank + jnp.where(beats, 1.0, 0.0)
    sel = jnp.where((lane < qb) & (rank < MOBA_TOPK), 1.0, 0.0)

    row = lax.broadcasted_iota(jnp.int32, (L, L), 0)
    colk = lax.broadcasted_iota(jnp.int32, (L, L), 1)
    causal = jnp.where(row >= colk, 1.0, 0.0)
    bias_adj = bias_scr[:, 0:L]
    bias_own = bias_scr[:, L:2 * L]
    far = tab_ref[REL_BUCKETS - 1, h]
    scale = HEAD_DIM ** -0.5

    logits = []
    for n in range(n_blocks):
        s = lax.dot_general(q, k_ref[n * L:(n + 1) * L, :], NT_DIMS, preferred_element_type=F32)
        bias = jnp.where(n == qb, bias_own, jnp.where(n == qb - 1, bias_adj, far))
        allowed = jnp.where(n == qb, causal, jnp.broadcast_to(sel[:, n:n + 1], (L, L)))
        logits.append(jnp.where(allowed > 0.5, s * scale + bias, neg_inf))
    mx = logits[0].max(axis=-1, keepdims=True)
    for n in range(1, n_blocks):
        mx = jnp.maximum(mx, logits[n].max(axis=-1, keepdims=True))
    denom = jnp.zeros((L, 1), F32)
    acc = jnp.zeros((L, HEAD_DIM), F32)
    for n in range(n_blocks):
        p = jnp.exp(logits[n] - mx)
        denom = denom + p.sum(axis=-1, keepdims=True)
        acc = acc + jnp.dot(p.astype(BF16), v_ref[n * L:(n + 1) * L, :], preferred_element_type=F32)
    o_ref[...] = (acc / denom).astype(o_ref.dtype)


def moba_attention(proj, rel_bias, batch, seq, q_col, k_col, v_col):
    L = MOBA_BLOCK
    n_blocks = seq // L
    bucket = jnp.asarray(_moba_bucket_tile())
    kern = functools.partial(_moba_kernel, n_blocks=n_blocks)
    return pl.pallas_call(
        kern,
        grid=(ATTN_HEADS, batch, n_blocks),
        in_specs=[
            pl.BlockSpec(memory_space=pltpu.SMEM),
            pl.BlockSpec((L, HEAD_DIM), lambda h, b, i: (b * n_blocks + i, q_col + h)),
            pl.BlockSpec((seq, HEAD_DIM), lambda h, b, i: (b, k_col + h)),
            pl.BlockSpec((seq, HEAD_DIM), lambda h, b, i: (b, v_col + h)),
            pl.BlockSpec((L, 2 * L), lambda h, b, i: (0, 0)),
        ],
        out_specs=pl.BlockSpec((L, HEAD_DIM), lambda h, b, i: (b * n_blocks + i, h)),
        out_shape=jax.ShapeDtypeStruct((batch * seq, ATTN_WIDTH), BF16),
        scratch_shapes=[pltpu.VMEM((L, 2 * L), F32), pltpu.VMEM((LANES, HEAD_DIM), F32)],
        compiler_params=_params(("arbitrary", "arbitrary", "arbitrary")), name="moba",
    )(rel_bias, proj, proj, proj, bucket)


def _conv_kernel(cb_ref, cc_ref, cu_ref, w_ref, b_ref, o_ref):
    g = cc_ref[...].astype(F32) * cu_ref[...].astype(F32)
    row = lax.broadcasted_iota(jnp.int32, g.shape, 0)
    y = w_ref[CONV_K - 1:CONV_K, :] * g
    for s in range(1, CONV_K):
        shifted = jnp.where(row >= s, pltpu.roll(g, s, axis=0), 0.0)
        y = y + w_ref[CONV_K - 1 - s:CONV_K - s, :] * shifted
    o_ref[...] = (cb_ref[...].astype(F32) * (y + b_ref[...])).astype(o_ref.dtype)


def short_conv(proj, conv_w, conv_b, batch, seq, cb_col, cc_col, cu_col, cw=512):
    width = conv_w.shape[1]
    return pl.pallas_call(
        _conv_kernel,
        grid=(batch, width // cw),
        in_specs=[
            pl.BlockSpec((seq, cw), lambda b, c: (b, cb_col + c)),
            pl.BlockSpec((seq, cw), lambda b, c: (b, cc_col + c)),
            pl.BlockSpec((seq, cw), lambda b, c: (b, cu_col + c)),
            pl.BlockSpec((CONV_K, cw), lambda b, c: (0, c)),
            pl.BlockSpec((1, cw), lambda b, c: (0, c)),
        ],
        out_specs=pl.BlockSpec((seq, cw), lambda b, c: (b, c)),
        out_shape=jax.ShapeDtypeStruct((batch * seq, width), BF16),
        compiler_params=_params(("parallel", "parallel")), name="short_conv",
    )(proj, proj, proj, conv_w, conv_b)


def _merge_kernel(attn_ref, conv_ref, wa_ref, wc_ref, ga_ref, gc_ref, bg_ref, o_ref):
    za = jnp.dot(attn_ref[...], wa_ref[...], preferred_element_type=F32)
    zc = jnp.dot(conv_ref[...], wc_ref[...], preferred_element_type=F32)
    ga = jax.nn.sigmoid(ga_ref[...].astype(F32) + bg_ref[0:1, :])
    gc = jax.nn.sigmoid(gc_ref[...].astype(F32) + bg_ref[1:2, :])
    o_ref[...] = (ga * za + gc * zc).astype(o_ref.dtype)


def branch_merge(attn, conv, wa, wc, proj, b_gate, ga_col, gc_col, tm=1024, tn=1024):
    m, ka = attn.shape
    kc = conv.shape[1]
    n = wa.shape[1]
    tm, tn = min(tm, m), min(tn, n)
    return pl.pallas_call(
        _merge_kernel,
        grid=(n // tn, m // tm),
        in_specs=[
            pl.BlockSpec((tm, ka), lambda j, i: (i, 0)),
            pl.BlockSpec((tm, kc), lambda j, i: (i, 0)),
            pl.BlockSpec((ka, tn), lambda j, i: (0, j)),
            pl.BlockSpec((kc, tn), lambda j, i: (0, j)),
            pl.BlockSpec((tm, tn), lambda j, i: (i, ga_col + j)),
            pl.BlockSpec((tm, tn), lambda j, i: (i, gc_col + j)),
            pl.BlockSpec((2, tn), lambda j, i: (0, j)),
        ],
        out_specs=pl.BlockSpec((tm, tn), lambda j, i: (i, j)),
        out_shape=jax.ShapeDtypeStruct((m, n), BF16),
        compiler_params=_params(("parallel", "parallel")), name="branch_merge",
    )(attn, conv, wa, wc, proj, proj, b_gate)


def _extract_max(s, lane_f, width):
    m = s.max(axis=-1, keepdims=True)
    idx = jnp.where(s == m, lane_f, float(width)).min(axis=-1, keepdims=True)
    return m, idx, jnp.where(lane_f == idx, -jnp.inf, s)


def _peer_route_kernel(q_ref, keys_ref, code_ref, gate_ref):
    K = PEER_TOPK
    q = q_ref[...]
    s1 = lax.dot_general(q[:, :PEER_HALF], keys_ref[0, 0], NT_DIMS,
                         precision=lax.Precision.HIGHEST, preferred_element_type=F32)
    s2 = lax.dot_general(q[:, PEER_HALF:], keys_ref[0, 1], NT_DIMS,
                         precision=lax.Precision.HIGHEST, preferred_element_type=F32)
    tt = q.shape[0]
    lane = lax.broadcasted_iota(jnp.int32, (tt, PEER_N_KEYS), 1).astype(F32)
    lane2_i = lax.broadcasted_iota(jnp.int32, (tt, K * K), 1)
    lane2 = lane2_i.astype(F32)
    zeros2 = jnp.zeros((tt, K * K), F32)
    v1e, c1e, v2e, c2e = zeros2, zeros2, zeros2, zeros2
    for a in range(K):
        m, idx, s1 = _extract_max(s1, lane, PEER_N_KEYS)
        grp = (lane2_i // K) == a
        v1e = jnp.where(grp, m, v1e)
        c1e = jnp.where(grp, idx, c1e)
        m, idx, s2 = _extract_max(s2, lane, PEER_N_KEYS)
        grp = (lane2_i % K) == a
        v2e = jnp.where(grp, m, v2e)
        c2e = jnp.where(grp, idx, c2e)
    cand = v1e + v2e
    code = c1e * float(PEER_N_KEYS) + c2e
    lane_k = lax.broadcasted_iota(jnp.int32, (tt, K), 1)
    score = jnp.zeros((tt, K), F32)
    picked = jnp.zeros((tt, K), F32)
    for k in range(K):
        m, idx, cand_next = _extract_max(cand, lane2, K * K)
        ck = jnp.where(lane2 == idx, code, -1.0).max(axis=-1, keepdims=True)
        cand = cand_next
        score = jnp.where(lane_k == k, m, score)
        picked = jnp.where(lane_k == k, ck, picked)
    e = jnp.exp(score - score[:, 0:1])
    gate_ref[0] = e / e.sum(axis=-1, keepdims=True)
    code_ref[0] = picked.astype(jnp.int32)


def peer_route(q, sub_keys, tt=256):
    n = q.shape[0]
    tt = min(tt, n)
    qd = 2 * PEER_HALF
    out_spec = pl.BlockSpec((1, tt, PEER_TOPK), lambda i, h: (h, i, 0))
    return pl.pallas_call(
        _peer_route_kernel,
        grid=(n // tt, PEER_HEADS),
        in_specs=[
            pl.BlockSpec((tt, qd), lambda i, h: (i, h)),
            pl.BlockSpec((1, 2, PEER_N_KEYS, PEER_HALF), lambda i, h: (h, 0, 0, 0)),
        ],
        out_specs=[out_spec, out_spec],
        out_shape=[jax.ShapeDtypeStruct((PEER_HEADS, n, PEER_TOPK), jnp.int32),
                   jax.ShapeDtypeStruct((PEER_HEADS, n, PEER_TOPK), F32)],
        compiler_params=_params(("parallel", "parallel")), name="peer_route",
    )(q, sub_keys)


def _peer_weights_kernel(code_ref, gate_ref, o_ref, *, unroll):
    nk = PEER_N_KEYS
    picks = code_ref.shape[1]
    sub = lax.broadcasted_iota(jnp.int32, (nk, picks), 0)

    def body(t, carry):
        c = code_ref[pl.ds(t, 1), :]
        g = gate_ref[pl.ds(t, 1), :]
        e1 = lax.shift_right_logical(c, 7)
        e2 = lax.bitwise_and(c, nk - 1)
        at = jnp.where(sub == e1, g, 0.0).astype(BF16)
        bt = jnp.where(sub == e2, 1.0, 0.0).astype(BF16)
        w = lax.dot_general(at, bt, NT_DIMS, preferred_element_type=F32)
        o_ref[t] = w.astype(o_ref.dtype)
        return carry

    lax.fori_loop(0, code_ref.shape[0], body, 0, unroll=unroll)


def peer_dense_weights(code, gate, tt=128, unroll=4):
    n, picks = code.shape
    tt = min(tt, n)
    nk = PEER_N_KEYS
    kern = functools.partial(_peer_weights_kernel, unroll=unroll)
    return pl.pallas_call(
        kern,
        grid=(n // tt,),
        in_specs=[pl.BlockSpec((tt, picks), lambda i: (i, 0)),
                  pl.BlockSpec((tt, picks), lambda i: (i, 0))],
        out_specs=pl.BlockSpec((tt, nk, nk), lambda i: (i, 0, 0)),
        out_shape=jax.ShapeDtypeStruct((n, nk, nk), BF16),
        compiler_params=_params(("parallel",)), name="peer_weights",
    )(code, gate)


def _peer_dense_kernel(x_ref, u_ref, v_ref, w_ref, o_ref):
    j = pl.program_id(1)

    @pl.when(j == 0)
    def _():
        o_ref[...] = jnp.zeros(o_ref.shape, o_ref.dtype)

    a = lax.dot_general(x_ref[...], u_ref[...], NT_DIMS, preferred_element_type=F32)
    act = 0.5 * a * (1.0 + lax.erf(a * math.sqrt(0.5))) * w_ref[...].astype(F32)
    o_ref[...] += jnp.dot(act.astype(BF16), v_ref[...], preferred_element_type=F32)


def peer_dense(x, u, v, w, tn=512, te=512):
    n, d = x.shape
    ne = u.shape[0]
    tn, te = min(tn, n), min(te, ne)
    return pl.pallas_call(
        _peer_dense_kernel,
        grid=(n // tn, ne // te),
        in_specs=[
            pl.BlockSpec((tn, d), lambda i, j: (i, 0)),
            pl.BlockSpec((te, d), lambda i, j: (j, 0)),
            pl.BlockSpec((te, d), lambda i, j: (j, 0)),
            pl.BlockSpec((tn, te), lambda i, j: (i, j)),
        ],
        out_specs=pl.BlockSpec((tn, d), lambda i, j: (i, 0)),
        out_shape=jax.ShapeDtypeStruct((n, d), F32),
        compiler_params=_params(("parallel", "arbitrary")), name="peer_dense",
    )(x, u, v, w)


def kernel(x, norm_mix, w_in, conv_w, conv_b, w_br_attn, w_br_conv, b_gate, rel_bias,
           w_out, norm_ffn, peer_w_q, peer_sub_keys, peer_u, peer_v, norm_final):
    batch, seq, d = x.shape
    n = batch * seq
    h = x.reshape(n, d)
    depth = norm_mix.shape[0]
    for l in range(depth):
        hn = rmsnorm(h, norm_mix[l], BF16)
        proj = matmul(hn, w_in[l].astype(BF16), BF16, name="in_proj")
        attn = moba_attention(proj, rel_bias, batch, seq,
                              q_col=0, k_col=ATTN_WIDTH // HEAD_DIM, v_col=2 * ATTN_WIDTH // HEAD_DIM)
        cw = 512
        c0 = 3 * ATTN_WIDTH // cw
        conv = short_conv(proj, conv_w[l].reshape(CONV_K, CONV_WIDTH), conv_b[l].reshape(1, CONV_WIDTH),
                          batch, seq, cb_col=c0, cc_col=c0 + CONV_WIDTH // cw, cu_col=c0 + 2 * CONV_WIDTH // cw)
        tn = 1024
        g0 = (3 * ATTN_WIDTH + 3 * CONV_WIDTH) // tn
        merged = branch_merge(attn, conv, w_br_attn[l].astype(BF16), w_br_conv[l].astype(BF16),
                              proj, b_gate[l], ga_col=g0, gc_col=g0 + d // tn, tn=tn)
        h = matmul(merged, w_out[l].astype(BF16), F32, residual=h, name="out_proj")

        hn2 = rmsnorm(h, norm_ffn[l], BF16)
        q = matmul(hn2, peer_w_q[l].astype(BF16), F32, name="peer_query")
        code, gate = peer_route(q, peer_sub_keys[l])
        picks = PEER_HEADS * PEER_TOPK
        code = code.transpose(1, 0, 2).reshape(n, picks)
        gate = gate.transpose(1, 0, 2).reshape(n, picks)
        w = peer_dense_weights(code, gate).reshape(n, PEER_N_KEYS * PEER_N_KEYS)
        y = peer_dense(hn2, peer_u[l].astype(BF16), peer_v[l].astype(BF16), w)
        if l + 1 < depth:
            h = h + y
    out = rmsnorm(h, norm_final, F32, add=y)
    return out.reshape(batch, seq, d)
```

```python
import functools
import math

import numpy as np
import jax
import jax.numpy as jnp
from jax import lax
from jax.experimental import pallas as pl
from jax.experimental.pallas import tpu as pltpu

F32 = jnp.float32
BF16 = jnp.bfloat16

D_MODEL = 4096
ATTN_HEADS = 16
HEAD_DIM = 128
ATTN_WIDTH = ATTN_HEADS * HEAD_DIM
MOBA_BLOCK = 256
MOBA_TOPK = 3
REL_BUCKETS = 32
REL_MAX_DIST = 128
CONV_WIDTH = 2048
CONV_K = 3
PEER_HEADS = 8
PEER_N_KEYS = 128
PEER_HALF = 128
PEER_TOPK = 16
NORM_EPS = 1e-6

V7X_VMEM_BYTES = 64 * 1024 * 1024
VMEM_LIMIT = V7X_VMEM_BYTES - 8 * 1024 * 1024
LANES = 128
SUBLANES = 8
NT_DIMS = (((1,), (1,)), ((), ()))


def _params(semantics):
    return pltpu.CompilerParams(dimension_semantics=semantics, vmem_limit_bytes=VMEM_LIMIT)


def _rmsnorm_kernel(x_ref, g_ref, o_ref):
    x = x_ref[...]
    ms = jnp.mean(x * x, axis=-1, keepdims=True)
    o_ref[...] = (x * lax.rsqrt(ms + NORM_EPS) * g_ref[...]).astype(o_ref.dtype)


def _rmsnorm_sum_kernel(x_ref, y_ref, g_ref, o_ref):
    x = x_ref[...] + y_ref[...]
    ms = jnp.mean(x * x, axis=-1, keepdims=True)
    o_ref[...] = (x * lax.rsqrt(ms + NORM_EPS) * g_ref[...]).astype(o_ref.dtype)


def rmsnorm(x, g, out_dtype, add=None, rows=256):
    n, d = x.shape
    rows = min(rows, n)
    row_spec = pl.BlockSpec((rows, d), lambda i: (i, 0))
    g_spec = pl.BlockSpec((1, d), lambda i: (0, 0))
    if add is None:
        kern, ins, specs = _rmsnorm_kernel, (x, g.reshape(1, d)), [row_spec, g_spec]
    else:
        kern, ins, specs = _rmsnorm_sum_kernel, (x, add, g.reshape(1, d)), [row_spec, row_spec, g_spec]
    return pl.pallas_call(
        kern, grid=(n // rows,), in_specs=specs, out_specs=row_spec,
        out_shape=jax.ShapeDtypeStruct((n, d), out_dtype),
        compiler_params=_params(("parallel",)), name="rmsnorm",
    )(*ins)


def _matmul_kernel(a_ref, b_ref, o_ref):
    o_ref[...] = jnp.dot(a_ref[...], b_ref[...], preferred_element_type=F32).astype(o_ref.dtype)


def _matmul_res_kernel(a_ref, b_ref, r_ref, o_ref):
    acc = jnp.dot(a_ref[...], b_ref[...], preferred_element_type=F32)
    o_ref[...] = (r_ref[...] + acc).astype(o_ref.dtype)


def matmul(a, b, out_dtype, residual=None, tm=1024, tn=1024, name="matmul"):
    m, k = a.shape
    _, n = b.shape
    tm, tn = min(tm, m), min(tn, n)
    a_spec = pl.BlockSpec((tm, k), lambda j, i: (i, 0))
    b_spec = pl.BlockSpec((k, tn), lambda j, i: (0, j))
    o_spec = pl.BlockSpec((tm, tn), lambda j, i: (i, j))
    if residual is None:
        kern, ins, specs = _matmul_kernel, (a, b), [a_spec, b_spec]
    else:
        kern, ins, specs = _matmul_res_kernel, (a, b, residual), [a_spec, b_spec, o_spec]
    return pl.pallas_call(
        kern, grid=(n // tn, m // tm), in_specs=specs, out_specs=o_spec,
        out_shape=jax.ShapeDtypeStruct((m, n), out_dtype),
        compiler_params=_params(("parallel", "parallel")), name=name,
    )(*ins)


def _t5_bucket_table(max_dist):
    dist = np.arange(max_dist, dtype=np.int32)
    max_exact = REL_BUCKETS // 2
    d32 = np.maximum(dist, 1).astype(np.float32)
    ratio = (np.log(d32 / np.float32(max_exact)) / np.float32(math.log(REL_MAX_DIST / max_exact))
             * np.float32(REL_BUCKETS - max_exact))
    large = max_exact + ratio.astype(np.int32)
    large = np.minimum(large, REL_BUCKETS - 1)
    return np.where(dist < max_exact, dist, large).astype(np.int32)


def _moba_bucket_tile():
    L = MOBA_BLOCK
    qi = np.arange(L)[:, None]
    col = np.arange(2 * L)[None, :]
    dist = np.maximum(qi - col + L, 0)
    return _t5_bucket_table(2 * L)[dist]


def _moba_kernel(tab_ref, q_ref, k_ref, v_ref, bucket_ref, o_ref, bias_scr, *, n_blocks):
    L = MOBA_BLOCK
    h = pl.program_id(0)
    b = pl.program_id(1)

    @pl.when(b == 0)
    def _():
        bk = bucket_ref[...]
        acc = jnp.zeros(bk.shape, F32)
        for kk in range(REL_BUCKETS):
            acc = jnp.where(bk == kk, tab_ref[kk, h], acc)
        bias_scr[...] = acc

    neg_inf = jnp.float32(-jnp.inf)
    far = tab_ref[REL_BUCKETS - 1, h]
    scale = HEAD_DIM ** -0.5
    k_means = [jnp.mean(k_ref[n * L:(n + 1) * L, :].astype(F32), axis=0, keepdims=True)
               for n in range(n_blocks)]
    pad = jnp.zeros((LANES - n_blocks, HEAD_DIM), F32)
    k_mean = jnp.concatenate(k_means + [pad], axis=0)
    row = lax.broadcasted_iota(jnp.int32, (L, L), 0)
    colk = lax.broadcasted_iota(jnp.int32, (L, L), 1)
    lane = lax.broadcasted_iota(jnp.int32, (L, LANES), 1)

    for qb in range(n_blocks):
        q = q_ref[qb * L:(qb + 1) * L, :]
        sel = None
        if qb > MOBA_TOPK:
            gate = lax.dot_general(q.astype(F32), k_mean, NT_DIMS,
                                   precision=lax.Precision.HIGHEST, preferred_element_type=F32)
            gm = jnp.where(lane < qb, gate, neg_inf)
            rank = jnp.zeros(gate.shape, F32)
            for m in range(qb):
                col = gm[:, m:m + 1]
                beats = (col > gm) | ((col == gm) & (lane > m))
                rank = rank + jnp.where(beats, 1.0, 0.0)
            sel = jnp.where(rank < MOBA_TOPK, 1.0, 0.0)
        logits = []
        for n in range(qb + 1):
            s = lax.dot_general(q, k_ref[n * L:(n + 1) * L, :], NT_DIMS, preferred_element_type=F32)
            if n == qb:
                logits.append(jnp.where(row >= colk, s * scale + bias_scr[:, L:2 * L], neg_inf))
            else:
                lg = s * scale + (bias_scr[:, 0:L] if n == qb - 1 else far)
                logits.append(lg if sel is None else jnp.where(sel[:, n:n + 1] > 0.5, lg, neg_inf))
        mx = logits[0].max(axis=-1, keepdims=True)
        for lg in logits[1:]:
            mx = jnp.maximum(mx, lg.max(axis=-1, keepdims=True))
        denom = jnp.zeros((L, 1), F32)
        acc = jnp.zeros((L, HEAD_DIM), F32)
        for n, lg in enumerate(logits):
            p = jnp.exp(lg - mx)
            denom = denom + p.sum(axis=-1, keepdims=True)
            acc = acc + jnp.dot(p.astype(BF16), v_ref[n * L:(n + 1) * L, :], preferred_element_type=F32)
        o_ref[qb * L:(qb + 1) * L, :] = (acc / denom).astype(o_ref.dtype)


def moba_attention(proj, rel_bias, batch, seq, q_col, k_col, v_col):
    L = MOBA_BLOCK
    n_blocks = seq // L
    bucket = jnp.asarray(_moba_bucket_tile())
    kern = functools.partial(_moba_kernel, n_blocks=n_blocks)
    return pl.pallas_call(
        kern,
        grid=(ATTN_HEADS, batch),
        in_specs=[
            pl.BlockSpec(memory_space=pltpu.SMEM),
            pl.BlockSpec((seq, HEAD_DIM), lambda h, b: (b, q_col + h)),
            pl.BlockSpec((seq, HEAD_DIM), lambda h, b: (b, k_col + h)),
            pl.BlockSpec((seq, HEAD_DIM), lambda h, b: (b, v_col + h)),
            pl.BlockSpec((L, 2 * L), lambda h, b: (0, 0)),
        ],
        out_specs=pl.BlockSpec((seq, HEAD_DIM), lambda h, b: (b, h)),
        out_shape=jax.ShapeDtypeStruct((batch * seq, ATTN_WIDTH), BF16),
        scratch_shapes=[pltpu.VMEM((L, 2 * L), F32)],
        compiler_params=_params(("arbitrary", "arbitrary")), name="moba",
    )(rel_bias, proj, proj, proj, bucket)


def _conv_kernel(cb_ref, cc_ref, cu_ref, w_ref, b_ref, o_ref):
    g = cc_ref[...].astype(F32) * cu_ref[...].astype(F32)
    row = lax.broadcasted_iota(jnp.int32, g.shape, 0)
    y = w_ref[CONV_K - 1:CONV_K, :] * g
    for s in range(1, CONV_K):
        shifted = jnp.where(row >= s, pltpu.roll(g, s, axis=0), 0.0)
        y = y + w_ref[CONV_K - 1 - s:CONV_K - s, :] * shifted
    o_ref[...] = (cb_ref[...].astype(F32) * (y + b_ref[...])).astype(o_ref.dtype)


def short_conv(proj, conv_w, conv_b, batch, seq, cb_col, cc_col, cu_col, cw=512):
    width = conv_w.shape[1]
    return pl.pallas_call(
        _conv_kernel,
        grid=(batch, width // cw),
        in_specs=[
            pl.BlockSpec((seq, cw), lambda b, c: (b, cb_col + c)),
            pl.BlockSpec((seq, cw), lambda b, c: (b, cc_col + c)),
            pl.BlockSpec((seq, cw), lambda b, c: (b, cu_col + c)),
            pl.BlockSpec((CONV_K, cw), lambda b, c: (0, c)),
            pl.BlockSpec((1, cw), lambda b, c: (0, c)),
        ],
        out_specs=pl.BlockSpec((seq, cw), lambda b, c: (b, c)),
        out_shape=jax.ShapeDtypeStruct((batch * seq, width), BF16),
        compiler_params=_params(("parallel", "parallel")), name="short_conv",
    )(proj, proj, proj, conv_w, conv_b)


def _merge_kernel(attn_ref, conv_ref, wa_ref, wc_ref, ga_ref, gc_ref, bg_ref, o_ref):
    za = jnp.dot(attn_ref[...], wa_ref[...], preferred_element_type=F32)
    zc = jnp.dot(conv_ref[...], wc_ref[...], preferred_element_type=F32)
    ga = jax.nn.sigmoid(ga_ref[...].astype(F32) + bg_ref[0:1, :])
    gc = jax.nn.sigmoid(gc_ref[...].astype(F32) + bg_ref[1:2, :])
    o_ref[...] = (ga * za + gc * zc).astype(o_ref.dtype)


def branch_merge(attn, conv, wa, wc, proj, b_gate, ga_col, gc_col, tm=1024, tn=1024):
    m, ka = attn.shape
    kc = conv.shape[1]
    n = wa.shape[1]
    tm, tn = min(tm, m), min(tn, n)
    return pl.pallas_call(
        _merge_kernel,
        grid=(n // tn, m // tm),
        in_specs=[
            pl.BlockSpec((tm, ka), lambda j, i: (i, 0)),
            pl.BlockSpec((tm, kc), lambda j, i: (i, 0)),
            pl.BlockSpec((ka, tn), lambda j, i: (0, j)),
            pl.BlockSpec((kc, tn), lambda j, i: (0, j)),
            pl.BlockSpec((tm, tn), lambda j, i: (i, ga_col + j)),
            pl.BlockSpec((tm, tn), lambda j, i: (i, gc_col + j)),
            pl.BlockSpec((2, tn), lambda j, i: (0, j)),
        ],
        out_specs=pl.BlockSpec((tm, tn), lambda j, i: (i, j)),
        out_shape=jax.ShapeDtypeStruct((m, n), BF16),
        compiler_params=_params(("parallel", "parallel")), name="branch_merge",
    )(attn, conv, wa, wc, proj, proj, b_gate)


def _topk_rows(s, k):
    nrows, ncols = s.shape
    rowi = lax.broadcasted_iota(jnp.int32, s.shape, 0).astype(F32)
    slot = lax.broadcasted_iota(jnp.int32, (k, ncols), 0)
    vals = jnp.zeros((k, ncols), F32)
    rows = jnp.zeros((k, ncols), F32)
    for r in range(k):
        m = s.max(axis=0, keepdims=True)
        idx = jnp.where(s == m, rowi, float(nrows)).min(axis=0, keepdims=True)
        s = jnp.where(rowi == idx, -jnp.inf, s)
        vals = jnp.where(slot == r, m, vals)
        rows = jnp.where(slot == r, idx, rows)
    return vals, rows


def _pair_candidates(v1, i1, v2, i2):
    K = PEER_TOPK
    ncols = v1.shape[1]
    sub8 = lax.broadcasted_iota(jnp.int32, (SUBLANES, ncols), 0)
    sub16 = lax.broadcasted_iota(jnp.int32, (K, ncols), 0)
    nk = float(PEER_N_KEYS)
    big = float(K * K)
    vals = [v1[0:1] + v2]
    codes = [i1[0:1] * nk + i2]
    flat = [sub16.astype(F32)]
    for a in range(1, SUBLANES):
        ok = sub8 < K // (a + 1)
        vals.append(jnp.where(ok, v1[a:a + 1] + v2[0:SUBLANES], -jnp.inf))
        codes.append(i1[a:a + 1] * nk + i2[0:SUBLANES])
        flat.append(jnp.where(ok, (sub8 + a * K).astype(F32), big))
    vals.append(v1[SUBLANES:K] + v2[0:1])
    codes.append(i1[SUBLANES:K] * nk + i2[0:1])
    flat.append(((sub8 + SUBLANES) * K).astype(F32))
    return jnp.concatenate(vals, axis=0), jnp.concatenate(codes, axis=0), jnp.concatenate(flat, axis=0)


def _peer_route_kernel(q_ref, keys_ref, code_ref, gate_ref, s1_scr, s2_scr, code_scr, gate_scr):
    K = PEER_TOPK
    h = pl.program_id(1)
    q = q_ref[...]
    s1_scr[...] = lax.dot_general(keys_ref[0, 0], q[:, :PEER_HALF], NT_DIMS,
                                  precision=lax.Precision.HIGHEST, preferred_element_type=F32)
    s2_scr[...] = lax.dot_general(keys_ref[0, 1], q[:, PEER_HALF:], NT_DIMS,
                                  precision=lax.Precision.HIGHEST, preferred_element_type=F32)
    row0 = pl.multiple_of(h * K, K)

    def chunk(c, carry):
        off = pl.multiple_of(c * LANES, LANES)
        v1, i1 = _topk_rows(s1_scr[:, pl.ds(off, LANES)], K)
        v2, i2 = _topk_rows(s2_scr[:, pl.ds(off, LANES)], K)
        cand, code, flat = _pair_candidates(v1, i1, v2, i2)
        slot = lax.broadcasted_iota(jnp.int32, (K, LANES), 0)
        score = jnp.zeros((K, LANES), F32)
        picked = jnp.zeros((K, LANES), F32)
        for r in range(K):
            m = cand.max(axis=0, keepdims=True)
            first = jnp.where(cand == m, flat, float(K * K)).min(axis=0, keepdims=True)
            hit = flat == first
            ck = jnp.where(hit, code, -1.0).max(axis=0, keepdims=True)
            cand = jnp.where(hit, -jnp.inf, cand)
            score = jnp.where(slot == r, m, score)
            picked = jnp.where(slot == r, ck, picked)
        e = jnp.exp(score - score[0:1])
        code_scr[pl.ds(row0, K), pl.ds(off, LANES)] = picked
        gate_scr[pl.ds(row0, K), pl.ds(off, LANES)] = e / e.sum(axis=0, keepdims=True)
        return carry

    lax.fori_loop(0, q.shape[0] // LANES, chunk, 0)

    @pl.when(h == PEER_HEADS - 1)
    def _():
        code_ref[...] = code_scr[...].T.astype(jnp.int32)
        gate_ref[...] = gate_scr[...].T


def peer_route(q, sub_keys, tt=512):
    n = q.shape[0]
    tt = min(tt, n)
    qd = 2 * PEER_HALF
    picks = PEER_HEADS * PEER_TOPK
    out_spec = pl.BlockSpec((tt, picks), lambda i, h: (i, 0))
    return pl.pallas_call(
        _peer_route_kernel,
        grid=(n // tt, PEER_HEADS),
        in_specs=[
            pl.BlockSpec((tt, qd), lambda i, h: (i, h)),
            pl.BlockSpec((1, 2, PEER_N_KEYS, PEER_HALF), lambda i, h: (h, 0, 0, 0)),
        ],
        out_specs=[out_spec, out_spec],
        out_shape=[jax.ShapeDtypeStruct((n, picks), jnp.int32),
                   jax.ShapeDtypeStruct((n, picks), F32)],
        scratch_shapes=[pltpu.VMEM((PEER_N_KEYS, tt), F32), pltpu.VMEM((PEER_N_KEYS, tt), F32),
                        pltpu.VMEM((picks, tt), F32), pltpu.VMEM((picks, tt), F32)],
        compiler_params=_params(("parallel", "arbitrary")), name="peer_route",
    )(q, sub_keys)


W_ROW_PITCH = PEER_N_KEYS + SUBLANES


def _peer_weights_kernel(code_ref, gate_ref, o_ref, w_scr, *, unroll):
    nk = PEER_N_KEYS
    tt, picks = code_ref.shape
    sub = lax.broadcasted_iota(jnp.int32, (nk, picks), 0)

    def body(t, carry):
        c = code_ref[pl.ds(t, 1), :]
        g = gate_ref[pl.ds(t, 1), :]
        e1 = lax.shift_right_logical(c, 7)
        e2 = lax.bitwise_and(c, nk - 1)
        at = jnp.where(sub == e1, g, 0.0).astype(BF16)
        bt = jnp.where(sub == e2, 1.0, 0.0).astype(BF16)
        w = lax.dot_general(at, bt, NT_DIMS, preferred_element_type=F32)
        w_scr[pl.ds(pl.multiple_of(t * W_ROW_PITCH, SUBLANES), nk), :] = w
        return carry

    lax.fori_loop(0, tt, body, 0, unroll=unroll)
    for e1 in range(nk):
        o_ref[:, e1 * nk:(e1 + 1) * nk] = w_scr[pl.ds(e1, tt, stride=W_ROW_PITCH), :].astype(o_ref.dtype)


def peer_dense_weights(code, gate, tt=128, unroll=8):
    n, picks = code.shape
    tt = min(tt, n)
    nk = PEER_N_KEYS
    kern = functools.partial(_peer_weights_kernel, unroll=unroll)
    return pl.pallas_call(
        kern,
        grid=(n // tt,),
        in_specs=[pl.BlockSpec((tt, picks), lambda i: (i, 0)),
                  pl.BlockSpec((tt, picks), lambda i: (i, 0))],
        out_specs=pl.BlockSpec((tt, nk * nk), lambda i: (i, 0)),
        out_shape=jax.ShapeDtypeStruct((n, nk * nk), BF16),
        scratch_shapes=[pltpu.VMEM((tt * W_ROW_PITCH, nk), F32)],
        compiler_params=_params(("parallel",)), name="peer_weights",
    )(code, gate)


def _peer_dense_kernel(x_ref, u_ref, v_ref, w_ref, o_ref):
    j = pl.program_id(1)

    @pl.when(j == 0)
    def _():
        o_ref[...] = jnp.zeros(o_ref.shape, o_ref.dtype)

    a = lax.dot_general(x_ref[...], u_ref[...], NT_DIMS, preferred_element_type=F32)
    act = 0.5 * a * (1.0 + lax.erf(a * math.sqrt(0.5))) * w_ref[...].astype(F32)
    o_ref[...] += jnp.dot(act.astype(BF16), v_ref[...], preferred_element_type=F32)


def peer_dense(x, u, v, w, tn=512, te=512):
    n, d = x.shape
    ne = u.shape[0]
    tn, te = min(tn, n), min(te, ne)
    return pl.pallas_call(
        _peer_dense_kernel,
        grid=(n // tn, ne // te),
        in_specs=[
            pl.BlockSpec((tn, d), lambda i, j: (i, 0)),
            pl.BlockSpec((te, d), lambda i, j: (j, 0)),
            pl.BlockSpec((te, d), lambda i, j: (j, 0)),
            pl.BlockSpec((tn, te), lambda i, j: (i, j)),
        ],
        out_specs=pl.BlockSpec((tn, d), lambda i, j: (i, 0)),
        out_shape=jax.ShapeDtypeStruct((n, d), F32),
        compiler_params=_params(("parallel", "arbitrary")), name="peer_dense",
    )(x, u, v, w)


def kernel(x, norm_mix, w_in, conv_w, conv_b, w_br_attn, w_br_conv, b_gate, rel_bias,
           w_out, norm_ffn, peer_w_q, peer_sub_keys, peer_u, peer_v, norm_final):
    batch, seq, d = x.shape
    n = batch * seq
    assert norm_mix.shape[0] == 1, "single-layer problem"
    h = x.reshape(n, d)

    hn = rmsnorm(h, norm_mix[0], BF16)
    proj = matmul(hn, w_in[0].astype(BF16), BF16, name="in_proj")
    attn = moba_attention(proj, rel_bias, batch, seq,
                          q_col=0, k_col=ATTN_WIDTH // HEAD_DIM, v_col=2 * ATTN_WIDTH // HEAD_DIM)
    cw = 512
    c0 = 3 * ATTN_WIDTH // cw
    conv = short_conv(proj, conv_w[0].reshape(CONV_K, CONV_WIDTH), conv_b[0].reshape(1, CONV_WIDTH),
                      batch, seq, cb_col=c0, cc_col=c0 + CONV_WIDTH // cw, cu_col=c0 + 2 * CONV_WIDTH // cw)
    tn = 1024
    g0 = (3 * ATTN_WIDTH + 3 * CONV_WIDTH) // tn
    merged = branch_merge(attn, conv, w_br_attn[0].astype(BF16), w_br_conv[0].astype(BF16),
                          proj, b_gate[0], ga_col=g0, gc_col=g0 + d // tn, tn=tn)
    h = matmul(merged, w_out[0].astype(BF16), F32, residual=h, name="out_proj")

    hn2 = rmsnorm(h, norm_ffn[0], BF16)
    q = matmul(hn2, peer_w_q[0].astype(BF16), F32, name="peer_query")
    code, gate = peer_route(q, peer_sub_keys[0])
    w = peer_dense_weights(code, gate)
    y = peer_dense(hn2, peer_u[0].astype(BF16), peer_v[0].astype(BF16), w)
    out = rmsnorm(h, norm_final, F32, add=y)
    return out.reshape(batch, seq, d)
```

```python
import functools
import math

import numpy as np
import jax
import jax.numpy as jnp
from jax import lax
from jax.experimental import pallas as pl
from jax.experimental.pallas import tpu as pltpu

F32 = jnp.float32
BF16 = jnp.bfloat16

D_MODEL = 4096
ATTN_HEADS = 16
HEAD_DIM = 128
ATTN_WIDTH = ATTN_HEADS * HEAD_DIM
MOBA_BLOCK = 256
MOBA_TOPK = 3
REL_BUCKETS = 32
REL_MAX_DIST = 128
CONV_WIDTH = 2048
CONV_K = 3
PEER_HEADS = 8
PEER_N_KEYS = 128
PEER_HALF = 128
PEER_TOPK = 16
NORM_EPS = 1e-6

V7X_VMEM_BYTES = 64 * 1024 * 1024
VMEM_LIMIT = V7X_VMEM_BYTES - 8 * 1024 * 1024
LANES = 128
SUBLANES = 8
NT_DIMS = (((1,), (1,)), ((), ()))


def _params(semantics):
    return pltpu.CompilerParams(dimension_semantics=semantics, vmem_limit_bytes=VMEM_LIMIT)


def _rmsnorm_kernel(x_ref, g_ref, o_ref):
    x = x_ref[...]
    ms = jnp.mean(x * x, axis=-1, keepdims=True)
    o_ref[...] = (x * lax.rsqrt(ms + NORM_EPS) * g_ref[...]).astype(o_ref.dtype)


def rmsnorm(x, g, out_dtype, rows=256):
    n, d = x.shape
    rows = min(rows, n)
    row_spec = pl.BlockSpec((rows, d), lambda i: (i, 0))
    return pl.pallas_call(
        _rmsnorm_kernel, grid=(n // rows,),
        in_specs=[row_spec, pl.BlockSpec((1, d), lambda i: (0, 0))], out_specs=row_spec,
        out_shape=jax.ShapeDtypeStruct((n, d), out_dtype),
        compiler_params=_params(("parallel",)), name="rmsnorm",
    )(x, g.reshape(1, d))


def _cast_weight_once(b_ref, b_scr):
    @pl.when(pl.program_id(1) == 0)
    def _():
        b_scr[...] = b_ref[...].astype(BF16)


def _matmul_kernel(a_ref, b_ref, o_ref, b_scr):
    _cast_weight_once(b_ref, b_scr)
    o_ref[...] = jnp.dot(a_ref[...], b_scr[...], preferred_element_type=F32).astype(o_ref.dtype)


def _matmul_res_kernel(a_ref, b_ref, r_ref, o_ref, b_scr):
    _cast_weight_once(b_ref, b_scr)
    acc = jnp.dot(a_ref[...], b_scr[...], preferred_element_type=F32)
    o_ref[...] = (r_ref[...] + acc).astype(o_ref.dtype)


def matmul(a, b, out_dtype, residual=None, tm=1024, tn=512, name="matmul"):
    m, k = a.shape
    _, n = b.shape
    tm, tn = min(tm, m), min(tn, n)
    a_spec = pl.BlockSpec((tm, k), lambda j, i: (i, 0))
    b_spec = pl.BlockSpec((k, tn), lambda j, i: (0, j))
    o_spec = pl.BlockSpec((tm, tn), lambda j, i: (i, j))
    if residual is None:
        kern, ins, specs = _matmul_kernel, (a, b), [a_spec, b_spec]
    else:
        kern, ins, specs = _matmul_res_kernel, (a, b, residual), [a_spec, b_spec, o_spec]
    return pl.pallas_call(
        kern, grid=(n // tn, m // tm), in_specs=specs, out_specs=o_spec,
        out_shape=jax.ShapeDtypeStruct((m, n), out_dtype),
        scratch_shapes=[pltpu.VMEM((k, tn), BF16)],
        compiler_params=_params(("parallel", "arbitrary")), name=name,
    )(*ins)


NORM_CHUNK_ROWS = 64


def _rmsnorm_chunks(load_rows, g_ref, dst_ref):
    def body(c, carry):
        rows = pl.ds(pl.multiple_of(c * NORM_CHUNK_ROWS, NORM_CHUNK_ROWS), NORM_CHUNK_ROWS)
        x = load_rows(rows)
        ms = jnp.mean(x * x, axis=-1, keepdims=True)
        dst_ref[rows, :] = (x * lax.rsqrt(ms + NORM_EPS) * g_ref[...]).astype(dst_ref.dtype)
        return carry

    lax.fori_loop(0, dst_ref.shape[0] // NORM_CHUNK_ROWS, body, 0)


def _norm_rows_once(x_ref, g_ref, hn_scr):
    @pl.when(pl.program_id(1) == 0)
    def _():
        _rmsnorm_chunks(lambda rows: x_ref[rows, :], g_ref, hn_scr)


def _norm_matmul_kernel(x_ref, g_ref, b_ref, o_ref, hn_scr):
    _norm_rows_once(x_ref, g_ref, hn_scr)
    o_ref[...] = jnp.dot(hn_scr[...], b_ref[...], preferred_element_type=F32).astype(o_ref.dtype)


def norm_matmul(x, g, b, out_dtype, tm=512, tn=1024, name="norm_matmul"):
    m, k = x.shape
    _, n = b.shape
    tm, tn = min(tm, m), min(tn, n)
    return pl.pallas_call(
        _norm_matmul_kernel, grid=(m // tm, n // tn),
        in_specs=[pl.BlockSpec((tm, k), lambda i, j: (i, 0)),
                  pl.BlockSpec((1, k), lambda i, j: (0, 0)),
                  pl.BlockSpec((k, tn), lambda i, j: (0, j))],
        out_specs=pl.BlockSpec((tm, tn), lambda i, j: (i, j)),
        out_shape=jax.ShapeDtypeStruct((m, n), out_dtype),
        scratch_shapes=[pltpu.VMEM((tm, k), BF16)],
        compiler_params=_params(("parallel", "arbitrary")), name=name,
    )(x, g.reshape(1, k), b)


def _t5_bucket_table(max_dist):
    dist = np.arange(max_dist, dtype=np.int32)
    max_exact = REL_BUCKETS // 2
    d32 = np.maximum(dist, 1).astype(np.float32)
    ratio = (np.log(d32 / np.float32(max_exact)) / np.float32(math.log(REL_MAX_DIST / max_exact))
             * np.float32(REL_BUCKETS - max_exact))
    large = max_exact + ratio.astype(np.int32)
    large = np.minimum(large, REL_BUCKETS - 1)
    return np.where(dist < max_exact, dist, large).astype(np.int32)


def _moba_bucket_tile():
    L = MOBA_BLOCK
    qi = np.arange(L)[:, None]
    col = np.arange(2 * L)[None, :]
    dist = np.maximum(qi - col + L, 0)
    return _t5_bucket_table(2 * L)[dist]


def _moba_kernel(tab_ref, q_ref, k_ref, v_ref, bucket_ref, o_ref, bias_scr, *, n_blocks):
    L = MOBA_BLOCK
    h = pl.program_id(0)
    b = pl.program_id(1)

    @pl.when(b == 0)
    def _():
        bk = bucket_ref[...]
        acc = jnp.zeros(bk.shape, F32)
        for kk in range(REL_BUCKETS):
            acc = jnp.where(bk == kk, tab_ref[kk, h], acc)
        bias_scr[...] = acc

    neg_inf = jnp.float32(-jnp.inf)
    far = tab_ref[REL_BUCKETS - 1, h]
    scale = HEAD_DIM ** -0.5
    k_means = [jnp.mean(k_ref[n * L:(n + 1) * L, :].astype(F32), axis=0, keepdims=True)
               for n in range(n_blocks)]
    pad = jnp.zeros((LANES - n_blocks, HEAD_DIM), F32)
    k_mean = jnp.concatenate(k_means + [pad], axis=0)
    row = lax.broadcasted_iota(jnp.int32, (L, L), 0)
    colk = lax.broadcasted_iota(jnp.int32, (L, L), 1)
    lane = lax.broadcasted_iota(jnp.int32, (L, LANES), 1)

    for qb in range(n_blocks):
        q = q_ref[qb * L:(qb + 1) * L, :]
        sel = None
        if qb > MOBA_TOPK:
            gate = lax.dot_general(q.astype(F32), k_mean, NT_DIMS,
                                   precision=lax.Precision.HIGHEST, preferred_element_type=F32)
            gm = jnp.where(lane < qb, gate, neg_inf)
            rank = jnp.zeros(gate.shape, F32)
            for m in range(qb):
                col = gm[:, m:m + 1]
                beats = (col > gm) | ((col == gm) & (lane > m))
                rank = rank + jnp.where(beats, 1.0, 0.0)
            sel = jnp.where(rank < MOBA_TOPK, 1.0, 0.0)
        logits = []
        for n in range(qb + 1):
            s = lax.dot_general(q, k_ref[n * L:(n + 1) * L, :], NT_DIMS, preferred_element_type=F32)
            if n == qb:
                logits.append(jnp.where(row >= colk, s * scale + bias_scr[:, L:2 * L], neg_inf))
            else:
                lg = s * scale + (bias_scr[:, 0:L] if n == qb - 1 else far)
                logits.append(lg if sel is None else jnp.where(sel[:, n:n + 1] > 0.5, lg, neg_inf))
        mx = logits[0].max(axis=-1, keepdims=True)
        for lg in logits[1:]:
            mx = jnp.maximum(mx, lg.max(axis=-1, keepdims=True))
        denom = jnp.zeros((L, 1), F32)
        acc = jnp.zeros((L, HEAD_DIM), F32)
        for n, lg in enumerate(logits):
            p = jnp.exp(lg - mx)
            denom = denom + p.sum(axis=-1, keepdims=True)
            acc = acc + jnp.dot(p.astype(BF16), v_ref[n * L:(n + 1) * L, :], preferred_element_type=F32)
        o_ref[qb * L:(qb + 1) * L, :] = (acc / denom).astype(o_ref.dtype)


def moba_attention(proj, rel_bias, batch, seq, q_col, k_col, v_col):
    L = MOBA_BLOCK
    n_blocks = seq // L
    bucket = jnp.asarray(_moba_bucket_tile())
    kern = functools.partial(_moba_kernel, n_blocks=n_blocks)
    return pl.pallas_call(
        kern,
        grid=(ATTN_HEADS, batch),
        in_specs=[
            pl.BlockSpec(memory_space=pltpu.SMEM),
            pl.BlockSpec((seq, HEAD_DIM), lambda h, b: (b, q_col + h)),
            pl.BlockSpec((seq, HEAD_DIM), lambda h, b: (b, k_col + h)),
            pl.BlockSpec((seq, HEAD_DIM), lambda h, b: (b, v_col + h)),
            pl.BlockSpec((L, 2 * L), lambda h, b: (0, 0)),
        ],
        out_specs=pl.BlockSpec((seq, HEAD_DIM), lambda h, b: (b, h)),
        out_shape=jax.ShapeDtypeStruct((batch * seq, ATTN_WIDTH), BF16),
        scratch_shapes=[pltpu.VMEM((L, 2 * L), F32)],
        compiler_params=_params(("arbitrary", "arbitrary")), name="moba",
    )(rel_bias, proj, proj, proj, bucket)


def _conv_kernel(cb_ref, cc_ref, cu_ref, w_ref, b_ref, o_ref):
    g = cc_ref[...].astype(F32) * cu_ref[...].astype(F32)
    row = lax.broadcasted_iota(jnp.int32, g.shape, 0)
    y = w_ref[CONV_K - 1:CONV_K, :] * g
    for s in range(1, CONV_K):
        shifted = jnp.where(row >= s, pltpu.roll(g, s, axis=0), 0.0)
        y = y + w_ref[CONV_K - 1 - s:CONV_K - s, :] * shifted
    o_ref[...] = (cb_ref[...].astype(F32) * (y + b_ref[...])).astype(o_ref.dtype)


def short_conv(proj, conv_w, conv_b, batch, seq, cb_col, cc_col, cu_col, cw=512):
    width = conv_w.shape[1]
    return pl.pallas_call(
        _conv_kernel,
        grid=(batch, width // cw),
        in_specs=[
            pl.BlockSpec((seq, cw), lambda b, c: (b, cb_col + c)),
            pl.BlockSpec((seq, cw), lambda b, c: (b, cc_col + c)),
            pl.BlockSpec((seq, cw), lambda b, c: (b, cu_col + c)),
            pl.BlockSpec((CONV_K, cw), lambda b, c: (0, c)),
            pl.BlockSpec((1, cw), lambda b, c: (0, c)),
        ],
        out_specs=pl.BlockSpec((seq, cw), lambda b, c: (b, c)),
        out_shape=jax.ShapeDtypeStruct((batch * seq, width), BF16),
        compiler_params=_params(("parallel", "parallel")), name="short_conv",
    )(proj, proj, proj, conv_w, conv_b)


def _merge_kernel(attn_ref, conv_ref, wa_ref, wc_ref, ga_ref, gc_ref, bg_ref, o_ref, wa_scr, wc_scr):
    _cast_weight_once(wa_ref, wa_scr)
    _cast_weight_once(wc_ref, wc_scr)
    za = jnp.dot(attn_ref[...], wa_scr[...], preferred_element_type=F32)
    zc = jnp.dot(conv_ref[...], wc_scr[...], preferred_element_type=F32)
    ga = jax.nn.sigmoid(ga_ref[...].astype(F32) + bg_ref[0:1, :])
    gc = jax.nn.sigmoid(gc_ref[...].astype(F32) + bg_ref[1:2, :])
    o_ref[...] = (ga * za + gc * zc).astype(o_ref.dtype)


def branch_merge(attn, conv, wa, wc, proj, b_gate, ga_col, gc_col, tm=1024, tn=512):
    m, ka = attn.shape
    kc = conv.shape[1]
    n = wa.shape[1]
    tm, tn = min(tm, m), min(tn, n)
    return pl.pallas_call(
        _merge_kernel,
        grid=(n // tn, m // tm),
        in_specs=[
            pl.BlockSpec((tm, ka), lambda j, i: (i, 0)),
            pl.BlockSpec((tm, kc), lambda j, i: (i, 0)),
            pl.BlockSpec((ka, tn), lambda j, i: (0, j)),
            pl.BlockSpec((kc, tn), lambda j, i: (0, j)),
            pl.BlockSpec((tm, tn), lambda j, i: (i, ga_col + j)),
            pl.BlockSpec((tm, tn), lambda j, i: (i, gc_col + j)),
            pl.BlockSpec((2, tn), lambda j, i: (0, j)),
        ],
        out_specs=pl.BlockSpec((tm, tn), lambda j, i: (i, j)),
        out_shape=jax.ShapeDtypeStruct((m, n), BF16),
        scratch_shapes=[pltpu.VMEM((ka, tn), BF16), pltpu.VMEM((kc, tn), BF16)],
        compiler_params=_params(("parallel", "arbitrary")), name="branch_merge",
    )(attn, conv, wa, wc, proj, proj, b_gate)


def _topk_rows(s, k):
    nrows, ncols = s.shape
    rowi = lax.broadcasted_iota(jnp.int32, s.shape, 0).astype(F32)
    slot = lax.broadcasted_iota(jnp.int32, (k, ncols), 0)
    vals = jnp.zeros((k, ncols), F32)
    rows = jnp.zeros((k, ncols), F32)
    for r in range(k):
        m = s.max(axis=0, keepdims=True)
        idx = jnp.where(s == m, rowi, float(nrows)).min(axis=0, keepdims=True)
        s = jnp.where(rowi == idx, -jnp.inf, s)
        vals = jnp.where(slot == r, m, vals)
        rows = jnp.where(slot == r, idx, rows)
    return vals, rows


def _pair_candidates(v1, i1, v2, i2):
    K = PEER_TOPK
    ncols = v1.shape[1]
    sub8 = lax.broadcasted_iota(jnp.int32, (SUBLANES, ncols), 0)
    sub16 = lax.broadcasted_iota(jnp.int32, (K, ncols), 0)
    nk = float(PEER_N_KEYS)
    big = float(K * K)
    vals = [v1[0:1] + v2]
    codes = [i1[0:1] * nk + i2]
    flat = [sub16.astype(F32)]
    for a in range(1, SUBLANES):
        ok = sub8 < K // (a + 1)
        vals.append(jnp.where(ok, v1[a:a + 1] + v2[0:SUBLANES], -jnp.inf))
        codes.append(i1[a:a + 1] * nk + i2[0:SUBLANES])
        flat.append(jnp.where(ok, (sub8 + a * K).astype(F32), big))
    vals.append(v1[SUBLANES:K] + v2[0:1])
    codes.append(i1[SUBLANES:K] * nk + i2[0:1])
    flat.append(((sub8 + SUBLANES) * K).astype(F32))
    return jnp.concatenate(vals, axis=0), jnp.concatenate(codes, axis=0), jnp.concatenate(flat, axis=0)


def _peer_route_kernel(q_ref, keys_ref, code_ref, gate_ref, s1_scr, s2_scr, code_scr, gate_scr):
    K = PEER_TOPK
    h = pl.program_id(1)
    q = q_ref[...]
    s1_scr[...] = lax.dot_general(keys_ref[0, 0], q[:, :PEER_HALF], NT_DIMS,
                                  precision=lax.Precision.HIGHEST, preferred_element_type=F32)
    s2_scr[...] = lax.dot_general(keys_ref[0, 1], q[:, PEER_HALF:], NT_DIMS,
                                  precision=lax.Precision.HIGHEST, preferred_element_type=F32)
    row0 = pl.multiple_of(h * K, K)

    def chunk(c, carry):
        off = pl.multiple_of(c * LANES, LANES)
        v1, i1 = _topk_rows(s1_scr[:, pl.ds(off, LANES)], K)
        v2, i2 = _topk_rows(s2_scr[:, pl.ds(off, LANES)], K)
        cand, code, flat = _pair_candidates(v1, i1, v2, i2)
        slot = lax.broadcasted_iota(jnp.int32, (K, LANES), 0)
        score = jnp.zeros((K, LANES), F32)
        picked = jnp.zeros((K, LANES), F32)
        for r in range(K):
            m = cand.max(axis=0, keepdims=True)
            first = jnp.where(cand == m, flat, float(K * K)).min(axis=0, keepdims=True)
            hit = flat == first
            ck = jnp.where(hit, code, -1.0).max(axis=0, keepdims=True)
            cand = jnp.where(hit, -jnp.inf, cand)
            score = jnp.where(slot == r, m, score)
            picked = jnp.where(slot == r, ck, picked)
        e = jnp.exp(score - score[0:1])
        code_scr[pl.ds(row0, K), pl.ds(off, LANES)] = picked
        gate_scr[pl.ds(row0, K), pl.ds(off, LANES)] = e / e.sum(axis=0, keepdims=True)
        return carry

    lax.fori_loop(0, q.shape[0] // LANES, chunk, 0)

    @pl.when(h == PEER_HEADS - 1)
    def _():
        code_ref[...] = code_scr[...].T.astype(jnp.int32)
        gate_ref[...] = gate_scr[...].T


def peer_route(q, sub_keys, tt=512):
    n = q.shape[0]
    tt = min(tt, n)
    qd = 2 * PEER_HALF
    picks = PEER_HEADS * PEER_TOPK
    out_spec = pl.BlockSpec((tt, picks), lambda i, h: (i, 0))
    return pl.pallas_call(
        _peer_route_kernel,
        grid=(n // tt, PEER_HEADS),
        in_specs=[
            pl.BlockSpec((tt, qd), lambda i, h: (i, h)),
            pl.BlockSpec((1, 2, PEER_N_KEYS, PEER_HALF), lambda i, h: (h, 0, 0, 0)),
        ],
        out_specs=[out_spec, out_spec],
        out_shape=[jax.ShapeDtypeStruct((n, picks), jnp.int32),
                   jax.ShapeDtypeStruct((n, picks), F32)],
        scratch_shapes=[pltpu.VMEM((PEER_N_KEYS, tt), F32), pltpu.VMEM((PEER_N_KEYS, tt), F32),
                        pltpu.VMEM((picks, tt), F32), pltpu.VMEM((picks, tt), F32)],
        compiler_params=_params(("parallel", "arbitrary")), name="peer_route",
    )(q, sub_keys)


W_ROW_PITCH = PEER_N_KEYS + SUBLANES


def _peer_weights_kernel(code_ref, gate_ref, o_ref, w_scr, *, unroll):
    nk = PEER_N_KEYS
    tt, picks = code_ref.shape
    sub = lax.broadcasted_iota(jnp.int32, (nk, picks), 0)
    zero = jnp.zeros((nk, picks), BF16)

    def one_hots(t):
        c = code_ref[pl.ds(t, 1), :]
        g = gate_ref[pl.ds(t, 1), :]
        e1 = lax.shift_right_logical(c, 7)
        e2 = lax.bitwise_and(c, nk - 1)
        at = jnp.where(sub == e1, g, 0.0).astype(BF16)
        bt = jnp.where(sub == e2, 1.0, 0.0).astype(BF16)
        return at, bt

    def body(p, carry):
        t = 2 * p
        at0, bt0 = one_hots(t)
        at1, bt1 = one_hots(t + 1)
        at = jnp.concatenate([at0, at1], axis=1)
        bt = jnp.concatenate([jnp.concatenate([bt0, zero], axis=1),
                              jnp.concatenate([zero, bt1], axis=1)], axis=0)
        w = lax.dot_general(at, bt, NT_DIMS, preferred_element_type=F32)
        row = pl.multiple_of(t * W_ROW_PITCH, SUBLANES)
        w_scr[pl.ds(row, nk), :] = w[:, :nk]
        w_scr[pl.ds(row + W_ROW_PITCH, nk), :] = w[:, nk:]
        return carry

    lax.fori_loop(0, tt // 2, body, 0, unroll=unroll)
    for e1 in range(nk):
        o_ref[:, e1 * nk:(e1 + 1) * nk] = w_scr[pl.ds(e1, tt, stride=W_ROW_PITCH), :].astype(o_ref.dtype)


def peer_dense_weights(code, gate, tt=128, unroll=64):
    n, picks = code.shape
    tt = min(tt, n)
    nk = PEER_N_KEYS
    kern = functools.partial(_peer_weights_kernel, unroll=unroll)
    return pl.pallas_call(
        kern,
        grid=(n // tt,),
        in_specs=[pl.BlockSpec((tt, picks), lambda i: (i, 0)),
                  pl.BlockSpec((tt, picks), lambda i: (i, 0))],
        out_specs=pl.BlockSpec((tt, nk * nk), lambda i: (i, 0)),
        out_shape=jax.ShapeDtypeStruct((n, nk * nk), BF16),
        scratch_shapes=[pltpu.VMEM((tt * W_ROW_PITCH, nk), F32)],
        compiler_params=_params(("parallel",)), name="peer_weights",
    )(code, gate)


def _peer_dense_kernel(h_ref, gin_ref, u_ref, v_ref, w_ref, g_ref, o_ref, x_scr, *, splits):
    j = pl.program_id(1)
    _norm_rows_once(h_ref, gin_ref, x_scr)

    @pl.when(j == 0)
    def _():
        o_ref[...] = jnp.zeros(o_ref.shape, F32)

    x = x_scr[...]
    te = u_ref.shape[0]
    part = None
    for s in range(splits):
        lo, hi = s * te // splits, (s + 1) * te // splits
        a = lax.dot_general(x, u_ref[lo:hi, :], NT_DIMS, preferred_element_type=F32)
        act = 0.5 * a * (1.0 + lax.erf(a * math.sqrt(0.5))) * w_ref[:, lo:hi].astype(F32)
        y = jnp.dot(act.astype(BF16), v_ref[lo:hi, :], preferred_element_type=F32)
        part = y if part is None else part + y
    o_ref[...] += part

    @pl.when(j == pl.num_programs(1) - 1)
    def _():
        _rmsnorm_chunks(lambda rows: h_ref[rows, :] + o_ref[rows, :], g_ref, o_ref)


def peer_dense_final(h, g_in, u, v, w, g_out, tn=512, te=512, splits=2):
    n, d = h.shape
    ne = u.shape[0]
    tn, te = min(tn, n), min(te, ne)
    kern = functools.partial(_peer_dense_kernel, splits=splits)
    return pl.pallas_call(
        kern,
        grid=(n // tn, ne // te),
        in_specs=[
            pl.BlockSpec((tn, d), lambda i, j: (i, 0), pipeline_mode=pl.Buffered(1)),
            pl.BlockSpec((1, d), lambda i, j: (0, 0)),
            pl.BlockSpec((te, d), lambda i, j: (j, 0)),
            pl.BlockSpec((te, d), lambda i, j: (j, 0)),
            pl.BlockSpec((tn, te), lambda i, j: (i, j)),
            pl.BlockSpec((1, d), lambda i, j: (0, 0)),
        ],
        out_specs=pl.BlockSpec((tn, d), lambda i, j: (i, 0)),
        out_shape=jax.ShapeDtypeStruct((n, d), F32),
        scratch_shapes=[pltpu.VMEM((tn, d), BF16)],
        compiler_params=_params(("parallel", "arbitrary")), name="peer_dense",
    )(h, g_in.reshape(1, d), u, v, w, g_out.reshape(1, d))


def kernel(x, norm_mix, w_in, conv_w, conv_b, w_br_attn, w_br_conv, b_gate, rel_bias,
           w_out, norm_ffn, peer_w_q, peer_sub_keys, peer_u, peer_v, norm_final):
    batch, seq, d = x.shape
    n = batch * seq
    assert norm_mix.shape[0] == 1, "single-layer problem"
    h = x.reshape(n, d)

    hn = rmsnorm(h, norm_mix[0], BF16)
    proj = matmul(hn, w_in[0], BF16, name="in_proj")
    attn = moba_attention(proj, rel_bias, batch, seq,
                          q_col=0, k_col=ATTN_WIDTH // HEAD_DIM, v_col=2 * ATTN_WIDTH // HEAD_DIM)
    cw = 512
    c0 = 3 * ATTN_WIDTH // cw
    conv = short_conv(proj, conv_w[0].reshape(CONV_K, CONV_WIDTH), conv_b[0].reshape(1, CONV_WIDTH),
                      batch, seq, cb_col=c0, cc_col=c0 + CONV_WIDTH // cw, cu_col=c0 + 2 * CONV_WIDTH // cw)
    tn = 512
    g0 = (3 * ATTN_WIDTH + 3 * CONV_WIDTH) // tn
    merged = branch_merge(attn, conv, w_br_attn[0], w_br_conv[0],
                          proj, b_gate[0], ga_col=g0, gc_col=g0 + d // tn, tn=tn)
    h = matmul(merged, w_out[0], F32, residual=h, name="out_proj")

    q = norm_matmul(h, norm_ffn[0], peer_w_q[0].astype(BF16), F32, name="peer_query")
    code, gate = peer_route(q, peer_sub_keys[0])
    w = peer_dense_weights(code, gate)
    out = peer_dense_final(h, norm_ffn[0], peer_u[0].astype(BF16), peer_v[0].astype(BF16), w, norm_final)
    return out.reshape(batch, seq, d)
```

```python
import functools
import math

import numpy as np
import jax
import jax.numpy as jnp
from jax import lax
from jax.experimental import pallas as pl
from jax.experimental.pallas import tpu as pltpu

F32 = jnp.float32
BF16 = jnp.bfloat16

D_MODEL = 4096
ATTN_HEADS = 16
HEAD_DIM = 128
ATTN_WIDTH = ATTN_HEADS * HEAD_DIM
MOBA_BLOCK = 256
MOBA_TOPK = 3
REL_BUCKETS = 32
REL_MAX_DIST = 128
CONV_WIDTH = 2048
CONV_K = 3
PEER_HEADS = 8
PEER_N_KEYS = 128
PEER_HALF = 128
PEER_TOPK = 16
NORM_EPS = 1e-6

V7X_VMEM_BYTES = 64 * 1024 * 1024
VMEM_LIMIT = V7X_VMEM_BYTES - 8 * 1024 * 1024
LANES = 128
SUBLANES = 8
NT_DIMS = (((1,), (1,)), ((), ()))


def _params(semantics):
    return pltpu.CompilerParams(dimension_semantics=semantics, vmem_limit_bytes=VMEM_LIMIT)


def _rmsnorm_kernel(x_ref, g_ref, o_ref):
    x = x_ref[...]
    ms = jnp.mean(x * x, axis=-1, keepdims=True)
    o_ref[...] = (x * lax.rsqrt(ms + NORM_EPS) * g_ref[...]).astype(o_ref.dtype)


def rmsnorm(x, g, out_dtype, rows=256):
    n, d = x.shape
    rows = min(rows, n)
    row_spec = pl.BlockSpec((rows, d), lambda i: (i, 0))
    return pl.pallas_call(
        _rmsnorm_kernel, grid=(n // rows,),
        in_specs=[row_spec, pl.BlockSpec((1, d), lambda i: (0, 0))], out_specs=row_spec,
        out_shape=jax.ShapeDtypeStruct((n, d), out_dtype),
        compiler_params=_params(("parallel",)), name="rmsnorm",
    )(x, g.reshape(1, d))


def _cast_weight_once(b_ref, b_scr):
    @pl.when(pl.program_id(1) == 0)
    def _():
        b_scr[...] = b_ref[...].astype(BF16)


def _matmul_kernel(a_ref, b_ref, o_ref, b_scr):
    _cast_weight_once(b_ref, b_scr)
    o_ref[...] = jnp.dot(a_ref[...], b_scr[...], preferred_element_type=F32).astype(o_ref.dtype)


def _matmul_res_kernel(a_ref, b_ref, r_ref, o_ref, b_scr):
    _cast_weight_once(b_ref, b_scr)
    acc = jnp.dot(a_ref[...], b_scr[...], preferred_element_type=F32)
    o_ref[...] = (r_ref[...] + acc).astype(o_ref.dtype)


def matmul(a, b, out_dtype, residual=None, tm=1024, tn=512, name="matmul"):
    m, k = a.shape
    _, n = b.shape
    tm, tn = min(tm, m), min(tn, n)
    a_spec = pl.BlockSpec((tm, k), lambda j, i: (i, 0))
    b_spec = pl.BlockSpec((k, tn), lambda j, i: (0, j))
    o_spec = pl.BlockSpec((tm, tn), lambda j, i: (i, j))
    if residual is None:
        kern, ins, specs = _matmul_kernel, (a, b), [a_spec, b_spec]
    else:
        kern, ins, specs = _matmul_res_kernel, (a, b, residual), [a_spec, b_spec, o_spec]
    return pl.pallas_call(
        kern, grid=(n // tn, m // tm), in_specs=specs, out_specs=o_spec,
        out_shape=jax.ShapeDtypeStruct((m, n), out_dtype),
        scratch_shapes=[pltpu.VMEM((k, tn), BF16)],
        compiler_params=_params(("parallel", "arbitrary")), name=name,
    )(*ins)


NORM_CHUNK_ROWS = 64


def _rmsnorm_chunks(load_rows, g_ref, dst_ref):
    def body(c, carry):
        rows = pl.ds(pl.multiple_of(c * NORM_CHUNK_ROWS, NORM_CHUNK_ROWS), NORM_CHUNK_ROWS)
        x = load_rows(rows)
        ms = jnp.mean(x * x, axis=-1, keepdims=True)
        dst_ref[rows, :] = (x * lax.rsqrt(ms + NORM_EPS) * g_ref[...]).astype(dst_ref.dtype)
        return carry

    lax.fori_loop(0, dst_ref.shape[0] // NORM_CHUNK_ROWS, body, 0)


def _norm_rows_once(x_ref, g_ref, hn_scr):
    @pl.when(pl.program_id(1) == 0)
    def _():
        _rmsnorm_chunks(lambda rows: x_ref[rows, :], g_ref, hn_scr)


def _norm_matmul_kernel(x_ref, g_ref, b_ref, o_ref, hn_scr):
    _norm_rows_once(x_ref, g_ref, hn_scr)
    o_ref[...] = jnp.dot(hn_scr[...], b_ref[...], preferred_element_type=F32).astype(o_ref.dtype)


def norm_matmul(x, g, b, out_dtype, tm=512, tn=1024, name="norm_matmul"):
    m, k = x.shape
    _, n = b.shape
    tm, tn = min(tm, m), min(tn, n)
    return pl.pallas_call(
        _norm_matmul_kernel, grid=(m // tm, n // tn),
        in_specs=[pl.BlockSpec((tm, k), lambda i, j: (i, 0)),
                  pl.BlockSpec((1, k), lambda i, j: (0, 0)),
                  pl.BlockSpec((k, tn), lambda i, j: (0, j))],
        out_specs=pl.BlockSpec((tm, tn), lambda i, j: (i, j)),
        out_shape=jax.ShapeDtypeStruct((m, n), out_dtype),
        scratch_shapes=[pltpu.VMEM((tm, k), BF16)],
        compiler_params=_params(("parallel", "arbitrary")), name=name,
    )(x, g.reshape(1, k), b)


def _t5_bucket_table(max_dist):
    dist = np.arange(max_dist, dtype=np.int32)
    max_exact = REL_BUCKETS // 2
    d32 = np.maximum(dist, 1).astype(np.float32)
    ratio = (np.log(d32 / np.float32(max_exact)) / np.float32(math.log(REL_MAX_DIST / max_exact))
             * np.float32(REL_BUCKETS - max_exact))
    large = max_exact + ratio.astype(np.int32)
    large = np.minimum(large, REL_BUCKETS - 1)
    return np.where(dist < max_exact, dist, large).astype(np.int32)


def _moba_bucket_tile():
    L = MOBA_BLOCK
    qi = np.arange(L)[:, None]
    col = np.arange(2 * L)[None, :]
    dist = np.maximum(qi - col + L, 0)
    return _t5_bucket_table(2 * L)[dist]


def _moba_kernel(tab_ref, q_ref, k_ref, v_ref, bucket_ref, *rest, n_blocks, n_side, steps_per_table):
    side_in, o_ref, side_out, bias_scr = rest[:n_side], rest[n_side], rest[n_side + 1:-1], rest[-1]
    L = MOBA_BLOCK
    h = pl.program_id(0)
    b = pl.program_id(1)

    step = h * pl.num_programs(1) + b
    for t in range(n_side):
        @pl.when((step >= t * steps_per_table) & (step < (t + 1) * steps_per_table))
        def _(t=t):
            def body(c, carry):
                rows = pl.ds(pl.multiple_of(c * NORM_CHUNK_ROWS, NORM_CHUNK_ROWS), NORM_CHUNK_ROWS)
                side_out[t][rows, :] = side_in[t][rows, :].astype(BF16)
                return carry
            lax.fori_loop(0, side_in[t].shape[0] // NORM_CHUNK_ROWS, body, 0)

    @pl.when(b == 0)
    def _():
        bk = bucket_ref[...]
        acc = jnp.zeros(bk.shape, F32)
        for kk in range(REL_BUCKETS):
            acc = jnp.where(bk == kk, tab_ref[kk, h], acc)
        bias_scr[...] = acc

    neg_inf = jnp.float32(-jnp.inf)
    far = tab_ref[REL_BUCKETS - 1, h]
    scale = HEAD_DIM ** -0.5
    k_means = [jnp.mean(k_ref[n * L:(n + 1) * L, :].astype(F32), axis=0, keepdims=True)
               for n in range(n_blocks)]
    pad = jnp.zeros((LANES - n_blocks, HEAD_DIM), F32)
    k_mean = jnp.concatenate(k_means + [pad], axis=0)
    row = lax.broadcasted_iota(jnp.int32, (L, L), 0)
    colk = lax.broadcasted_iota(jnp.int32, (L, L), 1)
    lane = lax.broadcasted_iota(jnp.int32, (L, LANES), 1)

    for qb in range(n_blocks):
        q = q_ref[qb * L:(qb + 1) * L, :]
        sel = None
        if qb > MOBA_TOPK:
            gate = lax.dot_general(q.astype(F32), k_mean, NT_DIMS,
                                   precision=lax.Precision.HIGHEST, preferred_element_type=F32)
            gm = jnp.where(lane < qb, gate, neg_inf)
            rank = jnp.zeros(gate.shape, F32)
            for m in range(qb):
                col = gm[:, m:m + 1]
                beats = (col > gm) | ((col == gm) & (lane > m))
                rank = rank + jnp.where(beats, 1.0, 0.0)
            sel = jnp.where(rank < MOBA_TOPK, 1.0, 0.0)
        logits = []
        for n in range(qb + 1):
            s = lax.dot_general(q, k_ref[n * L:(n + 1) * L, :], NT_DIMS, preferred_element_type=F32)
            if n == qb:
                logits.append(jnp.where(row >= colk, s * scale + bias_scr[:, L:2 * L], neg_inf))
            else:
                lg = s * scale + (bias_scr[:, 0:L] if n == qb - 1 else far)
                logits.append(lg if sel is None else jnp.where(sel[:, n:n + 1] > 0.5, lg, neg_inf))
        mx = logits[0].max(axis=-1, keepdims=True)
        for lg in logits[1:]:
            mx = jnp.maximum(mx, lg.max(axis=-1, keepdims=True))
        denom = jnp.zeros((L, 1), F32)
        acc = jnp.zeros((L, HEAD_DIM), F32)
        for n, lg in enumerate(logits):
            p = jnp.exp(lg - mx)
            denom = denom + p.sum(axis=-1, keepdims=True)
            acc = acc + jnp.dot(p.astype(BF16), v_ref[n * L:(n + 1) * L, :], preferred_element_type=F32)
        o_ref[qb * L:(qb + 1) * L, :] = (acc / denom).astype(o_ref.dtype)


def moba_attention(proj, rel_bias, batch, seq, q_col, k_col, v_col, side_tables=()):
    L = MOBA_BLOCK
    n_blocks = seq // L
    bucket = jnp.asarray(_moba_bucket_tile())
    n_side = len(side_tables)
    n_steps = ATTN_HEADS * batch
    steps_per_table = n_steps // max(n_side, 1)
    side_specs, side_shapes = [], []
    for t, tab in enumerate(side_tables):
        rows, cols = tab.shape
        assert rows % steps_per_table == 0
        def index_map(h, b, t=t):
            return (jnp.clip(h * batch + b - t * steps_per_table, 0, steps_per_table - 1), 0)
        side_specs.append(pl.BlockSpec((rows // steps_per_table, cols), index_map))
        side_shapes.append(jax.ShapeDtypeStruct(tab.shape, BF16))
    kern = functools.partial(_moba_kernel, n_blocks=n_blocks, n_side=n_side, steps_per_table=steps_per_table)
    outs = pl.pallas_call(
        kern,
        grid=(ATTN_HEADS, batch),
        in_specs=[
            pl.BlockSpec(memory_space=pltpu.SMEM),
            pl.BlockSpec((seq, HEAD_DIM), lambda h, b: (b, q_col + h)),
            pl.BlockSpec((seq, HEAD_DIM), lambda h, b: (b, k_col + h)),
            pl.BlockSpec((seq, HEAD_DIM), lambda h, b: (b, v_col + h)),
            pl.BlockSpec((L, 2 * L), lambda h, b: (0, 0)),
        ] + side_specs,
        out_specs=[pl.BlockSpec((seq, HEAD_DIM), lambda h, b: (b, h))] + side_specs,
        out_shape=[jax.ShapeDtypeStruct((batch * seq, ATTN_WIDTH), BF16)] + side_shapes,
        scratch_shapes=[pltpu.VMEM((L, 2 * L), F32)],
        compiler_params=_params(("arbitrary", "arbitrary")), name="moba",
    )(rel_bias, proj, proj, proj, bucket, *side_tables)
    return outs[0], tuple(outs[1:])


def _conv_kernel(cb_ref, cc_ref, cu_ref, w_ref, b_ref, o_ref):
    g = cc_ref[...].astype(F32) * cu_ref[...].astype(F32)
    row = lax.broadcasted_iota(jnp.int32, g.shape, 0)
    y = w_ref[CONV_K - 1:CONV_K, :] * g
    for s in range(1, CONV_K):
        shifted = jnp.where(row >= s, pltpu.roll(g, s, axis=0), 0.0)
        y = y + w_ref[CONV_K - 1 - s:CONV_K - s, :] * shifted
    o_ref[...] = (cb_ref[...].astype(F32) * (y + b_ref[...])).astype(o_ref.dtype)


def short_conv(proj, conv_w, conv_b, batch, seq, cb_col, cc_col, cu_col, cw=512):
    width = conv_w.shape[1]
    return pl.pallas_call(
        _conv_kernel,
        grid=(batch, width // cw),
        in_specs=[
            pl.BlockSpec((seq, cw), lambda b, c: (b, cb_col + c)),
            pl.BlockSpec((seq, cw), lambda b, c: (b, cc_col + c)),
            pl.BlockSpec((seq, cw), lambda b, c: (b, cu_col + c)),
            pl.BlockSpec((CONV_K, cw), lambda b, c: (0, c)),
            pl.BlockSpec((1, cw), lambda b, c: (0, c)),
        ],
        out_specs=pl.BlockSpec((seq, cw), lambda b, c: (b, c)),
        out_shape=jax.ShapeDtypeStruct((batch * seq, width), BF16),
        compiler_params=_params(("parallel", "parallel")), name="short_conv",
    )(proj, proj, proj, conv_w, conv_b)


def _merge_kernel(attn_ref, conv_ref, wa_ref, wc_ref, ga_ref, gc_ref, bg_ref, o_ref, wa_scr, wc_scr):
    _cast_weight_once(wa_ref, wa_scr)
    _cast_weight_once(wc_ref, wc_scr)
    za = jnp.dot(attn_ref[...], wa_scr[...], preferred_element_type=F32)
    zc = jnp.dot(conv_ref[...], wc_scr[...], preferred_element_type=F32)
    ga = jax.nn.sigmoid(ga_ref[...].astype(F32) + bg_ref[0:1, :])
    gc = jax.nn.sigmoid(gc_ref[...].astype(F32) + bg_ref[1:2, :])
    o_ref[...] = (ga * za + gc * zc).astype(o_ref.dtype)


def branch_merge(attn, conv, wa, wc, proj, b_gate, ga_col, gc_col, tm=1024, tn=512):
    m, ka = attn.shape
    kc = conv.shape[1]
    n = wa.shape[1]
    tm, tn = min(tm, m), min(tn, n)
    return pl.pallas_call(
        _merge_kernel,
        grid=(n // tn, m // tm),
        in_specs=[
            pl.BlockSpec((tm, ka), lambda j, i: (i, 0)),
            pl.BlockSpec((tm, kc), lambda j, i: (i, 0)),
            pl.BlockSpec((ka, tn), lambda j, i: (0, j)),
            pl.BlockSpec((kc, tn), lambda j, i: (0, j)),
            pl.BlockSpec((tm, tn), lambda j, i: (i, ga_col + j)),
            pl.BlockSpec((tm, tn), lambda j, i: (i, gc_col + j)),
            pl.BlockSpec((2, tn), lambda j, i: (0, j)),
        ],
        out_specs=pl.BlockSpec((tm, tn), lambda j, i: (i, j)),
        out_shape=jax.ShapeDtypeStruct((m, n), BF16),
        scratch_shapes=[pltpu.VMEM((ka, tn), BF16), pltpu.VMEM((kc, tn), BF16)],
        compiler_params=_params(("parallel", "arbitrary")), name="branch_merge",
    )(attn, conv, wa, wc, proj, proj, b_gate)


def _topk_rows(s, k):
    nrows, ncols = s.shape
    rowi = lax.broadcasted_iota(jnp.int32, s.shape, 0).astype(F32)
    slot = lax.broadcasted_iota(jnp.int32, (k, ncols), 0)
    vals = jnp.zeros((k, ncols), F32)
    rows = jnp.zeros((k, ncols), F32)
    for r in range(k):
        m = s.max(axis=0, keepdims=True)
        idx = jnp.where(s == m, rowi, float(nrows)).min(axis=0, keepdims=True)
        s = jnp.where(rowi == idx, -jnp.inf, s)
        vals = jnp.where(slot == r, m, vals)
        rows = jnp.where(slot == r, idx, rows)
    return vals, rows


def _pair_candidates(v1, i1, v2, i2):
    K = PEER_TOPK
    ncols = v1.shape[1]
    sub8 = lax.broadcasted_iota(jnp.int32, (SUBLANES, ncols), 0)
    sub16 = lax.broadcasted_iota(jnp.int32, (K, ncols), 0)
    nk = float(PEER_N_KEYS)
    big = float(K * K)
    vals = [v1[0:1] + v2]
    codes = [i1[0:1] * nk + i2]
    flat = [sub16.astype(F32)]
    for a in range(1, SUBLANES):
        ok = sub8 < K // (a + 1)
        vals.append(jnp.where(ok, v1[a:a + 1] + v2[0:SUBLANES], -jnp.inf))
        codes.append(i1[a:a + 1] * nk + i2[0:SUBLANES])
        flat.append(jnp.where(ok, (sub8 + a * K).astype(F32), big))
    vals.append(v1[SUBLANES:K] + v2[0:1])
    codes.append(i1[SUBLANES:K] * nk + i2[0:1])
    flat.append(((sub8 + SUBLANES) * K).astype(F32))
    return jnp.concatenate(vals, axis=0), jnp.concatenate(codes, axis=0), jnp.concatenate(flat, axis=0)


def _peer_route_kernel(q_ref, keys_ref, code_ref, gate_ref, s1_scr, s2_scr, code_scr, gate_scr):
    K = PEER_TOPK
    h = pl.program_id(1)
    q = q_ref[...]
    s1_scr[...] = lax.dot_general(keys_ref[0, 0], q[:, :PEER_HALF], NT_DIMS,
                                  precision=lax.Precision.HIGHEST, preferred_element_type=F32)
    s2_scr[...] = lax.dot_general(keys_ref[0, 1], q[:, PEER_HALF:], NT_DIMS,
                                  precision=lax.Precision.HIGHEST, preferred_element_type=F32)
    row0 = pl.multiple_of(h * K, K)

    def chunk(c, carry):
        off = pl.multiple_of(c * LANES, LANES)
        v1, i1 = _topk_rows(s1_scr[:, pl.ds(off, LANES)], K)
        v2, i2 = _topk_rows(s2_scr[:, pl.ds(off, LANES)], K)
        cand, code, flat = _pair_candidates(v1, i1, v2, i2)
        slot = lax.broadcasted_iota(jnp.int32, (K, LANES), 0)
        score = jnp.zeros((K, LANES), F32)
        picked = jnp.zeros((K, LANES), F32)
        for r in range(K):
            m = cand.max(axis=0, keepdims=True)
            first = jnp.where(cand == m, flat, float(K * K)).min(axis=0, keepdims=True)
            hit = flat == first
            ck = jnp.where(hit, code, -1.0).max(axis=0, keepdims=True)
            cand = jnp.where(hit, -jnp.inf, cand)
            score = jnp.where(slot == r, m, score)
            picked = jnp.where(slot == r, ck, picked)
        e = jnp.exp(score - score[0:1])
        code_scr[pl.ds(row0, K), pl.ds(off, LANES)] = picked
        gate_scr[pl.ds(row0, K), pl.ds(off, LANES)] = e / e.sum(axis=0, keepdims=True)
        return carry

    lax.fori_loop(0, q.shape[0] // LANES, chunk, 0)

    @pl.when(h == PEER_HEADS - 1)
    def _():
        code_ref[...] = code_scr[...].T.astype(jnp.int32)
        gate_ref[...] = gate_scr[...].T


def peer_route(q, sub_keys, tt=512):
    n = q.shape[0]
    tt = min(tt, n)
    qd = 2 * PEER_HALF
    picks = PEER_HEADS * PEER_TOPK
    out_spec = pl.BlockSpec((tt, picks), lambda i, h: (i, 0))
    return pl.pallas_call(
        _peer_route_kernel,
        grid=(n // tt, PEER_HEADS),
        in_specs=[
            pl.BlockSpec((tt, qd), lambda i, h: (i, h)),
            pl.BlockSpec((1, 2, PEER_N_KEYS, PEER_HALF), lambda i, h: (h, 0, 0, 0)),
        ],
        out_specs=[out_spec, out_spec],
        out_shape=[jax.ShapeDtypeStruct((n, picks), jnp.int32),
                   jax.ShapeDtypeStruct((n, picks), F32)],
        scratch_shapes=[pltpu.VMEM((PEER_N_KEYS, tt), F32), pltpu.VMEM((PEER_N_KEYS, tt), F32),
                        pltpu.VMEM((picks, tt), F32), pltpu.VMEM((picks, tt), F32)],
        compiler_params=_params(("parallel", "arbitrary")), name="peer_route",
    )(q, sub_keys)


W_ROW_PITCH = PEER_N_KEYS + SUBLANES


def _peer_weights_kernel(code_ref, gate_ref, o_ref, w_scr, *, unroll):
    nk = PEER_N_KEYS
    tt, picks = code_ref.shape
    sub = lax.broadcasted_iota(jnp.int32, (nk, picks), 0)
    zero = jnp.zeros((nk, picks), BF16)

    def one_hots(t):
        c = code_ref[pl.ds(t, 1), :]
        g = gate_ref[pl.ds(t, 1), :]
        e1 = lax.shift_right_logical(c, 7)
        e2 = lax.bitwise_and(c, nk - 1)
        at = jnp.where(sub == e1, g, 0.0).astype(BF16)
        bt = jnp.where(sub == e2, 1.0, 0.0).astype(BF16)
        return at, bt

    def body(p, carry):
        t = 2 * p
        at0, bt0 = one_hots(t)
        at1, bt1 = one_hots(t + 1)
        at = jnp.concatenate([at0, at1], axis=1)
        bt = jnp.concatenate([jnp.concatenate([bt0, zero], axis=1),
                              jnp.concatenate([zero, bt1], axis=1)], axis=0)
        w = lax.dot_general(at, bt, NT_DIMS, preferred_element_type=F32)
        row = pl.multiple_of(t * W_ROW_PITCH, SUBLANES)
        w_scr[pl.ds(row, nk), :] = w[:, :nk]
        w_scr[pl.ds(row + W_ROW_PITCH, nk), :] = w[:, nk:]
        return carry

    lax.fori_loop(0, tt // 2, body, 0, unroll=unroll)
    for e1 in range(nk):
        o_ref[:, e1 * nk:(e1 + 1) * nk] = w_scr[pl.ds(e1, tt, stride=W_ROW_PITCH), :].astype(o_ref.dtype)


def peer_dense_weights(code, gate, tt=128, unroll=64):
    n, picks = code.shape
    tt = min(tt, n)
    nk = PEER_N_KEYS
    kern = functools.partial(_peer_weights_kernel, unroll=unroll)
    return pl.pallas_call(
        kern,
        grid=(n // tt,),
        in_specs=[pl.BlockSpec((tt, picks), lambda i: (i, 0)),
                  pl.BlockSpec((tt, picks), lambda i: (i, 0))],
        out_specs=pl.BlockSpec((tt, nk * nk), lambda i: (i, 0)),
        out_shape=jax.ShapeDtypeStruct((n, nk * nk), BF16),
        scratch_shapes=[pltpu.VMEM((tt * W_ROW_PITCH, nk), F32)],
        compiler_params=_params(("parallel",)), name="peer_weights",
    )(code, gate)


def _row_rms_scale(t, g):
    return t * lax.rsqrt(jnp.mean(t * t, axis=-1, keepdims=True) + NORM_EPS) * g


def _peer_dense_kernel(h_ref, gin_ref, u_ref, v_ref, w_ref, gout_ref, o_ref, acc_scr, x_scr, *, n_edge, n_main):
    j = pl.program_id(1)
    edge_rows = h_ref.shape[0]

    @pl.when(j < n_edge)
    def _():
        rows = pl.ds(pl.multiple_of(j * edge_rows, edge_rows), edge_rows)
        hc = h_ref[...]
        acc_scr[rows, :] = hc
        x_scr[rows, :] = _row_rms_scale(hc, gin_ref[...]).astype(x_scr.dtype)

    @pl.when((j >= n_edge) & (j < n_edge + n_main))
    def _():
        a = lax.dot_general(x_scr[...], u_ref[...], NT_DIMS, preferred_element_type=F32)
        act = 0.5 * a * (1.0 + lax.erf(a * math.sqrt(0.5))) * w_ref[...].astype(F32)
        acc_scr[...] += jnp.dot(act.astype(BF16), v_ref[...], preferred_element_type=F32)

    @pl.when(j >= n_edge + n_main)
    def _():
        r = j - (n_edge + n_main)
        rows = pl.ds(pl.multiple_of(r * edge_rows, edge_rows), edge_rows)
        o_ref[...] = _row_rms_scale(acc_scr[rows, :], gout_ref[...])


def peer_dense_final(h, g_in, u, v, w, g_out, tn=1024, te=512, edge_rows=128):
    n, d = h.shape
    ne = u.shape[0]
    tn, te = min(tn, n), min(te, ne)
    edge_rows = min(edge_rows, tn)
    n_edge, n_main = tn // edge_rows, ne // te

    def main_step(j):
        return jnp.clip(j - n_edge, 0, n_main - 1)

    kern = functools.partial(_peer_dense_kernel, n_edge=n_edge, n_main=n_main)
    return pl.pallas_call(
        kern,
        grid=(n // tn, n_edge + n_main + n_edge),
        in_specs=[
            pl.BlockSpec((edge_rows, d), lambda i, j: (i * n_edge + jnp.minimum(j, n_edge - 1), 0)),
            pl.BlockSpec((1, d), lambda i, j: (0, 0)),
            pl.BlockSpec((te, d), lambda i, j: (main_step(j), 0)),
            pl.BlockSpec((te, d), lambda i, j: (main_step(j), 0)),
            pl.BlockSpec((tn, te), lambda i, j: (i, main_step(j))),
            pl.BlockSpec((1, d), lambda i, j: (0, 0)),
        ],
        out_specs=pl.BlockSpec(
            (edge_rows, d), lambda i, j: (i * n_edge + jnp.clip(j - n_edge - n_main, 0, n_edge - 1), 0)),
        out_shape=jax.ShapeDtypeStruct((n, d), F32),
        scratch_shapes=[pltpu.VMEM((tn, d), F32), pltpu.VMEM((tn, d), BF16)],
        compiler_params=_params(("parallel", "arbitrary")), name="peer_dense",
    )(h, g_in.reshape(1, d), u, v, w, g_out.reshape(1, d))


def kernel(x, norm_mix, w_in, conv_w, conv_b, w_br_attn, w_br_conv, b_gate, rel_bias,
           w_out, norm_ffn, peer_w_q, peer_sub_keys, peer_u, peer_v, norm_final):
    batch, seq, d = x.shape
    n = batch * seq
    assert norm_mix.shape[0] == 1, "single-layer problem"
    h = x.reshape(n, d)

    hn = rmsnorm(h, norm_mix[0], BF16)
    proj = matmul(hn, w_in[0], BF16, name="in_proj")
    attn, (u_bf16, v_bf16) = moba_attention(
        proj, rel_bias, batch, seq, q_col=0, k_col=ATTN_WIDTH // HEAD_DIM, v_col=2 * ATTN_WIDTH // HEAD_DIM,
        side_tables=(peer_u[0], peer_v[0]))
    cw = 512
    c0 = 3 * ATTN_WIDTH // cw
    conv = short_conv(proj, conv_w[0].reshape(CONV_K, CONV_WIDTH), conv_b[0].reshape(1, CONV_WIDTH),
                      batch, seq, cb_col=c0, cc_col=c0 + CONV_WIDTH // cw, cu_col=c0 + 2 * CONV_WIDTH // cw)
    tn = 512
    g0 = (3 * ATTN_WIDTH + 3 * CONV_WIDTH) // tn
    merged = branch_merge(attn, conv, w_br_attn[0], w_br_conv[0],
                          proj, b_gate[0], ga_col=g0, gc_col=g0 + d // tn, tn=tn)
    h = matmul(merged, w_out[0], F32, residual=h, name="out_proj")

    q = norm_matmul(h, norm_ffn[0], peer_w_q[0].astype(BF16), F32, name="peer_query")
    code, gate = peer_route(q, peer_sub_keys[0])
    w = peer_dense_weights(code, gate)
    out = peer_dense_final(h, norm_ffn[0], u_bf16, v_bf16, w, norm_final)
    return out.reshape(batch, seq, d)
```

```python
import functools
import math

import numpy as np
import jax
import jax.numpy as jnp
from jax import lax
from jax.experimental import pallas as pl
from jax.experimental.pallas import tpu as pltpu

F32 = jnp.float32
BF16 = jnp.bfloat16

D_MODEL = 4096
ATTN_HEADS = 16
HEAD_DIM = 128
ATTN_WIDTH = ATTN_HEADS * HEAD_DIM
MOBA_BLOCK = 256
MOBA_TOPK = 3
REL_BUCKETS = 32
REL_MAX_DIST = 128
CONV_WIDTH = 2048
CONV_K = 3
PEER_HEADS = 8
PEER_N_KEYS = 128
PEER_HALF = 128
PEER_TOPK = 16
NORM_EPS = 1e-6

V7X_VMEM_BYTES = 64 * 1024 * 1024
VMEM_LIMIT = V7X_VMEM_BYTES - 8 * 1024 * 1024
LANES = 128
SUBLANES = 8
NT_DIMS = (((1,), (1,)), ((), ()))


def _params(semantics):
    return pltpu.CompilerParams(dimension_semantics=semantics, vmem_limit_bytes=VMEM_LIMIT)


def _rmsnorm_kernel(x_ref, g_ref, o_ref):
    x = x_ref[...]
    ms = jnp.mean(x * x, axis=-1, keepdims=True)
    o_ref[...] = (x * lax.rsqrt(ms + NORM_EPS) * g_ref[...]).astype(o_ref.dtype)


def rmsnorm(x, g, out_dtype, rows=256):
    n, d = x.shape
    rows = min(rows, n)
    row_spec = pl.BlockSpec((rows, d), lambda i: (i, 0))
    return pl.pallas_call(
        _rmsnorm_kernel, grid=(n // rows,),
        in_specs=[row_spec, pl.BlockSpec((1, d), lambda i: (0, 0))], out_specs=row_spec,
        out_shape=jax.ShapeDtypeStruct((n, d), out_dtype),
        compiler_params=_params(("parallel",)), name="rmsnorm",
    )(x, g.reshape(1, d))


def _cast_weight_once(b_ref, b_scr):
    @pl.when(pl.program_id(1) == 0)
    def _():
        b_scr[...] = b_ref[...].astype(BF16)


def _matmul_kernel(a_ref, b_ref, *rest, n_col_blocks, has_residual):
    r_ref = rest[0] if has_residual else None
    o_ref, b_scr = rest[-2], rest[-1]
    jj = pl.program_id(0)
    i = pl.program_id(1)
    kc = b_ref.shape[0]

    @pl.when(jj < n_col_blocks)
    def _():
        rows = pl.ds(pl.multiple_of(i * kc, kc), kc)
        b_scr[jj % 2, rows, :] = b_ref[...].astype(BF16)

    @pl.when(jj > 0)
    def _():
        acc = jnp.dot(a_ref[...], b_scr[(jj + 1) % 2], preferred_element_type=F32)
        if has_residual:
            acc = r_ref[...] + acc
        o_ref[...] = acc.astype(o_ref.dtype)


def matmul(a, b, out_dtype, residual=None, tm=1024, tn=1024, name="matmul"):
    m, k = a.shape
    _, n = b.shape
    tm, tn = min(tm, m), min(tn, n)
    ni, nj = m // tm, n // tn
    assert k % ni == 0
    kc = k // ni

    def out_index(jj, i):
        return (jnp.where(jj == 0, 0, i), jnp.maximum(jj - 1, 0))

    a_spec = pl.BlockSpec((tm, k), lambda jj, i: (jnp.where(jj == 0, 0, i), 0))
    b_spec = pl.BlockSpec((kc, tn), lambda jj, i: (jnp.where(jj < nj, i, ni - 1), jnp.minimum(jj, nj - 1)))
    o_spec = pl.BlockSpec((tm, tn), out_index)
    ins, specs = (a, b), [a_spec, b_spec]
    if residual is not None:
        ins, specs = ins + (residual,), specs + [o_spec]
    kern = functools.partial(_matmul_kernel, n_col_blocks=nj, has_residual=residual is not None)
    return pl.pallas_call(
        kern, grid=(nj + 1, ni), in_specs=specs, out_specs=o_spec,
        out_shape=jax.ShapeDtypeStruct((m, n), out_dtype),
        scratch_shapes=[pltpu.VMEM((2, k, tn), BF16)],
        compiler_params=_params(("arbitrary", "arbitrary")), name=name,
    )(*ins)


NORM_CHUNK_ROWS = 64


def _rmsnorm_chunks(load_rows, g_ref, dst_ref, unrolled=False):
    def body(c, carry):
        start = c * NORM_CHUNK_ROWS
        rows = pl.ds(start if unrolled else pl.multiple_of(start, NORM_CHUNK_ROWS), NORM_CHUNK_ROWS)
        x = load_rows(rows)
        ms = jnp.mean(x * x, axis=-1, keepdims=True)
        dst_ref[rows, :] = (x * lax.rsqrt(ms + NORM_EPS) * g_ref[...]).astype(dst_ref.dtype)
        return carry

    n_chunks = dst_ref.shape[0] // NORM_CHUNK_ROWS
    if unrolled:
        for c in range(n_chunks):
            body(c, 0)
    else:
        lax.fori_loop(0, n_chunks, body, 0)


def _norm_matmul_kernel(x_ref, g_ref, b_ref, o_ref, hn_scr, *, n_row_tiles):
    s = pl.program_id(0)

    def normalise():
        _rmsnorm_chunks(lambda rows: x_ref[rows, :], g_ref, hn_scr.at[s % 2], unrolled=True)

    def multiply():
        o_ref[...] = jnp.dot(hn_scr[(s + 1) % 2], b_ref[...], preferred_element_type=F32).astype(o_ref.dtype)

    @pl.when(s == 0)
    def _():
        normalise()

    @pl.when((s > 0) & (s < n_row_tiles))
    def _():
        multiply()
        normalise()

    @pl.when(s == n_row_tiles)
    def _():
        multiply()


def norm_matmul(x, g, b, out_dtype, tm=512, name="norm_matmul"):
    m, k = x.shape
    _, n = b.shape
    tm = min(tm, m)
    ni = m // tm
    kern = functools.partial(_norm_matmul_kernel, n_row_tiles=ni)
    return pl.pallas_call(
        kern, grid=(ni + 1,),
        in_specs=[pl.BlockSpec((tm, k), lambda s: (jnp.minimum(s, ni - 1), 0)),
                  pl.BlockSpec((1, k), lambda s: (0, 0)),
                  pl.BlockSpec((k, n), lambda s: (0, 0), pipeline_mode=pl.Buffered(1))],
        out_specs=pl.BlockSpec((tm, n), lambda s: (jnp.maximum(s - 1, 0), 0)),
        out_shape=jax.ShapeDtypeStruct((m, n), out_dtype),
        scratch_shapes=[pltpu.VMEM((2, tm, k), BF16)],
        compiler_params=_params(("arbitrary",)), name=name,
    )(x, g.reshape(1, k), b)


def _t5_bucket_table(max_dist):
    dist = np.arange(max_dist, dtype=np.int32)
    max_exact = REL_BUCKETS // 2
    d32 = np.maximum(dist, 1).astype(np.float32)
    ratio = (np.log(d32 / np.float32(max_exact)) / np.float32(math.log(REL_MAX_DIST / max_exact))
             * np.float32(REL_BUCKETS - max_exact))
    large = max_exact + ratio.astype(np.int32)
    large = np.minimum(large, REL_BUCKETS - 1)
    return np.where(dist < max_exact, dist, large).astype(np.int32)


def _moba_bucket_tile():
    L = MOBA_BLOCK
    qi = np.arange(L)[:, None]
    col = np.arange(2 * L)[None, :]
    dist = np.maximum(qi - col + L, 0)
    return _t5_bucket_table(2 * L)[dist]


def _moba_kernel(tab_ref, q_ref, k_ref, v_ref, bucket_ref, *rest, n_blocks, n_side, steps_per_table):
    side_in, o_ref, side_out, bias_scr = rest[:n_side], rest[n_side], rest[n_side + 1:-1], rest[-1]
    L = MOBA_BLOCK
    h = pl.program_id(0)
    b = pl.program_id(1)

    step = h * pl.num_programs(1) + b
    for t in range(n_side):
        @pl.when((step >= t * steps_per_table) & (step < (t + 1) * steps_per_table))
        def _(t=t):
            def body(c, carry):
                rows = pl.ds(pl.multiple_of(c * NORM_CHUNK_ROWS, NORM_CHUNK_ROWS), NORM_CHUNK_ROWS)
                side_out[t][rows, :] = side_in[t][rows, :].astype(BF16)
                return carry
            lax.fori_loop(0, side_in[t].shape[0] // NORM_CHUNK_ROWS, body, 0)

    @pl.when(b == 0)
    def _():
        bk = bucket_ref[...]
        acc = jnp.zeros(bk.shape, F32)
        for kk in range(REL_BUCKETS):
            acc = jnp.where(bk == kk, tab_ref[kk, h], acc)
        bias_scr[...] = acc

    neg_inf = jnp.float32(-jnp.inf)
    far = tab_ref[REL_BUCKETS - 1, h]
    scale = HEAD_DIM ** -0.5
    k_means = [jnp.mean(k_ref[n * L:(n + 1) * L, :].astype(F32), axis=0, keepdims=True)
               for n in range(n_blocks)]
    pad = jnp.zeros((LANES - n_blocks, HEAD_DIM), F32)
    k_mean = jnp.concatenate(k_means + [pad], axis=0)
    row = lax.broadcasted_iota(jnp.int32, (L, L), 0)
    colk = lax.broadcasted_iota(jnp.int32, (L, L), 1)
    lane = lax.broadcasted_iota(jnp.int32, (L, LANES), 1)

    for qb in range(n_blocks):
        q = q_ref[qb * L:(qb + 1) * L, :]
        sel = None
        if qb > MOBA_TOPK:
            gate = lax.dot_general(q.astype(F32), k_mean, NT_DIMS,
                                   precision=lax.Precision.HIGHEST, preferred_element_type=F32)
            gm = jnp.where(lane < qb, gate, neg_inf)
            rank = jnp.zeros(gate.shape, F32)
            for m in range(qb):
                col = gm[:, m:m + 1]
                beats = (col > gm) | ((col == gm) & (lane > m))
                rank = rank + jnp.where(beats, 1.0, 0.0)
            sel = jnp.where(rank < MOBA_TOPK, 1.0, 0.0)
        logits = []
        for n in range(qb + 1):
            s = lax.dot_general(q, k_ref[n * L:(n + 1) * L, :], NT_DIMS, preferred_element_type=F32)
            if n == qb:
                logits.append(jnp.where(row >= colk, s * scale + bias_scr[:, L:2 * L], neg_inf))
            else:
                lg = s * scale + (bias_scr[:, 0:L] if n == qb - 1 else far)
                logits.append(lg if sel is None else jnp.where(sel[:, n:n + 1] > 0.5, lg, neg_inf))
        mx = logits[0].max(axis=-1, keepdims=True)
        for lg in logits[1:]:
            mx = jnp.maximum(mx, lg.max(axis=-1, keepdims=True))
        denom = jnp.zeros((L, 1), F32)
        acc = jnp.zeros((L, HEAD_DIM), F32)
        for n, lg in enumerate(logits):
            p = jnp.exp(lg - mx)
            denom = denom + p.sum(axis=-1, keepdims=True)
            acc = acc + jnp.dot(p.astype(BF16), v_ref[n * L:(n + 1) * L, :], preferred_element_type=F32)
        o_ref[qb * L:(qb + 1) * L, :] = (acc / denom).astype(o_ref.dtype)


def moba_attention(proj, rel_bias, batch, seq, q_col, k_col, v_col, side_tables=()):
    L = MOBA_BLOCK
    n_blocks = seq // L
    bucket = jnp.asarray(_moba_bucket_tile())
    n_side = len(side_tables)
    n_steps = ATTN_HEADS * batch
    steps_per_table = n_steps // max(n_side, 1)
    side_specs, side_shapes = [], []
    for t, tab in enumerate(side_tables):
        rows, cols = tab.shape
        assert rows % steps_per_table == 0
        def index_map(h, b, t=t):
            return (jnp.clip(h * batch + b - t * steps_per_table, 0, steps_per_table - 1), 0)
        side_specs.append(pl.BlockSpec((rows // steps_per_table, cols), index_map))
        side_shapes.append(jax.ShapeDtypeStruct(tab.shape, BF16))
    kern = functools.partial(_moba_kernel, n_blocks=n_blocks, n_side=n_side, steps_per_table=steps_per_table)
    outs = pl.pallas_call(
        kern,
        grid=(ATTN_HEADS, batch),
        in_specs=[
            pl.BlockSpec(memory_space=pltpu.SMEM),
            pl.BlockSpec((seq, HEAD_DIM), lambda h, b: (b, q_col + h)),
            pl.BlockSpec((seq, HEAD_DIM), lambda h, b: (b, k_col + h)),
            pl.BlockSpec((seq, HEAD_DIM), lambda h, b: (b, v_col + h)),
            pl.BlockSpec((L, 2 * L), lambda h, b: (0, 0)),
        ] + side_specs,
        out_specs=[pl.BlockSpec((seq, HEAD_DIM), lambda h, b: (b, h))] + side_specs,
        out_shape=[jax.ShapeDtypeStruct((batch * seq, ATTN_WIDTH), BF16)] + side_shapes,
        scratch_shapes=[pltpu.VMEM((L, 2 * L), F32)],
        compiler_params=_params(("arbitrary", "arbitrary")), name="moba",
    )(rel_bias, proj, proj, proj, bucket, *side_tables)
    return outs[0], tuple(outs[1:])


def _conv_kernel(cb_ref, cc_ref, cu_ref, w_ref, b_ref, o_ref):
    g = cc_ref[...].astype(F32) * cu_ref[...].astype(F32)
    row = lax.broadcasted_iota(jnp.int32, g.shape, 0)
    y = w_ref[CONV_K - 1:CONV_K, :] * g
    for s in range(1, CONV_K):
        shifted = jnp.where(row >= s, pltpu.roll(g, s, axis=0), 0.0)
        y = y + w_ref[CONV_K - 1 - s:CONV_K - s, :] * shifted
    o_ref[...] = (cb_ref[...].astype(F32) * (y + b_ref[...])).astype(o_ref.dtype)


def short_conv(proj, conv_w, conv_b, batch, seq, cb_col, cc_col, cu_col, cw=512):
    width = conv_w.shape[1]
    return pl.pallas_call(
        _conv_kernel,
        grid=(batch, width // cw),
        in_specs=[
            pl.BlockSpec((seq, cw), lambda b, c: (b, cb_col + c)),
            pl.BlockSpec((seq, cw), lambda b, c: (b, cc_col + c)),
            pl.BlockSpec((seq, cw), lambda b, c: (b, cu_col + c)),
            pl.BlockSpec((CONV_K, cw), lambda b, c: (0, c)),
            pl.BlockSpec((1, cw), lambda b, c: (0, c)),
        ],
        out_specs=pl.BlockSpec((seq, cw), lambda b, c: (b, c)),
        out_shape=jax.ShapeDtypeStruct((batch * seq, width), BF16),
        compiler_params=_params(("parallel", "parallel")), name="short_conv",
    )(proj, proj, proj, conv_w, conv_b)


def _merge_kernel(attn_ref, conv_ref, wa_ref, wc_ref, ga_ref, gc_ref, bg_ref, o_ref, wa_scr, wc_scr):
    _cast_weight_once(wa_ref, wa_scr)
    _cast_weight_once(wc_ref, wc_scr)
    za = jnp.dot(attn_ref[...], wa_scr[...], preferred_element_type=F32)
    zc = jnp.dot(conv_ref[...], wc_scr[...], preferred_element_type=F32)
    ga = jax.nn.sigmoid(ga_ref[...].astype(F32) + bg_ref[0:1, :])
    gc = jax.nn.sigmoid(gc_ref[...].astype(F32) + bg_ref[1:2, :])
    o_ref[...] = (ga * za + gc * zc).astype(o_ref.dtype)


def branch_merge(attn, conv, wa, wc, proj, b_gate, ga_col, gc_col, tm=1024, tn=512):
    m, ka = attn.shape
    kc = conv.shape[1]
    n = wa.shape[1]
    tm, tn = min(tm, m), min(tn, n)
    return pl.pallas_call(
        _merge_kernel,
        grid=(n // tn, m // tm),
        in_specs=[
            pl.BlockSpec((tm, ka), lambda j, i: (i, 0)),
            pl.BlockSpec((tm, kc), lambda j, i: (i, 0)),
            pl.BlockSpec((ka, tn), lambda j, i: (0, j)),
            pl.BlockSpec((kc, tn), lambda j, i: (0, j)),
            pl.BlockSpec((tm, tn), lambda j, i: (i, ga_col + j)),
            pl.BlockSpec((tm, tn), lambda j, i: (i, gc_col + j)),
            pl.BlockSpec((2, tn), lambda j, i: (0, j)),
        ],
        out_specs=pl.BlockSpec((tm, tn), lambda j, i: (i, j)),
        out_shape=jax.ShapeDtypeStruct((m, n), BF16),
        scratch_shapes=[pltpu.VMEM((ka, tn), BF16), pltpu.VMEM((kc, tn), BF16)],
        compiler_params=_params(("parallel", "arbitrary")), name="branch_merge",
    )(attn, conv, wa, wc, proj, proj, b_gate)


def _topk_rows(s, k):
    nrows, ncols = s.shape
    rowi = lax.broadcasted_iota(jnp.int32, s.shape, 0).astype(F32)
    slot = lax.broadcasted_iota(jnp.int32, (k, ncols), 0)
    vals = jnp.zeros((k, ncols), F32)
    rows = jnp.zeros((k, ncols), F32)
    for r in range(k):
        m = s.max(axis=0, keepdims=True)
        idx = jnp.where(s == m, rowi, float(nrows)).min(axis=0, keepdims=True)
        s = jnp.where(rowi == idx, -jnp.inf, s)
        vals = jnp.where(slot == r, m, vals)
        rows = jnp.where(slot == r, idx, rows)
    return vals, rows


def _pair_candidates(v1, i1, v2, i2):
    K = PEER_TOPK
    ncols = v1.shape[1]
    sub8 = lax.broadcasted_iota(jnp.int32, (SUBLANES, ncols), 0)
    sub16 = lax.broadcasted_iota(jnp.int32, (K, ncols), 0)
    nk = float(PEER_N_KEYS)
    big = float(K * K)
    vals = [v1[0:1] + v2]
    codes = [i1[0:1] * nk + i2]
    flat = [sub16.astype(F32)]
    for a in range(1, SUBLANES):
        ok = sub8 < K // (a + 1)
        vals.append(jnp.where(ok, v1[a:a + 1] + v2[0:SUBLANES], -jnp.inf))
        codes.append(i1[a:a + 1] * nk + i2[0:SUBLANES])
        flat.append(jnp.where(ok, (sub8 + a * K).astype(F32), big))
    vals.append(v1[SUBLANES:K] + v2[0:1])
    codes.append(i1[SUBLANES:K] * nk + i2[0:1])
    flat.append(((sub8 + SUBLANES) * K).astype(F32))
    return jnp.concatenate(vals, axis=0), jnp.concatenate(codes, axis=0), jnp.concatenate(flat, axis=0)


def _peer_route_kernel(q_ref, keys_ref, code_ref, gate_ref, s1_scr, s2_scr, code_scr, gate_scr):
    K = PEER_TOPK
    h = pl.program_id(1)
    q = q_ref[...]
    s1_scr[...] = lax.dot_general(keys_ref[0, 0], q[:, :PEER_HALF], NT_DIMS,
                                  precision=lax.Precision.HIGHEST, preferred_element_type=F32)
    s2_scr[...] = lax.dot_general(keys_ref[0, 1], q[:, PEER_HALF:], NT_DIMS,
                                  precision=lax.Precision.HIGHEST, preferred_element_type=F32)
    row0 = pl.multiple_of(h * K, K)

    def chunk(c, carry):
        off = pl.multiple_of(c * LANES, LANES)
        v1, i1 = _topk_rows(s1_scr[:, pl.ds(off, LANES)], K)
        v2, i2 = _topk_rows(s2_scr[:, pl.ds(off, LANES)], K)
        cand, code, flat = _pair_candidates(v1, i1, v2, i2)
        slot = lax.broadcasted_iota(jnp.int32, (K, LANES), 0)
        score = jnp.zeros((K, LANES), F32)
        picked = jnp.zeros((K, LANES), F32)
        for r in range(K):
            m = cand.max(axis=0, keepdims=True)
            first = jnp.where(cand == m, flat, float(K * K)).min(axis=0, keepdims=True)
            hit = flat == first
            ck = jnp.where(hit, code, -1.0).max(axis=0, keepdims=True)
            cand = jnp.where(hit, -jnp.inf, cand)
            score = jnp.where(slot == r, m, score)
            picked = jnp.where(slot == r, ck, picked)
        e = jnp.exp(score - score[0:1])
        code_scr[pl.ds(row0, K), pl.ds(off, LANES)] = picked
        gate_scr[pl.ds(row0, K), pl.ds(off, LANES)] = e / e.sum(axis=0, keepdims=True)
        return carry

    lax.fori_loop(0, q.shape[0] // LANES, chunk, 0)

    @pl.when(h == PEER_HEADS - 1)
    def _():
        code_ref[...] = code_scr[...].T.astype(jnp.int32)
        gate_ref[...] = gate_scr[...].T


def peer_route(q, sub_keys, tt=512):
    n = q.shape[0]
    tt = min(tt, n)
    qd = 2 * PEER_HALF
    picks = PEER_HEADS * PEER_TOPK
    out_spec = pl.BlockSpec((tt, picks), lambda i, h: (i, 0))
    return pl.pallas_call(
        _peer_route_kernel,
        grid=(n // tt, PEER_HEADS),
        in_specs=[
            pl.BlockSpec((tt, qd), lambda i, h: (i, h)),
            pl.BlockSpec((1, 2, PEER_N_KEYS, PEER_HALF), lambda i, h: (h, 0, 0, 0)),
        ],
        out_specs=[out_spec, out_spec],
        out_shape=[jax.ShapeDtypeStruct((n, picks), jnp.int32),
                   jax.ShapeDtypeStruct((n, picks), F32)],
        scratch_shapes=[pltpu.VMEM((PEER_N_KEYS, tt), F32), pltpu.VMEM((PEER_N_KEYS, tt), F32),
                        pltpu.VMEM((picks, tt), F32), pltpu.VMEM((picks, tt), F32)],
        compiler_params=_params(("parallel", "arbitrary")), name="peer_route",
    )(q, sub_keys)


W_ROW_PITCH = PEER_N_KEYS + SUBLANES


def _peer_weights_kernel(code_ref, gate_ref, o_ref, w_scr, *, unroll):
    nk = PEER_N_KEYS
    tt, picks = code_ref.shape
    sub = lax.broadcasted_iota(jnp.int32, (nk, picks), 0)
    zero = jnp.zeros((nk, picks), BF16)

    def one_hots(t):
        c = code_ref[pl.ds(t, 1), :]
        g = gate_ref[pl.ds(t, 1), :]
        e1 = lax.shift_right_logical(c, 7)
        e2 = lax.bitwise_and(c, nk - 1)
        at = jnp.where(sub == e1, g, 0.0).astype(BF16)
        bt = jnp.where(sub == e2, 1.0, 0.0).astype(BF16)
        return at, bt

    def body(p, carry):
        t = 2 * p
        at0, bt0 = one_hots(t)
        at1, bt1 = one_hots(t + 1)
        at = jnp.concatenate([at0, at1], axis=1)
        bt = jnp.concatenate([jnp.concatenate([bt0, zero], axis=1),
                              jnp.concatenate([zero, bt1], axis=1)], axis=0)
        w = lax.dot_general(at, bt, NT_DIMS, preferred_element_type=F32)
        row = pl.multiple_of(t * W_ROW_PITCH, SUBLANES)
        w_scr[pl.ds(row, nk), :] = w[:, :nk]
        w_scr[pl.ds(row + W_ROW_PITCH, nk), :] = w[:, nk:]
        return carry

    lax.fori_loop(0, tt // 2, body, 0, unroll=unroll)
    for e1 in range(nk):
        o_ref[:, e1 * nk:(e1 + 1) * nk] = w_scr[pl.ds(e1, tt, stride=W_ROW_PITCH), :].astype(o_ref.dtype)


def peer_dense_weights(code, gate, tt=128, unroll=64):
    n, picks = code.shape
    tt = min(tt, n)
    nk = PEER_N_KEYS
    kern = functools.partial(_peer_weights_kernel, unroll=unroll)
    return pl.pallas_call(
        kern,
        grid=(n // tt,),
        in_specs=[pl.BlockSpec((tt, picks), lambda i: (i, 0)),
                  pl.BlockSpec((tt, picks), lambda i: (i, 0))],
        out_specs=pl.BlockSpec((tt, nk * nk), lambda i: (i, 0)),
        out_shape=jax.ShapeDtypeStruct((n, nk * nk), BF16),
        scratch_shapes=[pltpu.VMEM((tt * W_ROW_PITCH, nk), F32)],
        compiler_params=_params(("parallel",)), name="peer_weights",
    )(code, gate)


def _row_rms_scale(t, g):
    return t * lax.rsqrt(jnp.mean(t * t, axis=-1, keepdims=True) + NORM_EPS) * g


def _peer_dense_kernel(h_ref, gin_ref, u_ref, v_ref, w_ref, gout_ref, o_ref, acc_scr, x_scr, *, n_edge, n_main):
    j = pl.program_id(1)
    edge_rows = h_ref.shape[0]

    @pl.when(j < n_edge)
    def _():
        rows = pl.ds(pl.multiple_of(j * edge_rows, edge_rows), edge_rows)
        hc = h_ref[...]
        acc_scr[rows, :] = hc
        x_scr[rows, :] = _row_rms_scale(hc, gin_ref[...]).astype(x_scr.dtype)

    @pl.when((j >= n_edge) & (j < n_edge + n_main))
    def _():
        a = lax.dot_general(x_scr[...], u_ref[...], NT_DIMS, preferred_element_type=F32)
        act = 0.5 * a * (1.0 + lax.erf(a * math.sqrt(0.5))) * w_ref[...].astype(F32)
        acc_scr[...] += jnp.dot(act.astype(BF16), v_ref[...], preferred_element_type=F32)

    @pl.when(j >= n_edge + n_main)
    def _():
        r = j - (n_edge + n_main)
        rows = pl.ds(pl.multiple_of(r * edge_rows, edge_rows), edge_rows)
        o_ref[...] = _row_rms_scale(acc_scr[rows, :], gout_ref[...])


def peer_dense_final(h, g_in, u, v, w, g_out, tn=1024, te=512, edge_rows=128):
    n, d = h.shape
    ne = u.shape[0]
    tn, te = min(tn, n), min(te, ne)
    edge_rows = min(edge_rows, tn)
    n_edge, n_main = tn // edge_rows, ne // te

    def main_step(j):
        return jnp.clip(j - n_edge, 0, n_main - 1)

    kern = functools.partial(_peer_dense_kernel, n_edge=n_edge, n_main=n_main)
    return pl.pallas_call(
        kern,
        grid=(n // tn, n_edge + n_main + n_edge),
        in_specs=[
            pl.BlockSpec((edge_rows, d), lambda i, j: (i * n_edge + jnp.minimum(j, n_edge - 1), 0)),
            pl.BlockSpec((1, d), lambda i, j: (0, 0)),
            pl.BlockSpec((te, d), lambda i, j: (main_step(j), 0)),
            pl.BlockSpec((te, d), lambda i, j: (main_step(j), 0)),
            pl.BlockSpec((tn, te), lambda i, j: (i, main_step(j))),
            pl.BlockSpec((1, d), lambda i, j: (0, 0)),
        ],
        out_specs=pl.BlockSpec(
            (edge_rows, d), lambda i, j: (i * n_edge + jnp.clip(j - n_edge - n_main, 0, n_edge - 1), 0)),
        out_shape=jax.ShapeDtypeStruct((n, d), F32),
        scratch_shapes=[pltpu.VMEM((tn, d), F32), pltpu.VMEM((tn, d), BF16)],
        compiler_params=_params(("parallel", "arbitrary")), name="peer_dense",
    )(h, g_in.reshape(1, d), u, v, w, g_out.reshape(1, d))


def kernel(x, norm_mix, w_in, conv_w, conv_b, w_br_attn, w_br_conv, b_gate, rel_bias,
           w_out, norm_ffn, peer_w_q, peer_sub_keys, peer_u, peer_v, norm_final):
    batch, seq, d = x.shape
    n = batch * seq
    assert norm_mix.shape[0] == 1, "single-layer problem"
    h = x.reshape(n, d)

    hn = rmsnorm(h, norm_mix[0], BF16)
    proj = matmul(hn, w_in[0], BF16, name="in_proj")
    attn, (u_bf16, v_bf16) = moba_attention(
        proj, rel_bias, batch, seq, q_col=0, k_col=ATTN_WIDTH // HEAD_DIM, v_col=2 * ATTN_WIDTH // HEAD_DIM,
        side_tables=(peer_u[0], peer_v[0]))
    cw = 512
    c0 = 3 * ATTN_WIDTH // cw
    conv = short_conv(proj, conv_w[0].reshape(CONV_K, CONV_WIDTH), conv_b[0].reshape(1, CONV_WIDTH),
                      batch, seq, cb_col=c0, cc_col=c0 + CONV_WIDTH // cw, cu_col=c0 + 2 * CONV_WIDTH // cw)
    tn = 512
    g0 = (3 * ATTN_WIDTH + 3 * CONV_WIDTH) // tn
    merged = branch_merge(attn, conv, w_br_attn[0], w_br_conv[0],
                          proj, b_gate[0], ga_col=g0, gc_col=g0 + d // tn, tn=tn)
    h = matmul(merged, w_out[0], F32, residual=h, tn=512, name="out_proj")

    q = norm_matmul(h, norm_ffn[0], peer_w_q[0].astype(BF16), F32, name="peer_query")
    code, gate = peer_route(q, peer_sub_keys[0])
    w = peer_dense_weights(code, gate)
    out = peer_dense_final(h, norm_ffn[0], u_bf16, v_bf16, w, norm_final)
    return out.reshape(batch, seq, d)
```

```python
import functools
import math

import numpy as np
import jax
import jax.numpy as jnp
from jax import lax
from jax.experimental import pallas as pl
from jax.experimental.pallas import tpu as pltpu

F32 = jnp.float32
BF16 = jnp.bfloat16

D_MODEL = 4096
ATTN_HEADS = 16
HEAD_DIM = 128
ATTN_WIDTH = ATTN_HEADS * HEAD_DIM
MOBA_BLOCK = 256
MOBA_TOPK = 3
REL_BUCKETS = 32
REL_MAX_DIST = 128
CONV_WIDTH = 2048
CONV_K = 3
PEER_HEADS = 8
PEER_N_KEYS = 128
PEER_HALF = 128
PEER_TOPK = 16
NORM_EPS = 1e-6

V7X_VMEM_BYTES = 64 * 1024 * 1024
VMEM_LIMIT = V7X_VMEM_BYTES - 8 * 1024 * 1024
LANES = 128
SUBLANES = 8
NT_DIMS = (((1,), (1,)), ((), ()))
LOG2_E = math.log2(math.e)


def _params(semantics):
    return pltpu.CompilerParams(dimension_semantics=semantics, vmem_limit_bytes=VMEM_LIMIT)


def _rmsnorm_kernel(x_ref, g_ref, o_ref):
    x = x_ref[...]
    ms = jnp.mean(x * x, axis=-1, keepdims=True)
    o_ref[...] = (x * lax.rsqrt(ms + NORM_EPS) * g_ref[...]).astype(o_ref.dtype)


def rmsnorm(x, g, out_dtype, rows=256):
    n, d = x.shape
    rows = min(rows, n)
    row_spec = pl.BlockSpec((rows, d), lambda i: (i, 0))
    return pl.pallas_call(
        _rmsnorm_kernel, grid=(n // rows,),
        in_specs=[row_spec, pl.BlockSpec((1, d), lambda i: (0, 0))], out_specs=row_spec,
        out_shape=jax.ShapeDtypeStruct((n, d), out_dtype),
        compiler_params=_params(("parallel",)), name="rmsnorm",
    )(x, g.reshape(1, d))


def _cast_weight_once(b_ref, b_scr):
    @pl.when(pl.program_id(1) == 0)
    def _():
        b_scr[...] = b_ref[...].astype(BF16)


def _matmul_kernel(a_ref, b_ref, *rest, n_col_blocks, has_residual):
    r_ref = rest[0] if has_residual else None
    o_ref, b_scr = rest[-2], rest[-1]
    jj = pl.program_id(0)
    i = pl.program_id(1)
    kc = b_ref.shape[0]

    @pl.when(jj < n_col_blocks)
    def _():
        rows = pl.ds(pl.multiple_of(i * kc, kc), kc)
        b_scr[jj % 2, rows, :] = b_ref[...].astype(BF16)

    @pl.when(jj > 0)
    def _():
        acc = jnp.dot(a_ref[...], b_scr[(jj + 1) % 2], preferred_element_type=F32)
        if has_residual:
            acc = r_ref[...] + acc
        o_ref[...] = acc.astype(o_ref.dtype)


def matmul(a, b, out_dtype, residual=None, tm=1024, tn=1024, name="matmul"):
    m, k = a.shape
    _, n = b.shape
    tm, tn = min(tm, m), min(tn, n)
    ni, nj = m // tm, n // tn
    assert k % ni == 0
    kc = k // ni

    def out_index(jj, i):
        return (jnp.where(jj == 0, 0, i), jnp.maximum(jj - 1, 0))

    a_spec = pl.BlockSpec((tm, k), lambda jj, i: (jnp.where(jj == 0, 0, i), 0))
    b_spec = pl.BlockSpec((kc, tn), lambda jj, i: (jnp.where(jj < nj, i, ni - 1), jnp.minimum(jj, nj - 1)))
    o_spec = pl.BlockSpec((tm, tn), out_index)
    ins, specs = (a, b), [a_spec, b_spec]
    if residual is not None:
        ins, specs = ins + (residual,), specs + [o_spec]
    kern = functools.partial(_matmul_kernel, n_col_blocks=nj, has_residual=residual is not None)
    return pl.pallas_call(
        kern, grid=(nj + 1, ni), in_specs=specs, out_specs=o_spec,
        out_shape=jax.ShapeDtypeStruct((m, n), out_dtype),
        scratch_shapes=[pltpu.VMEM((2, k, tn), BF16)],
        compiler_params=_params(("arbitrary", "arbitrary")), name=name,
    )(*ins)


NORM_CHUNK_ROWS = 64


def _rmsnorm_chunks(load_rows, g_ref, dst_ref, unrolled=False):
    def body(c, carry):
        start = c * NORM_CHUNK_ROWS
        rows = pl.ds(start if unrolled else pl.multiple_of(start, NORM_CHUNK_ROWS), NORM_CHUNK_ROWS)
        x = load_rows(rows)
        ms = jnp.mean(x * x, axis=-1, keepdims=True)
        dst_ref[rows, :] = (x * lax.rsqrt(ms + NORM_EPS) * g_ref[...]).astype(dst_ref.dtype)
        return carry

    n_chunks = dst_ref.shape[0] // NORM_CHUNK_ROWS
    if unrolled:
        for c in range(n_chunks):
            body(c, 0)
    else:
        lax.fori_loop(0, n_chunks, body, 0)


def _norm_matmul_kernel(x_ref, g_ref, b_ref, o_ref, hn_scr, *, n_row_tiles):
    s = pl.program_id(0)

    def normalise():
        _rmsnorm_chunks(lambda rows: x_ref[rows, :], g_ref, hn_scr.at[s % 2], unrolled=True)

    def multiply():
        o_ref[...] = jnp.dot(hn_scr[(s + 1) % 2], b_ref[...], preferred_element_type=F32).astype(o_ref.dtype)

    @pl.when(s == 0)
    def _():
        normalise()

    @pl.when((s > 0) & (s < n_row_tiles))
    def _():
        multiply()
        normalise()

    @pl.when(s == n_row_tiles)
    def _():
        multiply()


def norm_matmul(x, g, b, out_dtype, tm=512, name="norm_matmul"):
    m, k = x.shape
    _, n = b.shape
    tm = min(tm, m)
    ni = m // tm
    kern = functools.partial(_norm_matmul_kernel, n_row_tiles=ni)
    return pl.pallas_call(
        kern, grid=(ni + 1,),
        in_specs=[pl.BlockSpec((tm, k), lambda s: (jnp.minimum(s, ni - 1), 0)),
                  pl.BlockSpec((1, k), lambda s: (0, 0)),
                  pl.BlockSpec((k, n), lambda s: (0, 0), pipeline_mode=pl.Buffered(1))],
        out_specs=pl.BlockSpec((tm, n), lambda s: (jnp.maximum(s - 1, 0), 0)),
        out_shape=jax.ShapeDtypeStruct((m, n), out_dtype),
        scratch_shapes=[pltpu.VMEM((2, tm, k), BF16)],
        compiler_params=_params(("arbitrary",)), name=name,
    )(x, g.reshape(1, k), b)


def _t5_bucket_table(max_dist):
    dist = np.arange(max_dist, dtype=np.int32)
    max_exact = REL_BUCKETS // 2
    d32 = np.maximum(dist, 1).astype(np.float32)
    ratio = (np.log(d32 / np.float32(max_exact)) / np.float32(math.log(REL_MAX_DIST / max_exact))
             * np.float32(REL_BUCKETS - max_exact))
    large = max_exact + ratio.astype(np.int32)
    large = np.minimum(large, REL_BUCKETS - 1)
    return np.where(dist < max_exact, dist, large).astype(np.int32)


def _moba_bucket_tile():
    L = MOBA_BLOCK
    key = np.arange(2 * L)[:, None]
    qi = np.arange(L)[None, :]
    dist = np.maximum(qi - key + L, 0)
    return _t5_bucket_table(2 * L)[dist]


def _moba_kernel(tab_ref, q_ref, k_ref, v_ref, bucket_ref, *rest, n_blocks, n_side, steps_per_table):
    side_in, o_ref, side_out, bias_scr = rest[:n_side], rest[n_side], rest[n_side + 1:-1], rest[-1]
    L = MOBA_BLOCK
    h = pl.program_id(0)
    b = pl.program_id(1)

    step = h * pl.num_programs(1) + b
    for t in range(n_side):
        @pl.when((step >= t * steps_per_table) & (step < (t + 1) * steps_per_table))
        def _(t=t):
            def body(c, carry):
                rows = pl.ds(pl.multiple_of(c * NORM_CHUNK_ROWS, NORM_CHUNK_ROWS), NORM_CHUNK_ROWS)
                side_out[t][rows, :] = side_in[t][rows, :].astype(BF16)
                return carry
            lax.fori_loop(0, side_in[t].shape[0] // NORM_CHUNK_ROWS, body, 0)

    @pl.when(b == 0)
    def _():
        bk = bucket_ref[...]
        acc = jnp.zeros(bk.shape, F32)
        for kk in range(REL_BUCKETS):
            acc = jnp.where(bk == kk, tab_ref[kk, h], acc)
        bias_scr[...] = acc * LOG2_E

    neg_inf = jnp.float32(-jnp.inf)
    far = tab_ref[REL_BUCKETS - 1, h] * LOG2_E
    scale = HEAD_DIM ** -0.5 * LOG2_E
    k_means = [jnp.mean(k_ref[n * L:(n + 1) * L, :].astype(F32), axis=0, keepdims=True)
               for n in range(n_blocks)]
    pad = jnp.zeros((SUBLANES - n_blocks % SUBLANES, HEAD_DIM), F32) if n_blocks % SUBLANES else None
    k_mean = jnp.concatenate(k_means + ([pad] if pad is not None else []), axis=0)
    v_t = v_ref[...].astype(F32).T.astype(BF16)
    key_i = lax.broadcasted_iota(jnp.int32, (L, L), 0)
    qry_i = lax.broadcasted_iota(jnp.int32, (L, L), 1)

    for qb in range(n_blocks):
        q = q_ref[qb * L:(qb + 1) * L, :]
        sel = None
        if qb > MOBA_TOPK:
            gate = lax.dot_general(k_mean, q.astype(F32), NT_DIMS,
                                   precision=lax.Precision.HIGHEST, preferred_element_type=F32)
            g = [gate[m:m + 1, :] for m in range(qb)]
            sel = []
            for n in range(qb):
                rank = jnp.zeros((1, L), F32)
                for m in range(qb):
                    if m != n:
                        beats = (g[m] > g[n]) | (g[m] == g[n]) if m < n else (g[m] > g[n])
                        rank = rank + jnp.where(beats, 1.0, 0.0)
                sel.append(jnp.where(rank < MOBA_TOPK, 1.0, 0.0))
        logits = []
        for n in range(qb + 1):
            s = lax.dot_general(k_ref[n * L:(n + 1) * L, :], q, NT_DIMS, preferred_element_type=F32)
            if n == qb:
                logits.append(jnp.where(qry_i >= key_i, s * scale + bias_scr[L:2 * L, :], neg_inf))
            else:
                lg = s * scale + (bias_scr[0:L, :] if n == qb - 1 else far)
                logits.append(lg if sel is None else jnp.where(sel[n] > 0.5, lg, neg_inf))
        mx = logits[0].max(axis=0, keepdims=True)
        for lg in logits[1:]:
            mx = jnp.maximum(mx, lg.max(axis=0, keepdims=True))
        denom = jnp.zeros((1, L), F32)
        acc = jnp.zeros((HEAD_DIM, L), F32)
        for n, lg in enumerate(logits):
            p = jnp.exp2(lg - mx)
            denom = denom + p.sum(axis=0, keepdims=True)
            acc = acc + jnp.dot(v_t[:, n * L:(n + 1) * L], p.astype(BF16), preferred_element_type=F32)
        o_ref[qb * L:(qb + 1) * L, :] = (acc / denom).T.astype(o_ref.dtype)


def moba_attention(proj, rel_bias, batch, seq, q_col, k_col, v_col, side_tables=()):
    L = MOBA_BLOCK
    n_blocks = seq // L
    bucket = jnp.asarray(_moba_bucket_tile())
    n_side = len(side_tables)
    n_steps = ATTN_HEADS * batch
    steps_per_table = n_steps // max(n_side, 1)
    side_specs, side_shapes = [], []
    for t, tab in enumerate(side_tables):
        rows, cols = tab.shape
        assert rows % steps_per_table == 0
        def index_map(h, b, t=t):
            return (jnp.clip(h * batch + b - t * steps_per_table, 0, steps_per_table - 1), 0)
        side_specs.append(pl.BlockSpec((rows // steps_per_table, cols), index_map))
        side_shapes.append(jax.ShapeDtypeStruct(tab.shape, BF16))
    kern = functools.partial(_moba_kernel, n_blocks=n_blocks, n_side=n_side, steps_per_table=steps_per_table)
    outs = pl.pallas_call(
        kern,
        grid=(ATTN_HEADS, batch),
        in_specs=[
            pl.BlockSpec(memory_space=pltpu.SMEM),
            pl.BlockSpec((seq, HEAD_DIM), lambda h, b: (b, q_col + h)),
            pl.BlockSpec((seq, HEAD_DIM), lambda h, b: (b, k_col + h)),
            pl.BlockSpec((seq, HEAD_DIM), lambda h, b: (b, v_col + h)),
            pl.BlockSpec((2 * L, L), lambda h, b: (0, 0)),
        ] + side_specs,
        out_specs=[pl.BlockSpec((seq, HEAD_DIM), lambda h, b: (b, h))] + side_specs,
        out_shape=[jax.ShapeDtypeStruct((batch * seq, ATTN_WIDTH), BF16)] + side_shapes,
        scratch_shapes=[pltpu.VMEM((2 * L, L), F32)],
        compiler_params=_params(("arbitrary", "arbitrary")), name="moba",
    )(rel_bias, proj, proj, proj, bucket, *side_tables)
    return outs[0], tuple(outs[1:])


def _conv_kernel(cb_ref, cc_ref, cu_ref, w_ref, b_ref, o_ref):
    g = cc_ref[...].astype(F32) * cu_ref[...].astype(F32)
    row = lax.broadcasted_iota(jnp.int32, g.shape, 0)
    y = w_ref[CONV_K - 1:CONV_K, :] * g
    for s in range(1, CONV_K):
        shifted = jnp.where(row >= s, pltpu.roll(g, s, axis=0), 0.0)
        y = y + w_ref[CONV_K - 1 - s:CONV_K - s, :] * shifted
    o_ref[...] = (cb_ref[...].astype(F32) * (y + b_ref[...])).astype(o_ref.dtype)


def short_conv(proj, conv_w, conv_b, batch, seq, cb_col, cc_col, cu_col, cw=512):
    width = conv_w.shape[1]
    return pl.pallas_call(
        _conv_kernel,
        grid=(batch, width // cw),
        in_specs=[
            pl.BlockSpec((seq, cw), lambda b, c: (b, cb_col + c)),
            pl.BlockSpec((seq, cw), lambda b, c: (b, cc_col + c)),
            pl.BlockSpec((seq, cw), lambda b, c: (b, cu_col + c)),
            pl.BlockSpec((CONV_K, cw), lambda b, c: (0, c)),
            pl.BlockSpec((1, cw), lambda b, c: (0, c)),
        ],
        out_specs=pl.BlockSpec((seq, cw), lambda b, c: (b, c)),
        out_shape=jax.ShapeDtypeStruct((batch * seq, width), BF16),
        compiler_params=_params(("parallel", "parallel")), name="short_conv",
    )(proj, proj, proj, conv_w, conv_b)


def _merge_kernel(attn_ref, conv_ref, wa_ref, wc_ref, ga_ref, gc_ref, bg_ref, o_ref, wa_scr, wc_scr):
    _cast_weight_once(wa_ref, wa_scr)
    _cast_weight_once(wc_ref, wc_scr)
    za = jnp.dot(attn_ref[...], wa_scr[...], preferred_element_type=F32)
    zc = jnp.dot(conv_ref[...], wc_scr[...], preferred_element_type=F32)
    ga = jax.nn.sigmoid(ga_ref[...].astype(F32) + bg_ref[0:1, :])
    gc = jax.nn.sigmoid(gc_ref[...].astype(F32) + bg_ref[1:2, :])
    o_ref[...] = (ga * za + gc * zc).astype(o_ref.dtype)


def branch_merge(attn, conv, wa, wc, proj, b_gate, ga_col, gc_col, tm=1024, tn=512):
    m, ka = attn.shape
    kc = conv.shape[1]
    n = wa.shape[1]
    tm, tn = min(tm, m), min(tn, n)
    return pl.pallas_call(
        _merge_kernel,
        grid=(n // tn, m // tm),
        in_specs=[
            pl.BlockSpec((tm, ka), lambda j, i: (i, 0)),
            pl.BlockSpec((tm, kc), lambda j, i: (i, 0)),
            pl.BlockSpec((ka, tn), lambda j, i: (0, j)),
            pl.BlockSpec((kc, tn), lambda j, i: (0, j)),
            pl.BlockSpec((tm, tn), lambda j, i: (i, ga_col + j)),
            pl.BlockSpec((tm, tn), lambda j, i: (i, gc_col + j)),
            pl.BlockSpec((2, tn), lambda j, i: (0, j)),
        ],
        out_specs=pl.BlockSpec((tm, tn), lambda j, i: (i, j)),
        out_shape=jax.ShapeDtypeStruct((m, n), BF16),
        scratch_shapes=[pltpu.VMEM((ka, tn), BF16), pltpu.VMEM((kc, tn), BF16)],
        compiler_params=_params(("parallel", "arbitrary")), name="branch_merge",
    )(attn, conv, wa, wc, proj, proj, b_gate)


def _topk_rows(s, k):
    nrows, ncols = s.shape
    rowi = lax.broadcasted_iota(jnp.int32, s.shape, 0).astype(F32)
    slot = lax.broadcasted_iota(jnp.int32, (k, ncols), 0)
    vals = jnp.zeros((k, ncols), F32)
    rows = jnp.zeros((k, ncols), F32)
    for r in range(k):
        m = s.max(axis=0, keepdims=True)
        idx = jnp.where(s == m, rowi, float(nrows)).min(axis=0, keepdims=True)
        s = jnp.where(rowi == idx, -jnp.inf, s)
        vals = jnp.where(slot == r, m, vals)
        rows = jnp.where(slot == r, idx, rows)
    return vals, rows


def _pair_candidates(v1, i1, v2, i2):
    K = PEER_TOPK
    ncols = v1.shape[1]
    sub8 = lax.broadcasted_iota(jnp.int32, (SUBLANES, ncols), 0)
    sub16 = lax.broadcasted_iota(jnp.int32, (K, ncols), 0)
    nk = float(PEER_N_KEYS)
    big = float(K * K)
    vals = [v1[0:1] + v2]
    codes = [i1[0:1] * nk + i2]
    flat = [sub16.astype(F32)]
    for a in range(1, SUBLANES):
        ok = sub8 < K // (a + 1)
        vals.append(jnp.where(ok, v1[a:a + 1] + v2[0:SUBLANES], -jnp.inf))
        codes.append(i1[a:a + 1] * nk + i2[0:SUBLANES])
        flat.append(jnp.where(ok, (sub8 + a * K).astype(F32), big))
    vals.append(v1[SUBLANES:K] + v2[0:1])
    codes.append(i1[SUBLANES:K] * nk + i2[0:1])
    flat.append(((sub8 + SUBLANES) * K).astype(F32))
    return jnp.concatenate(vals, axis=0), jnp.concatenate(codes, axis=0), jnp.concatenate(flat, axis=0)


def _peer_route_kernel(q_ref, keys_ref, code_ref, gate_ref, s1_scr, s2_scr, code_scr, gate_scr, *, chunk_tokens):
    K = PEER_TOPK
    h = pl.program_id(1)
    q = q_ref[...]
    s1_scr[...] = lax.dot_general(keys_ref[0, 0], q[:, :PEER_HALF], NT_DIMS,
                                  precision=lax.Precision.HIGHEST, preferred_element_type=F32)
    s2_scr[...] = lax.dot_general(keys_ref[0, 1], q[:, PEER_HALF:], NT_DIMS,
                                  precision=lax.Precision.HIGHEST, preferred_element_type=F32)
    row0 = pl.multiple_of(h * K, K)

    def chunk(c, carry):
        off = pl.multiple_of(c * chunk_tokens, chunk_tokens)
        v1, i1 = _topk_rows(s1_scr[:, pl.ds(off, chunk_tokens)], K)
        v2, i2 = _topk_rows(s2_scr[:, pl.ds(off, chunk_tokens)], K)
        cand, code, flat = _pair_candidates(v1, i1, v2, i2)
        slot = lax.broadcasted_iota(jnp.int32, (K, chunk_tokens), 0)
        score = jnp.zeros((K, chunk_tokens), F32)
        picked = jnp.zeros((K, chunk_tokens), F32)
        for r in range(K):
            m = cand.max(axis=0, keepdims=True)
            first = jnp.where(cand == m, flat, float(K * K)).min(axis=0, keepdims=True)
            hit = flat == first
            ck = jnp.where(hit, code, -1.0).max(axis=0, keepdims=True)
            cand = jnp.where(hit, -jnp.inf, cand)
            score = jnp.where(slot == r, m, score)
            picked = jnp.where(slot == r, ck, picked)
        e = jnp.exp(score - score[0:1])
        code_scr[pl.ds(row0, K), pl.ds(off, chunk_tokens)] = picked
        gate_scr[pl.ds(row0, K), pl.ds(off, chunk_tokens)] = e / e.sum(axis=0, keepdims=True)
        return carry

    lax.fori_loop(0, q.shape[0] // chunk_tokens, chunk, 0)

    @pl.when(h == PEER_HEADS - 1)
    def _():
        code_ref[...] = code_scr[...].T.astype(jnp.int32)
        gate_ref[...] = gate_scr[...].T


def peer_route(q, sub_keys, tt=512, chunk_tokens=256):
    n = q.shape[0]
    tt = min(tt, n)
    chunk_tokens = min(chunk_tokens, tt)
    qd = 2 * PEER_HALF
    picks = PEER_HEADS * PEER_TOPK
    out_spec = pl.BlockSpec((tt, picks), lambda i, h: (i, 0))
    return pl.pallas_call(
        functools.partial(_peer_route_kernel, chunk_tokens=chunk_tokens),
        grid=(n // tt, PEER_HEADS),
        in_specs=[
            pl.BlockSpec((tt, qd), lambda i, h: (i, h)),
            pl.BlockSpec((1, 2, PEER_N_KEYS, PEER_HALF), lambda i, h: (h, 0, 0, 0)),
        ],
        out_specs=[out_spec, out_spec],
        out_shape=[jax.ShapeDtypeStruct((n, picks), jnp.int32),
                   jax.ShapeDtypeStruct((n, picks), F32)],
        scratch_shapes=[pltpu.VMEM((PEER_N_KEYS, tt), F32), pltpu.VMEM((PEER_N_KEYS, tt), F32),
                        pltpu.VMEM((picks, tt), F32), pltpu.VMEM((picks, tt), F32)],
        compiler_params=_params(("parallel", "arbitrary")), name="peer_route",
    )(q, sub_keys)


W_ROW_PITCH = PEER_N_KEYS + SUBLANES


def _peer_weights_kernel(code_ref, gate_ref, o_ref, w_scr, *, unroll):
    nk = PEER_N_KEYS
    tt, picks = code_ref.shape
    sub = lax.broadcasted_iota(jnp.int32, (nk, picks), 0)
    zero = jnp.zeros((nk, picks), BF16)

    def one_hots(t):
        c = code_ref[pl.ds(t, 1), :]
        g = gate_ref[pl.ds(t, 1), :]
        e1 = lax.shift_right_logical(c, 7)
        e2 = lax.bitwise_and(c, nk - 1)
        at = jnp.where(sub == e1, g, 0.0).astype(BF16)
        bt = jnp.where(sub == e2, 1.0, 0.0).astype(BF16)
        return at, bt

    def body(p, carry):
        t = 2 * p
        at0, bt0 = one_hots(t)
        at1, bt1 = one_hots(t + 1)
        at = jnp.concatenate([at0, at1], axis=1)
        bt = jnp.concatenate([jnp.concatenate([bt0, zero], axis=1),
                              jnp.concatenate([zero, bt1], axis=1)], axis=0)
        w = lax.dot_general(at, bt, NT_DIMS, preferred_element_type=F32)
        row = pl.multiple_of(t * W_ROW_PITCH, SUBLANES)
        w_scr[pl.ds(row, nk), :] = w[:, :nk]
        w_scr[pl.ds(row + W_ROW_PITCH, nk), :] = w[:, nk:]
        return carry

    lax.fori_loop(0, tt // 2, body, 0, unroll=unroll)
    for e1 in range(nk):
        o_ref[:, e1 * nk:(e1 + 1) * nk] = w_scr[pl.ds(e1, tt, stride=W_ROW_PITCH), :].astype(o_ref.dtype)


def peer_dense_weights(code, gate, tt=128, unroll=64):
    n, picks = code.shape
    tt = min(tt, n)
    nk = PEER_N_KEYS
    kern = functools.partial(_peer_weights_kernel, unroll=unroll)
    return pl.pallas_call(
        kern,
        grid=(n // tt,),
        in_specs=[pl.BlockSpec((tt, picks), lambda i: (i, 0)),
                  pl.BlockSpec((tt, picks), lambda i: (i, 0))],
        out_specs=pl.BlockSpec((tt, nk * nk), lambda i: (i, 0)),
        out_shape=jax.ShapeDtypeStruct((n, nk * nk), BF16),
        scratch_shapes=[pltpu.VMEM((tt * W_ROW_PITCH, nk), F32)],
        compiler_params=_params(("parallel",)), name="peer_weights",
    )(code, gate)


def _row_rms_scale(t, g):
    return t * lax.rsqrt(jnp.mean(t * t, axis=-1, keepdims=True) + NORM_EPS) * g


def _peer_dense_kernel(h_ref, gin_ref, u_ref, v_ref, w_ref, gout_ref, o_ref, acc_scr, x_scr, *, n_edge, n_main):
    j = pl.program_id(1)
    edge_rows = h_ref.shape[0]

    @pl.when(j < n_edge)
    def _():
        rows = pl.ds(pl.multiple_of(j * edge_rows, edge_rows), edge_rows)
        hc = h_ref[...]
        acc_scr[rows, :] = hc
        x_scr[rows, :] = _row_rms_scale(hc, gin_ref[...]).astype(x_scr.dtype)

    @pl.when((j >= n_edge) & (j < n_edge + n_main))
    def _():
        a = lax.dot_general(x_scr[...], u_ref[...], NT_DIMS, preferred_element_type=F32)
        act = 0.5 * a * (1.0 + lax.erf(a * math.sqrt(0.5))) * w_ref[...].astype(F32)
        acc_scr[...] += jnp.dot(act.astype(BF16), v_ref[...], preferred_element_type=F32)

    @pl.when(j >= n_edge + n_main)
    def _():
        r = j - (n_edge + n_main)
        rows = pl.ds(pl.multiple_of(r * edge_rows, edge_rows), edge_rows)
        o_ref[...] = _row_rms_scale(acc_scr[rows, :], gout_ref[...])


def peer_dense_final(h, g_in, u, v, w, g_out, tn=1024, te=512, edge_rows=128):
    n, d = h.shape
    ne = u.shape[0]
    tn, te = min(tn, n), min(te, ne)
    edge_rows = min(edge_rows, tn)
    n_edge, n_main = tn // edge_rows, ne // te

    def main_step(j):
        return jnp.clip(j - n_edge, 0, n_main - 1)

    kern = functools.partial(_peer_dense_kernel, n_edge=n_edge, n_main=n_main)
    return pl.pallas_call(
        kern,
        grid=(n // tn, n_edge + n_main + n_edge),
        in_specs=[
            pl.BlockSpec((edge_rows, d), lambda i, j: (i * n_edge + jnp.minimum(j, n_edge - 1), 0)),
            pl.BlockSpec((1, d), lambda i, j: (0, 0)),
            pl.BlockSpec((te, d), lambda i, j: (main_step(j), 0)),
            pl.BlockSpec((te, d), lambda i, j: (main_step(j), 0)),
            pl.BlockSpec((tn, te), lambda i, j: (i, main_step(j))),
            pl.BlockSpec((1, d), lambda i, j: (0, 0)),
        ],
        out_specs=pl.BlockSpec(
            (edge_rows, d), lambda i, j: (i * n_edge + jnp.clip(j - n_edge - n_main, 0, n_edge - 1), 0)),
        out_shape=jax.ShapeDtypeStruct((n, d), F32),
        scratch_shapes=[pltpu.VMEM((tn, d), F32), pltpu.VMEM((tn, d), BF16)],
        compiler_params=_params(("parallel", "arbitrary")), name="peer_dense",
    )(h, g_in.reshape(1, d), u, v, w, g_out.reshape(1, d))


def kernel(x, norm_mix, w_in, conv_w, conv_b, w_br_attn, w_br_conv, b_gate, rel_bias,
           w_out, norm_ffn, peer_w_q, peer_sub_keys, peer_u, peer_v, norm_final):
    batch, seq, d = x.shape
    n = batch * seq
    assert norm_mix.shape[0] == 1, "single-layer problem"
    h = x.reshape(n, d)

    hn = rmsnorm(h, norm_mix[0], BF16)
    proj = matmul(hn, w_in[0], BF16, name="in_proj")
    attn, (u_bf16, v_bf16) = moba_attention(
        proj, rel_bias, batch, seq, q_col=0, k_col=ATTN_WIDTH // HEAD_DIM, v_col=2 * ATTN_WIDTH // HEAD_DIM,
        side_tables=(peer_u[0], peer_v[0]))
    cw = 512
    c0 = 3 * ATTN_WIDTH // cw
    conv = short_conv(proj, conv_w[0].reshape(CONV_K, CONV_WIDTH), conv_b[0].reshape(1, CONV_WIDTH),
                      batch, seq, cb_col=c0, cc_col=c0 + CONV_WIDTH // cw, cu_col=c0 + 2 * CONV_WIDTH // cw)
    tn = 512
    g0 = (3 * ATTN_WIDTH + 3 * CONV_WIDTH) // tn
    merged = branch_merge(attn, conv, w_br_attn[0], w_br_conv[0],
                          proj, b_gate[0], ga_col=g0, gc_col=g0 + d // tn, tn=tn)
    h = matmul(merged, w_out[0], F32, residual=h, tn=512, name="out_proj")

    q = norm_matmul(h, norm_ffn[0], peer_w_q[0].astype(BF16), F32, name="peer_query")
    code, gate = peer_route(q, peer_sub_keys[0])
    w = peer_dense_weights(code, gate)
    out = peer_dense_final(h, norm_ffn[0], u_bf16, v_bf16, w, norm_final)
    return out.reshape(batch, seq, d)
```

```python
import functools
import math

import numpy as np
import jax
import jax.numpy as jnp
from jax import lax
from jax.experimental import pallas as pl
from jax.experimental.pallas import tpu as pltpu

F32 = jnp.float32
BF16 = jnp.bfloat16

D_MODEL = 4096
ATTN_HEADS = 16
HEAD_DIM = 128
ATTN_WIDTH = ATTN_HEADS * HEAD_DIM
MOBA_BLOCK = 256
MOBA_TOPK = 3
REL_BUCKETS = 32
REL_MAX_DIST = 128
CONV_WIDTH = 2048
CONV_K = 3
PEER_HEADS = 8
PEER_N_KEYS = 128
PEER_HALF = 128
PEER_TOPK = 16
NORM_EPS = 1e-6

V7X_VMEM_BYTES = 64 * 1024 * 1024
VMEM_LIMIT = V7X_VMEM_BYTES - 8 * 1024 * 1024
LANES = 128
SUBLANES = 8
NT_DIMS = (((1,), (1,)), ((), ()))
LOG2_E = math.log2(math.e)


def _params(semantics):
    return pltpu.CompilerParams(dimension_semantics=semantics, vmem_limit_bytes=VMEM_LIMIT)


def _rmsnorm_kernel(x_ref, g_ref, o_ref):
    x = x_ref[...]
    ms = jnp.mean(x * x, axis=-1, keepdims=True)
    o_ref[...] = (x * lax.rsqrt(ms + NORM_EPS) * g_ref[...]).astype(o_ref.dtype)


def rmsnorm(x, g, out_dtype, rows=256):
    n, d = x.shape
    rows = min(rows, n)
    row_spec = pl.BlockSpec((rows, d), lambda i: (i, 0))
    return pl.pallas_call(
        _rmsnorm_kernel, grid=(n // rows,),
        in_specs=[row_spec, pl.BlockSpec((1, d), lambda i: (0, 0))], out_specs=row_spec,
        out_shape=jax.ShapeDtypeStruct((n, d), out_dtype),
        compiler_params=_params(("parallel",)), name="rmsnorm",
    )(x, g.reshape(1, d))


def _cast_weight_once(b_ref, b_scr):
    @pl.when(pl.program_id(1) == 0)
    def _():
        b_scr[...] = b_ref[...].astype(BF16)


def _matmul_kernel(a_ref, b_ref, *rest, n_col_blocks, has_residual):
    r_ref = rest[0] if has_residual else None
    o_ref, b_scr = rest[-2], rest[-1]
    jj = pl.program_id(0)
    i = pl.program_id(1)
    kc = b_ref.shape[0]

    @pl.when(jj < n_col_blocks)
    def _():
        rows = pl.ds(pl.multiple_of(i * kc, kc), kc)
        b_scr[jj % 2, rows, :] = b_ref[...].astype(BF16)

    @pl.when(jj > 0)
    def _():
        acc = jnp.dot(a_ref[...], b_scr[(jj + 1) % 2], preferred_element_type=F32)
        if has_residual:
            acc = r_ref[...] + acc
        o_ref[...] = acc.astype(o_ref.dtype)


def matmul(a, b, out_dtype, residual=None, tm=1024, tn=1024, name="matmul"):
    m, k = a.shape
    _, n = b.shape
    tm, tn = min(tm, m), min(tn, n)
    ni, nj = m // tm, n // tn
    assert k % ni == 0
    kc = k // ni

    def out_index(jj, i):
        return (jnp.where(jj == 0, 0, i), jnp.maximum(jj - 1, 0))

    a_spec = pl.BlockSpec((tm, k), lambda jj, i: (jnp.where(jj == 0, 0, i), 0))
    b_spec = pl.BlockSpec((kc, tn), lambda jj, i: (jnp.where(jj < nj, i, ni - 1), jnp.minimum(jj, nj - 1)))
    o_spec = pl.BlockSpec((tm, tn), out_index)
    ins, specs = (a, b), [a_spec, b_spec]
    if residual is not None:
        ins, specs = ins + (residual,), specs + [o_spec]
    kern = functools.partial(_matmul_kernel, n_col_blocks=nj, has_residual=residual is not None)
    return pl.pallas_call(
        kern, grid=(nj + 1, ni), in_specs=specs, out_specs=o_spec,
        out_shape=jax.ShapeDtypeStruct((m, n), out_dtype),
        scratch_shapes=[pltpu.VMEM((2, k, tn), BF16)],
        compiler_params=_params(("arbitrary", "arbitrary")), name=name,
    )(*ins)


NORM_CHUNK_ROWS = 64


def _rmsnorm_chunks(load_rows, g_ref, dst_ref, unrolled=False):
    def body(c, carry):
        start = c * NORM_CHUNK_ROWS
        rows = pl.ds(start if unrolled else pl.multiple_of(start, NORM_CHUNK_ROWS), NORM_CHUNK_ROWS)
        x = load_rows(rows)
        ms = jnp.mean(x * x, axis=-1, keepdims=True)
        dst_ref[rows, :] = (x * lax.rsqrt(ms + NORM_EPS) * g_ref[...]).astype(dst_ref.dtype)
        return carry

    n_chunks = dst_ref.shape[0] // NORM_CHUNK_ROWS
    if unrolled:
        for c in range(n_chunks):
            body(c, 0)
    else:
        lax.fori_loop(0, n_chunks, body, 0)


def _norm_matmul_kernel(x_ref, g_ref, b_ref, o_ref, hn_scr, *, n_row_tiles):
    s = pl.program_id(0)

    def normalise():
        _rmsnorm_chunks(lambda rows: x_ref[rows, :], g_ref, hn_scr.at[s % 2], unrolled=True)

    def multiply():
        o_ref[...] = jnp.dot(hn_scr[(s + 1) % 2], b_ref[...], preferred_element_type=F32).astype(o_ref.dtype)

    @pl.when(s == 0)
    def _():
        normalise()

    @pl.when((s > 0) & (s < n_row_tiles))
    def _():
        multiply()
        normalise()

    @pl.when(s == n_row_tiles)
    def _():
        multiply()


def norm_matmul(x, g, b, out_dtype, tm=512, name="norm_matmul"):
    m, k = x.shape
    _, n = b.shape
    tm = min(tm, m)
    ni = m // tm
    kern = functools.partial(_norm_matmul_kernel, n_row_tiles=ni)
    return pl.pallas_call(
        kern, grid=(ni + 1,),
        in_specs=[pl.BlockSpec((tm, k), lambda s: (jnp.minimum(s, ni - 1), 0)),
                  pl.BlockSpec((1, k), lambda s: (0, 0)),
                  pl.BlockSpec((k, n), lambda s: (0, 0), pipeline_mode=pl.Buffered(1))],
        out_specs=pl.BlockSpec((tm, n), lambda s: (jnp.maximum(s - 1, 0), 0)),
        out_shape=jax.ShapeDtypeStruct((m, n), out_dtype),
        scratch_shapes=[pltpu.VMEM((2, tm, k), BF16)],
        compiler_params=_params(("arbitrary",)), name=name,
    )(x, g.reshape(1, k), b)


def _t5_bucket_table(max_dist):
    dist = np.arange(max_dist, dtype=np.int32)
    max_exact = REL_BUCKETS // 2
    d32 = np.maximum(dist, 1).astype(np.float32)
    ratio = (np.log(d32 / np.float32(max_exact)) / np.float32(math.log(REL_MAX_DIST / max_exact))
             * np.float32(REL_BUCKETS - max_exact))
    large = max_exact + ratio.astype(np.int32)
    large = np.minimum(large, REL_BUCKETS - 1)
    return np.where(dist < max_exact, dist, large).astype(np.int32)


def _moba_bucket_tile():
    L = MOBA_BLOCK
    key = np.arange(2 * L)[:, None]
    qi = np.arange(L)[None, :]
    dist = np.maximum(qi - key + L, 0)
    return _t5_bucket_table(2 * L)[dist]


def _moba_kernel(tab_ref, q_ref, k_ref, v_ref, bucket_ref, *rest, n_blocks, n_side, steps_per_table):
    side_in, o_ref, side_out, bias_scr = rest[:n_side], rest[n_side], rest[n_side + 1:-1], rest[-1]
    L = MOBA_BLOCK
    h = pl.program_id(0)
    b = pl.program_id(1)

    step = h * pl.num_programs(1) + b
    for t in range(n_side):
        @pl.when((step >= t * steps_per_table) & (step < (t + 1) * steps_per_table))
        def _(t=t):
            def body(c, carry):
                rows = pl.ds(pl.multiple_of(c * NORM_CHUNK_ROWS, NORM_CHUNK_ROWS), NORM_CHUNK_ROWS)
                side_out[t][rows, :] = side_in[t][rows, :].astype(BF16)
                return carry
            lax.fori_loop(0, side_in[t].shape[0] // NORM_CHUNK_ROWS, body, 0)

    @pl.when(b == 0)
    def _():
        bk = bucket_ref[...]
        acc = jnp.zeros(bk.shape, F32)
        for kk in range(REL_BUCKETS):
            acc = jnp.where(bk == kk, tab_ref[kk, h], acc)
        bias_scr[...] = acc * LOG2_E

    neg_inf = jnp.float32(-jnp.inf)
    far = tab_ref[REL_BUCKETS - 1, h] * LOG2_E
    scale = HEAD_DIM ** -0.5 * LOG2_E
    k_means = [jnp.mean(k_ref[n * L:(n + 1) * L, :].astype(F32), axis=0, keepdims=True)
               for n in range(n_blocks)]
    pad = jnp.zeros((SUBLANES - n_blocks % SUBLANES, HEAD_DIM), F32) if n_blocks % SUBLANES else None
    k_mean = jnp.concatenate(k_means + ([pad] if pad is not None else []), axis=0)
    v_t = v_ref[...].astype(F32).T.astype(BF16)
    key_i = lax.broadcasted_iota(jnp.int32, (L, L), 0)
    qry_i = lax.broadcasted_iota(jnp.int32, (L, L), 1)

    for qb in range(n_blocks):
        q = q_ref[qb * L:(qb + 1) * L, :]
        sel = None
        if qb > MOBA_TOPK:
            gate = lax.dot_general(k_mean, q.astype(F32), NT_DIMS,
                                   precision=lax.Precision.HIGHEST, preferred_element_type=F32)
            g = [gate[m:m + 1, :] for m in range(qb)]
            sel = []
            for n in range(qb):
                rank = jnp.zeros((1, L), F32)
                for m in range(qb):
                    if m != n:
                        beats = (g[m] > g[n]) | (g[m] == g[n]) if m < n else (g[m] > g[n])
                        rank = rank + jnp.where(beats, 1.0, 0.0)
                sel.append(jnp.where(rank < MOBA_TOPK, 1.0, 0.0))
        logits = []
        for n in range(qb + 1):
            s = lax.dot_general(k_ref[n * L:(n + 1) * L, :], q, NT_DIMS, preferred_element_type=F32)
            if n == qb:
                logits.append(jnp.where(qry_i >= key_i, s * scale + bias_scr[L:2 * L, :], neg_inf))
            else:
                lg = s * scale + (bias_scr[0:L, :] if n == qb - 1 else far)
                logits.append(lg if sel is None else jnp.where(sel[n] > 0.5, lg, neg_inf))
        mx = logits[0].max(axis=0, keepdims=True)
        for lg in logits[1:]:
            mx = jnp.maximum(mx, lg.max(axis=0, keepdims=True))
        denom = jnp.zeros((1, L), F32)
        acc = jnp.zeros((HEAD_DIM, L), F32)
        for n, lg in enumerate(logits):
            p = jnp.exp2(lg - mx)
            denom = denom + p.sum(axis=0, keepdims=True)
            acc = acc + jnp.dot(v_t[:, n * L:(n + 1) * L], p.astype(BF16), preferred_element_type=F32)
        o_ref[qb * L:(qb + 1) * L, :] = (acc / denom).T.astype(o_ref.dtype)


def moba_attention(proj, rel_bias, batch, seq, q_col, k_col, v_col, side_tables=()):
    L = MOBA_BLOCK
    n_blocks = seq // L
    bucket = jnp.asarray(_moba_bucket_tile())
    n_side = len(side_tables)
    n_steps = ATTN_HEADS * batch
    steps_per_table = n_steps // max(n_side, 1)
    side_specs, side_shapes = [], []
    for t, tab in enumerate(side_tables):
        rows, cols = tab.shape
        assert rows % steps_per_table == 0
        def index_map(h, b, t=t):
            return (jnp.clip(h * batch + b - t * steps_per_table, 0, steps_per_table - 1), 0)
        side_specs.append(pl.BlockSpec((rows // steps_per_table, cols), index_map))
        side_shapes.append(jax.ShapeDtypeStruct(tab.shape, BF16))
    kern = functools.partial(_moba_kernel, n_blocks=n_blocks, n_side=n_side, steps_per_table=steps_per_table)
    outs = pl.pallas_call(
        kern,
        grid=(ATTN_HEADS, batch),
        in_specs=[
            pl.BlockSpec(memory_space=pltpu.SMEM),
            pl.BlockSpec((seq, HEAD_DIM), lambda h, b: (b, q_col + h)),
            pl.BlockSpec((seq, HEAD_DIM), lambda h, b: (b, k_col + h)),
            pl.BlockSpec((seq, HEAD_DIM), lambda h, b: (b, v_col + h)),
            pl.BlockSpec((2 * L, L), lambda h, b: (0, 0)),
        ] + side_specs,
        out_specs=[pl.BlockSpec((seq, HEAD_DIM), lambda h, b: (b, h))] + side_specs,
        out_shape=[jax.ShapeDtypeStruct((batch * seq, ATTN_WIDTH), BF16)] + side_shapes,
        scratch_shapes=[pltpu.VMEM((2 * L, L), F32)],
        compiler_params=_params(("arbitrary", "arbitrary")), name="moba",
    )(rel_bias, proj, proj, proj, bucket, *side_tables)
    return outs[0], tuple(outs[1:])


def _conv_kernel(cb_ref, cc_ref, cu_ref, w_ref, b_ref, o_ref):
    g = cc_ref[...].astype(F32) * cu_ref[...].astype(F32)
    row = lax.broadcasted_iota(jnp.int32, g.shape, 0)
    y = w_ref[CONV_K - 1:CONV_K, :] * g
    for s in range(1, CONV_K):
        shifted = jnp.where(row >= s, pltpu.roll(g, s, axis=0), 0.0)
        y = y + w_ref[CONV_K - 1 - s:CONV_K - s, :] * shifted
    o_ref[...] = (cb_ref[...].astype(F32) * (y + b_ref[...])).astype(o_ref.dtype)


def short_conv(proj, conv_w, conv_b, batch, seq, cb_col, cc_col, cu_col, cw=512):
    width = conv_w.shape[1]
    return pl.pallas_call(
        _conv_kernel,
        grid=(batch, width // cw),
        in_specs=[
            pl.BlockSpec((seq, cw), lambda b, c: (b, cb_col + c)),
            pl.BlockSpec((seq, cw), lambda b, c: (b, cc_col + c)),
            pl.BlockSpec((seq, cw), lambda b, c: (b, cu_col + c)),
            pl.BlockSpec((CONV_K, cw), lambda b, c: (0, c)),
            pl.BlockSpec((1, cw), lambda b, c: (0, c)),
        ],
        out_specs=pl.BlockSpec((seq, cw), lambda b, c: (b, c)),
        out_shape=jax.ShapeDtypeStruct((batch * seq, width), BF16),
        compiler_params=_params(("parallel", "parallel")), name="short_conv",
    )(proj, proj, proj, conv_w, conv_b)


def _merge_kernel(attn_ref, conv_ref, wa_ref, wc_ref, ga_ref, gc_ref, bg_ref, o_ref, wa_scr, wc_scr):
    _cast_weight_once(wa_ref, wa_scr)
    _cast_weight_once(wc_ref, wc_scr)
    za = jnp.dot(attn_ref[...], wa_scr[...], preferred_element_type=F32)
    zc = jnp.dot(conv_ref[...], wc_scr[...], preferred_element_type=F32)
    ga = jax.nn.sigmoid(ga_ref[...].astype(F32) + bg_ref[0:1, :])
    gc = jax.nn.sigmoid(gc_ref[...].astype(F32) + bg_ref[1:2, :])
    o_ref[...] = (ga * za + gc * zc).astype(o_ref.dtype)


def branch_merge(attn, conv, wa, wc, proj, b_gate, ga_col, gc_col, tm=1024, tn=512):
    m, ka = attn.shape
    kc = conv.shape[1]
    n = wa.shape[1]
    tm, tn = min(tm, m), min(tn, n)
    return pl.pallas_call(
        _merge_kernel,
        grid=(n // tn, m // tm),
        in_specs=[
            pl.BlockSpec((tm, ka), lambda j, i: (i, 0)),
            pl.BlockSpec((tm, kc), lambda j, i: (i, 0)),
            pl.BlockSpec((ka, tn), lambda j, i: (0, j)),
            pl.BlockSpec((kc, tn), lambda j, i: (0, j)),
            pl.BlockSpec((tm, tn), lambda j, i: (i, ga_col + j)),
            pl.BlockSpec((tm, tn), lambda j, i: (i, gc_col + j)),
            pl.BlockSpec((2, tn), lambda j, i: (0, j)),
        ],
        out_specs=pl.BlockSpec((tm, tn), lambda j, i: (i, j)),
        out_shape=jax.ShapeDtypeStruct((m, n), BF16),
        scratch_shapes=[pltpu.VMEM((ka, tn), BF16), pltpu.VMEM((kc, tn), BF16)],
        compiler_params=_params(("parallel", "arbitrary")), name="branch_merge",
    )(attn, conv, wa, wc, proj, proj, b_gate)


def _topk_rows(s, k):
    nrows, ncols = s.shape
    rowi = lax.broadcasted_iota(jnp.int32, s.shape, 0).astype(F32)
    slot = lax.broadcasted_iota(jnp.int32, (k, ncols), 0)
    vals = jnp.zeros((k, ncols), F32)
    rows = jnp.zeros((k, ncols), F32)
    for r in range(k):
        m = s.max(axis=0, keepdims=True)
        idx = jnp.where(s == m, rowi, float(nrows)).min(axis=0, keepdims=True)
        s = jnp.where(rowi == idx, -jnp.inf, s)
        vals = jnp.where(slot == r, m, vals)
        rows = jnp.where(slot == r, idx, rows)
    return vals, rows


def _topk_rows_untied(s, k):
    nrows, ncols = s.shape
    lane = lax.broadcasted_iota(jnp.int32, (SUBLANES, nrows), 1).astype(F32)
    sub = lax.broadcasted_iota(jnp.int32, (SUBLANES, nrows), 0)
    probe = jnp.where(sub == 0, lane, jnp.where(sub == 1, 1.0, 0.0))
    slot = lax.broadcasted_iota(jnp.int32, (k, ncols), 0)
    vals = jnp.zeros((k, ncols), F32)
    rows = jnp.zeros((k, ncols), F32)
    worst = jnp.zeros((1, ncols), F32)
    for r in range(k):
        m = s.max(axis=0, keepdims=True)
        hit = s == m
        s = jnp.where(hit, -jnp.inf, s)
        stats = jnp.dot(probe, jnp.where(hit, 1.0, 0.0), preferred_element_type=F32)
        vals = jnp.where(slot == r, m, vals)
        rows = jnp.where(slot == r, stats[0:1], rows)
        worst = jnp.maximum(worst, stats[1:2])
    return vals, rows, worst


def _pair_candidates(v1, i1, v2, i2):
    K = PEER_TOPK
    ncols = v1.shape[1]
    sub8 = lax.broadcasted_iota(jnp.int32, (SUBLANES, ncols), 0)
    sub16 = lax.broadcasted_iota(jnp.int32, (K, ncols), 0)
    nk = float(PEER_N_KEYS)
    big = float(K * K)
    vals = [v1[0:1] + v2]
    codes = [i1[0:1] * nk + i2]
    flat = [sub16.astype(F32)]
    for a in range(1, SUBLANES):
        ok = sub8 < K // (a + 1)
        vals.append(jnp.where(ok, v1[a:a + 1] + v2[0:SUBLANES], -jnp.inf))
        codes.append(i1[a:a + 1] * nk + i2[0:SUBLANES])
        flat.append(jnp.where(ok, (sub8 + a * K).astype(F32), big))
    vals.append(v1[SUBLANES:K] + v2[0:1])
    codes.append(i1[SUBLANES:K] * nk + i2[0:1])
    flat.append(((sub8 + SUBLANES) * K).astype(F32))
    return jnp.concatenate(vals, axis=0), jnp.concatenate(codes, axis=0), jnp.concatenate(flat, axis=0)


def _peer_route_kernel(q_ref, keys_ref, code_ref, gate_ref, s1_scr, s2_scr, code_scr, gate_scr, topk_scr,
                       *, chunk_tokens):
    K = PEER_TOPK
    h = pl.program_id(1)
    q = q_ref[...]
    s1_scr[...] = lax.dot_general(keys_ref[0, 0], q[:, :PEER_HALF], NT_DIMS,
                                  precision=lax.Precision.HIGHEST, preferred_element_type=F32)
    s2_scr[...] = lax.dot_general(keys_ref[0, 1], q[:, PEER_HALF:], NT_DIMS,
                                  precision=lax.Precision.HIGHEST, preferred_element_type=F32)
    row0 = pl.multiple_of(h * K, K)

    def chunk(c, carry):
        off = pl.multiple_of(c * chunk_tokens, chunk_tokens)
        cols = pl.ds(off, chunk_tokens)
        v1, i1, t1 = _topk_rows_untied(s1_scr[:, cols], K)
        v2, i2, t2 = _topk_rows_untied(s2_scr[:, cols], K)
        for slot_i, val in enumerate((v1, i1, v2, i2)):
            topk_scr[slot_i] = val

        @pl.when(jnp.max(jnp.maximum(t1, t2)) > 1.5)
        def _():
            for slot_i, s_scr in ((0, s1_scr), (2, s2_scr)):
                v, i = _topk_rows(s_scr[:, cols], K)
                topk_scr[slot_i] = v
                topk_scr[slot_i + 1] = i

        cand, code, flat = _pair_candidates(topk_scr[0], topk_scr[1], topk_scr[2], topk_scr[3])
        slot = lax.broadcasted_iota(jnp.int32, (K, chunk_tokens), 0)
        score = jnp.zeros((K, chunk_tokens), F32)
        picked = jnp.zeros((K, chunk_tokens), F32)
        for r in range(K):
            m = cand.max(axis=0, keepdims=True)
            first = jnp.where(cand == m, flat, float(K * K)).min(axis=0, keepdims=True)
            hit = flat == first
            ck = jnp.where(hit, code, -1.0).max(axis=0, keepdims=True)
            cand = jnp.where(hit, -jnp.inf, cand)
            score = jnp.where(slot == r, m, score)
            picked = jnp.where(slot == r, ck, picked)
        e = jnp.exp(score - score[0:1])
        code_scr[pl.ds(row0, K), pl.ds(off, chunk_tokens)] = picked
        gate_scr[pl.ds(row0, K), pl.ds(off, chunk_tokens)] = e / e.sum(axis=0, keepdims=True)
        return carry

    lax.fori_loop(0, q.shape[0] // chunk_tokens, chunk, 0)

    @pl.when(h == PEER_HEADS - 1)
    def _():
        code_ref[...] = code_scr[...].T.astype(jnp.int32)
        gate_ref[...] = gate_scr[...].T


def peer_route(q, sub_keys, tt=512, chunk_tokens=512):
    n = q.shape[0]
    tt = min(tt, n)
    chunk_tokens = min(chunk_tokens, tt)
    qd = 2 * PEER_HALF
    picks = PEER_HEADS * PEER_TOPK
    out_spec = pl.BlockSpec((tt, picks), lambda i, h: (i, 0))
    return pl.pallas_call(
        functools.partial(_peer_route_kernel, chunk_tokens=chunk_tokens),
        grid=(n // tt, PEER_HEADS),
        in_specs=[
            pl.BlockSpec((tt, qd), lambda i, h: (i, h)),
            pl.BlockSpec((1, 2, PEER_N_KEYS, PEER_HALF), lambda i, h: (h, 0, 0, 0)),
        ],
        out_specs=[out_spec, out_spec],
        out_shape=[jax.ShapeDtypeStruct((n, picks), jnp.int32),
                   jax.ShapeDtypeStruct((n, picks), F32)],
        scratch_shapes=[pltpu.VMEM((PEER_N_KEYS, tt), F32), pltpu.VMEM((PEER_N_KEYS, tt), F32),
                        pltpu.VMEM((picks, tt), F32), pltpu.VMEM((picks, tt), F32),
                        pltpu.VMEM((4, PEER_TOPK, chunk_tokens), F32)],
        compiler_params=_params(("parallel", "arbitrary")), name="peer_route",
    )(q, sub_keys)


W_ROW_PITCH = PEER_N_KEYS + SUBLANES


def _peer_weights_kernel(code_ref, gate_ref, o_ref, w_scr, *, unroll):
    nk = PEER_N_KEYS
    tt, picks = code_ref.shape
    sub = lax.broadcasted_iota(jnp.int32, (nk, picks), 0)
    zero = jnp.zeros((nk, picks), BF16)

    def one_hots(t):
        c = code_ref[pl.ds(t, 1), :]
        g = gate_ref[pl.ds(t, 1), :]
        e1 = lax.shift_right_logical(c, 7)
        e2 = lax.bitwise_and(c, nk - 1)
        at = jnp.where(sub == e1, g, 0.0).astype(BF16)
        bt = jnp.where(sub == e2, 1.0, 0.0).astype(BF16)
        return at, bt

    def body(p, carry):
        t = 2 * p
        at0, bt0 = one_hots(t)
        at1, bt1 = one_hots(t + 1)
        at = jnp.concatenate([at0, at1], axis=1)
        bt = jnp.concatenate([jnp.concatenate([bt0, zero], axis=1),
                              jnp.concatenate([zero, bt1], axis=1)], axis=0)
        w = lax.dot_general(at, bt, NT_DIMS, preferred_element_type=F32)
        row = pl.multiple_of(t * W_ROW_PITCH, SUBLANES)
        w_scr[pl.ds(row, nk), :] = w[:, :nk]
        w_scr[pl.ds(row + W_ROW_PITCH, nk), :] = w[:, nk:]
        return carry

    lax.fori_loop(0, tt // 2, body, 0, unroll=unroll)
    for e1 in range(nk):
        o_ref[:, e1 * nk:(e1 + 1) * nk] = w_scr[pl.ds(e1, tt, stride=W_ROW_PITCH), :].astype(o_ref.dtype)


def peer_dense_weights(code, gate, tt=128, unroll=64):
    n, picks = code.shape
    tt = min(tt, n)
    nk = PEER_N_KEYS
    kern = functools.partial(_peer_weights_kernel, unroll=unroll)
    return pl.pallas_call(
        kern,
        grid=(n // tt,),
        in_specs=[pl.BlockSpec((tt, picks), lambda i: (i, 0)),
                  pl.BlockSpec((tt, picks), lambda i: (i, 0))],
        out_specs=pl.BlockSpec((tt, nk * nk), lambda i: (i, 0)),
        out_shape=jax.ShapeDtypeStruct((n, nk * nk), BF16),
        scratch_shapes=[pltpu.VMEM((tt * W_ROW_PITCH, nk), F32)],
        compiler_params=_params(("parallel",)), name="peer_weights",
    )(code, gate)


def _row_rms_scale(t, g):
    return t * lax.rsqrt(jnp.mean(t * t, axis=-1, keepdims=True) + NORM_EPS) * g


def _peer_dense_kernel(h_ref, gin_ref, u_ref, v_ref, w_ref, gout_ref, o_ref, acc_scr, x_scr, *, n_edge, n_main):
    j = pl.program_id(1)
    edge_rows = h_ref.shape[0]

    @pl.when(j < n_edge)
    def _():
        rows = pl.ds(pl.multiple_of(j * edge_rows, edge_rows), edge_rows)
        hc = h_ref[...]
        acc_scr[rows, :] = hc
        x_scr[rows, :] = _row_rms_scale(hc, gin_ref[...]).astype(x_scr.dtype)

    @pl.when((j >= n_edge) & (j < n_edge + n_main))
    def _():
        a = lax.dot_general(x_scr[...], u_ref[...], NT_DIMS, preferred_element_type=F32)
        act = 0.5 * a * (1.0 + lax.erf(a * math.sqrt(0.5))) * w_ref[...].astype(F32)
        acc_scr[...] += jnp.dot(act.astype(BF16), v_ref[...], preferred_element_type=F32)

    @pl.when(j >= n_edge + n_main)
    def _():
        r = j - (n_edge + n_main)
        rows = pl.ds(pl.multiple_of(r * edge_rows, edge_rows), edge_rows)
        o_ref[...] = _row_rms_scale(acc_scr[rows, :], gout_ref[...])


def peer_dense_final(h, g_in, u, v, w, g_out, tn=1024, te=512, edge_rows=128):
    n, d = h.shape
    ne = u.shape[0]
    tn, te = min(tn, n), min(te, ne)
    edge_rows = min(edge_rows, tn)
    n_edge, n_main = tn // edge_rows, ne // te

    def main_step(j):
        return jnp.clip(j - n_edge, 0, n_main - 1)

    kern = functools.partial(_peer_dense_kernel, n_edge=n_edge, n_main=n_main)
    return pl.pallas_call(
        kern,
        grid=(n // tn, n_edge + n_main + n_edge),
        in_specs=[
            pl.BlockSpec((edge_rows, d), lambda i, j: (i * n_edge + jnp.minimum(j, n_edge - 1), 0)),
            pl.BlockSpec((1, d), lambda i, j: (0, 0)),
            pl.BlockSpec((te, d), lambda i, j: (main_step(j), 0)),
            pl.BlockSpec((te, d), lambda i, j: (main_step(j), 0)),
            pl.BlockSpec((tn, te), lambda i, j: (i, main_step(j))),
            pl.BlockSpec((1, d), lambda i, j: (0, 0)),
        ],
        out_specs=pl.BlockSpec(
            (edge_rows, d), lambda i, j: (i * n_edge + jnp.clip(j - n_edge - n_main, 0, n_edge - 1), 0)),
        out_shape=jax.ShapeDtypeStruct((n, d), F32),
        scratch_shapes=[pltpu.VMEM((tn, d), F32), pltpu.VMEM((tn, d), BF16)],
        compiler_params=_params(("parallel", "arbitrary")), name="peer_dense",
    )(h, g_in.reshape(1, d), u, v, w, g_out.reshape(1, d))


def kernel(x, norm_mix, w_in, conv_w, conv_b, w_br_attn, w_br_conv, b_gate, rel_bias,
           w_out, norm_ffn, peer_w_q, peer_sub_keys, peer_u, peer_v, norm_final):
    batch, seq, d = x.shape
    n = batch * seq
    assert norm_mix.shape[0] == 1, "single-layer problem"
    h = x.reshape(n, d)

    hn = rmsnorm(h, norm_mix[0], BF16)
    proj = matmul(hn, w_in[0], BF16, name="in_proj")
    attn, (u_bf16, v_bf16) = moba_attention(
        proj, rel_bias, batch, seq, q_col=0, k_col=ATTN_WIDTH // HEAD_DIM, v_col=2 * ATTN_WIDTH // HEAD_DIM,
        side_tables=(peer_u[0], peer_v[0]))
    cw = 512
    c0 = 3 * ATTN_WIDTH // cw
    conv = short_conv(proj, conv_w[0].reshape(CONV_K, CONV_WIDTH), conv_b[0].reshape(1, CONV_WIDTH),
                      batch, seq, cb_col=c0, cc_col=c0 + CONV_WIDTH // cw, cu_col=c0 + 2 * CONV_WIDTH // cw)
    tn = 512
    g0 = (3 * ATTN_WIDTH + 3 * CONV_WIDTH) // tn
    merged = branch_merge(attn, conv, w_br_attn[0], w_br_conv[0],
                          proj, b_gate[0], ga_col=g0, gc_col=g0 + d // tn, tn=tn)
    h = matmul(merged, w_out[0], F32, residual=h, tn=512, name="out_proj")

    q = norm_matmul(h, norm_ffn[0], peer_w_q[0].astype(BF16), F32, name="peer_query")
    code, gate = peer_route(q, peer_sub_keys[0])
    w = peer_dense_weights(code, gate)
    out = peer_dense_final(h, norm_ffn[0], u_bf16, v_bf16, w, norm_final)
    return out.reshape(batch, seq, d)
```

```python
import functools
import math

import numpy as np
import jax
import jax.numpy as jnp
from jax import lax
from jax.experimental import pallas as pl
from jax.experimental.pallas import tpu as pltpu

F32 = jnp.float32
BF16 = jnp.bfloat16

D_MODEL = 4096
ATTN_HEADS = 16
HEAD_DIM = 128
ATTN_WIDTH = ATTN_HEADS * HEAD_DIM
MOBA_BLOCK = 256
MOBA_TOPK = 3
REL_BUCKETS = 32
REL_MAX_DIST = 128
CONV_WIDTH = 2048
CONV_K = 3
PEER_HEADS = 8
PEER_N_KEYS = 128
PEER_HALF = 128
PEER_TOPK = 16
NORM_EPS = 1e-6

V7X_VMEM_BYTES = 64 * 1024 * 1024
VMEM_LIMIT = V7X_VMEM_BYTES - 8 * 1024 * 1024
LANES = 128
SUBLANES = 8
NT_DIMS = (((1,), (1,)), ((), ()))
LOG2_E = math.log2(math.e)


def _params(semantics):
    return pltpu.CompilerParams(dimension_semantics=semantics, vmem_limit_bytes=VMEM_LIMIT)


def _rmsnorm_kernel(x_ref, g_ref, o_ref):
    x = x_ref[...]
    ms = jnp.mean(x * x, axis=-1, keepdims=True)
    o_ref[...] = (x * lax.rsqrt(ms + NORM_EPS) * g_ref[...]).astype(o_ref.dtype)


def rmsnorm(x, g, out_dtype, rows=256):
    n, d = x.shape
    rows = min(rows, n)
    row_spec = pl.BlockSpec((rows, d), lambda i: (i, 0))
    return pl.pallas_call(
        _rmsnorm_kernel, grid=(n // rows,),
        in_specs=[row_spec, pl.BlockSpec((1, d), lambda i: (0, 0))], out_specs=row_spec,
        out_shape=jax.ShapeDtypeStruct((n, d), out_dtype),
        compiler_params=_params(("parallel",)), name="rmsnorm",
    )(x, g.reshape(1, d))


def _matmul_kernel(a_ref, b_ref, *rest, n_col_blocks, has_residual):
    r_ref = rest[0] if has_residual else None
    o_ref, b_scr = rest[-2], rest[-1]
    jj = pl.program_id(0)
    i = pl.program_id(1)
    kc = b_ref.shape[0]

    @pl.when(jj < n_col_blocks)
    def _():
        rows = pl.ds(pl.multiple_of(i * kc, kc), kc)
        b_scr[jj % 2, rows, :] = b_ref[...].astype(BF16)

    @pl.when(jj > 0)
    def _():
        acc = jnp.dot(a_ref[...], b_scr[(jj + 1) % 2], preferred_element_type=F32)
        if has_residual:
            acc = r_ref[...] + acc
        o_ref[...] = acc.astype(o_ref.dtype)


def matmul(a, b, out_dtype, residual=None, tm=1024, tn=1024, name="matmul"):
    m, k = a.shape
    _, n = b.shape
    tm, tn = min(tm, m), min(tn, n)
    ni, nj = m // tm, n // tn
    assert k % ni == 0
    kc = k // ni

    def out_index(jj, i):
        return (jnp.where(jj == 0, 0, i), jnp.maximum(jj - 1, 0))

    a_spec = pl.BlockSpec((tm, k), lambda jj, i: (jnp.where(jj == 0, 0, i), 0))
    b_spec = pl.BlockSpec((kc, tn), lambda jj, i: (jnp.where(jj < nj, i, ni - 1), jnp.minimum(jj, nj - 1)))
    o_spec = pl.BlockSpec((tm, tn), out_index)
    ins, specs = (a, b), [a_spec, b_spec]
    if residual is not None:
        ins, specs = ins + (residual,), specs + [o_spec]
    kern = functools.partial(_matmul_kernel, n_col_blocks=nj, has_residual=residual is not None)
    return pl.pallas_call(
        kern, grid=(nj + 1, ni), in_specs=specs, out_specs=o_spec,
        out_shape=jax.ShapeDtypeStruct((m, n), out_dtype),
        scratch_shapes=[pltpu.VMEM((2, k, tn), BF16)],
        compiler_params=_params(("arbitrary", "arbitrary")), name=name,
    )(*ins)


NORM_CHUNK_ROWS = 64


def _rmsnorm_chunks(load_rows, g_ref, dst_ref, unrolled=False):
    def body(c, carry):
        start = c * NORM_CHUNK_ROWS
        rows = pl.ds(start if unrolled else pl.multiple_of(start, NORM_CHUNK_ROWS), NORM_CHUNK_ROWS)
        x = load_rows(rows)
        ms = jnp.mean(x * x, axis=-1, keepdims=True)
        dst_ref[rows, :] = (x * lax.rsqrt(ms + NORM_EPS) * g_ref[...]).astype(dst_ref.dtype)
        return carry

    n_chunks = dst_ref.shape[0] // NORM_CHUNK_ROWS
    if unrolled:
        for c in range(n_chunks):
            body(c, 0)
    else:
        lax.fori_loop(0, n_chunks, body, 0)


def _norm_matmul_kernel(x_ref, g_ref, b_ref, o_ref, hn_ref, hn_scr, *, n_row_tiles):
    s = pl.program_id(0)

    def normalise():
        _rmsnorm_chunks(lambda rows: x_ref[rows, :], g_ref, hn_scr.at[s % 2], unrolled=True)
        hn_ref[...] = hn_scr[s % 2]

    def multiply():
        o_ref[...] = jnp.dot(hn_scr[(s + 1) % 2], b_ref[...], preferred_element_type=F32).astype(o_ref.dtype)

    @pl.when(s == 0)
    def _():
        normalise()

    @pl.when((s > 0) & (s < n_row_tiles))
    def _():
        multiply()
        normalise()

    @pl.when(s == n_row_tiles)
    def _():
        multiply()


def norm_matmul(x, g, b, out_dtype, tm=256, name="norm_matmul"):
    m, k = x.shape
    _, n = b.shape
    tm = min(tm, m)
    ni = m // tm
    kern = functools.partial(_norm_matmul_kernel, n_row_tiles=ni)
    row_tile = lambda s: (jnp.minimum(s, ni - 1), 0)
    return pl.pallas_call(
        kern, grid=(ni + 1,),
        in_specs=[pl.BlockSpec((tm, k), row_tile),
                  pl.BlockSpec((1, k), lambda s: (0, 0)),
                  pl.BlockSpec((k, n), lambda s: (0, 0), pipeline_mode=pl.Buffered(1))],
        out_specs=[pl.BlockSpec((tm, n), lambda s: (jnp.maximum(s - 1, 0), 0)),
                   pl.BlockSpec((tm, k), row_tile)],
        out_shape=[jax.ShapeDtypeStruct((m, n), out_dtype), jax.ShapeDtypeStruct((m, k), BF16)],
        scratch_shapes=[pltpu.VMEM((2, tm, k), BF16)],
        compiler_params=_params(("arbitrary",)), name=name,
    )(x, g.reshape(1, k), b)


def _t5_bucket_table(max_dist):
    dist = np.arange(max_dist, dtype=np.int32)
    max_exact = REL_BUCKETS // 2
    d32 = np.maximum(dist, 1).astype(np.float32)
    ratio = (np.log(d32 / np.float32(max_exact)) / np.float32(math.log(REL_MAX_DIST / max_exact))
             * np.float32(REL_BUCKETS - max_exact))
    large = max_exact + ratio.astype(np.int32)
    large = np.minimum(large, REL_BUCKETS - 1)
    return np.where(dist < max_exact, dist, large).astype(np.int32)


def _moba_bucket_tile():
    L = MOBA_BLOCK
    key = np.arange(2 * L)[:, None]
    qi = np.arange(L)[None, :]
    dist = np.maximum(qi - key + L, 0)
    return _t5_bucket_table(2 * L)[dist]


def _moba_kernel(tab_ref, q_ref, k_ref, v_ref, bucket_ref, *rest, n_blocks, n_side, steps_per_table):
    side_in, o_ref, side_out, bias_scr = rest[:n_side], rest[n_side], rest[n_side + 1:-1], rest[-1]
    L = MOBA_BLOCK
    h = pl.program_id(0)
    b = pl.program_id(1)

    step = h * pl.num_programs(1) + b
    for t in range(n_side):
        @pl.when((step >= t * steps_per_table) & (step < (t + 1) * steps_per_table))
        def _(t=t):
            def body(c, carry):
                rows = pl.ds(pl.multiple_of(c * NORM_CHUNK_ROWS, NORM_CHUNK_ROWS), NORM_CHUNK_ROWS)
                side_out[t][rows, :] = side_in[t][rows, :].astype(BF16)
                return carry
            lax.fori_loop(0, side_in[t].shape[0] // NORM_CHUNK_ROWS, body, 0)

    @pl.when(b == 0)
    def _():
        bk = bucket_ref[...]
        acc = jnp.zeros(bk.shape, F32)
        for kk in range(REL_BUCKETS):
            acc = jnp.where(bk == kk, tab_ref[kk, h], acc)
        bias_scr[...] = acc * LOG2_E

    neg_inf = jnp.float32(-jnp.inf)
    far = tab_ref[REL_BUCKETS - 1, h] * LOG2_E
    scale = HEAD_DIM ** -0.5 * LOG2_E
    k_means = [jnp.mean(k_ref[n * L:(n + 1) * L, :].astype(F32), axis=0, keepdims=True)
               for n in range(n_blocks)]
    pad = jnp.zeros((SUBLANES - n_blocks % SUBLANES, HEAD_DIM), F32) if n_blocks % SUBLANES else None
    k_mean = jnp.concatenate(k_means + ([pad] if pad is not None else []), axis=0)
    v_t = v_ref[...].astype(F32).T.astype(BF16)
    key_i = lax.broadcasted_iota(jnp.int32, (L, L), 0)
    qry_i = lax.broadcasted_iota(jnp.int32, (L, L), 1)

    for qb in range(n_blocks):
        q = q_ref[qb * L:(qb + 1) * L, :]
        sel = None
        if qb > MOBA_TOPK:
            gate = lax.dot_general(k_mean, q.astype(F32), NT_DIMS,
                                   precision=lax.Precision.HIGHEST, preferred_element_type=F32)
            g = [gate[m:m + 1, :] for m in range(qb)]
            sel = []
            for n in range(qb):
                rank = jnp.zeros((1, L), F32)
                for m in range(qb):
                    if m != n:
                        beats = (g[m] > g[n]) | (g[m] == g[n]) if m < n else (g[m] > g[n])
                        rank = rank + jnp.where(beats, 1.0, 0.0)
                sel.append(jnp.where(rank < MOBA_TOPK, 1.0, 0.0))
        logits = []
        for n in range(qb + 1):
            s = lax.dot_general(k_ref[n * L:(n + 1) * L, :], q, NT_DIMS, preferred_element_type=F32)
            if n == qb:
                logits.append(jnp.where(qry_i >= key_i, s * scale + bias_scr[L:2 * L, :], neg_inf))
            else:
                lg = s * scale + (bias_scr[0:L, :] if n == qb - 1 else far)
                logits.append(lg if sel is None else jnp.where(sel[n] > 0.5, lg, neg_inf))
        mx = logits[0].max(axis=0, keepdims=True)
        for lg in logits[1:]:
            mx = jnp.maximum(mx, lg.max(axis=0, keepdims=True))
        denom = jnp.zeros((1, L), F32)
        acc = jnp.zeros((HEAD_DIM, L), F32)
        for n, lg in enumerate(logits):
            p = jnp.exp2(lg - mx)
            denom = denom + p.sum(axis=0, keepdims=True)
            acc = acc + jnp.dot(v_t[:, n * L:(n + 1) * L], p.astype(BF16), preferred_element_type=F32)
        o_ref[qb * L:(qb + 1) * L, :] = (acc / denom).T.astype(o_ref.dtype)


def moba_attention(proj, rel_bias, batch, seq, q_col, k_col, v_col, side_tables=()):
    L = MOBA_BLOCK
    n_blocks = seq // L
    bucket = jnp.asarray(_moba_bucket_tile())
    n_side = len(side_tables)
    n_steps = ATTN_HEADS * batch
    steps_per_table = n_steps // max(n_side, 1)
    side_specs, side_shapes = [], []
    for t, tab in enumerate(side_tables):
        rows, cols = tab.shape
        assert rows % steps_per_table == 0
        def index_map(h, b, t=t):
            return (jnp.clip(h * batch + b - t * steps_per_table, 0, steps_per_table - 1), 0)
        side_specs.append(pl.BlockSpec((rows // steps_per_table, cols), index_map))
        side_shapes.append(jax.ShapeDtypeStruct(tab.shape, BF16))
    kern = functools.partial(_moba_kernel, n_blocks=n_blocks, n_side=n_side, steps_per_table=steps_per_table)
    outs = pl.pallas_call(
        kern,
        grid=(ATTN_HEADS, batch),
        in_specs=[
            pl.BlockSpec(memory_space=pltpu.SMEM),
            pl.BlockSpec((seq, HEAD_DIM), lambda h, b: (b, q_col + h)),
            pl.BlockSpec((seq, HEAD_DIM), lambda h, b: (b, k_col + h)),
            pl.BlockSpec((seq, HEAD_DIM), lambda h, b: (b, v_col + h)),
            pl.BlockSpec((2 * L, L), lambda h, b: (0, 0)),
        ] + side_specs,
        out_specs=[pl.BlockSpec((seq, HEAD_DIM), lambda h, b: (b, h))] + side_specs,
        out_shape=[jax.ShapeDtypeStruct((batch * seq, ATTN_WIDTH), BF16)] + side_shapes,
        scratch_shapes=[pltpu.VMEM((2 * L, L), F32)],
        compiler_params=_params(("arbitrary", "arbitrary")), name="moba",
    )(rel_bias, proj, proj, proj, bucket, *side_tables)
    return outs[0], tuple(outs[1:])


def _conv_kernel(cb_ref, cc_ref, cu_ref, w_ref, b_ref, o_ref):
    g = cc_ref[...].astype(F32) * cu_ref[...].astype(F32)
    row = lax.broadcasted_iota(jnp.int32, g.shape, 0)
    y = w_ref[CONV_K - 1:CONV_K, :] * g
    for s in range(1, CONV_K):
        shifted = jnp.where(row >= s, pltpu.roll(g, s, axis=0), 0.0)
        y = y + w_ref[CONV_K - 1 - s:CONV_K - s, :] * shifted
    o_ref[...] = (cb_ref[...].astype(F32) * (y + b_ref[...])).astype(o_ref.dtype)


def short_conv(proj, conv_w, conv_b, batch, seq, cb_col, cc_col, cu_col, cw=512):
    width = conv_w.shape[1]
    return pl.pallas_call(
        _conv_kernel,
        grid=(batch, width // cw),
        in_specs=[
            pl.BlockSpec((seq, cw), lambda b, c: (b, cb_col + c)),
            pl.BlockSpec((seq, cw), lambda b, c: (b, cc_col + c)),
            pl.BlockSpec((seq, cw), lambda b, c: (b, cu_col + c)),
            pl.BlockSpec((CONV_K, cw), lambda b, c: (0, c)),
            pl.BlockSpec((1, cw), lambda b, c: (0, c)),
        ],
        out_specs=pl.BlockSpec((seq, cw), lambda b, c: (b, c)),
        out_shape=jax.ShapeDtypeStruct((batch * seq, width), BF16),
        compiler_params=_params(("parallel", "parallel")), name="short_conv",
    )(proj, proj, proj, conv_w, conv_b)


def _merge_kernel(attn_ref, conv_ref, wa_ref, wc_ref, ga_ref, gc_ref, bg_ref, o_ref, wa_scr, wc_scr,
                  *, n_col_blocks):
    jj = pl.program_id(0)
    i = pl.program_id(1)

    @pl.when(jj < n_col_blocks)
    def _():
        for w_ref, w_scr in ((wa_ref, wa_scr), (wc_ref, wc_scr)):
            kc = w_ref.shape[0]
            w_scr[jj % 2, pl.ds(pl.multiple_of(i * kc, kc), kc), :] = w_ref[...].astype(BF16)

    @pl.when(jj > 0)
    def _():
        za = jnp.dot(attn_ref[...], wa_scr[(jj + 1) % 2], preferred_element_type=F32)
        zc = jnp.dot(conv_ref[...], wc_scr[(jj + 1) % 2], preferred_element_type=F32)
        ga = jax.nn.sigmoid(ga_ref[...].astype(F32) + bg_ref[0:1, :])
        gc = jax.nn.sigmoid(gc_ref[...].astype(F32) + bg_ref[1:2, :])
        o_ref[...] = (ga * za + gc * zc).astype(o_ref.dtype)


def branch_merge(attn, conv, wa, wc, proj, b_gate, ga_col, gc_col, tm=512, tn=1024):
    m, ka = attn.shape
    kc = conv.shape[1]
    n = wa.shape[1]
    tm, tn = min(tm, m), min(tn, n)
    ni, nj = m // tm, n // tn
    assert ka % ni == 0 and kc % ni == 0

    def rows(jj, i):
        return jnp.where(jj == 0, 0, i)

    def col(jj):
        return jnp.maximum(jj - 1, 0)

    def weight_index(jj, i):
        return (jnp.where(jj < nj, i, ni - 1), jnp.minimum(jj, nj - 1))

    kern = functools.partial(_merge_kernel, n_col_blocks=nj)
    return pl.pallas_call(
        kern,
        grid=(nj + 1, ni),
        in_specs=[
            pl.BlockSpec((tm, ka), lambda jj, i: (rows(jj, i), 0)),
            pl.BlockSpec((tm, kc), lambda jj, i: (rows(jj, i), 0)),
            pl.BlockSpec((ka // ni, tn), weight_index),
            pl.BlockSpec((kc // ni, tn), weight_index),
            pl.BlockSpec((tm, tn), lambda jj, i: (rows(jj, i), ga_col + col(jj))),
            pl.BlockSpec((tm, tn), lambda jj, i: (rows(jj, i), gc_col + col(jj))),
            pl.BlockSpec((2, tn), lambda jj, i: (0, col(jj))),
        ],
        out_specs=pl.BlockSpec((tm, tn), lambda jj, i: (rows(jj, i), col(jj))),
        out_shape=jax.ShapeDtypeStruct((m, n), BF16),
        scratch_shapes=[pltpu.VMEM((2, ka, tn), BF16), pltpu.VMEM((2, kc, tn), BF16)],
        compiler_params=_params(("arbitrary", "arbitrary")), name="branch_merge",
    )(attn, conv, wa, wc, proj, proj, b_gate)


def _topk_rows(s, k):
    nrows, ncols = s.shape
    rowi = lax.broadcasted_iota(jnp.int32, s.shape, 0).astype(F32)
    slot = lax.broadcasted_iota(jnp.int32, (k, ncols), 0)
    vals = jnp.zeros((k, ncols), F32)
    rows = jnp.zeros((k, ncols), F32)
    for r in range(k):
        m = s.max(axis=0, keepdims=True)
        idx = jnp.where(s == m, rowi, float(nrows)).min(axis=0, keepdims=True)
        s = jnp.where(rowi == idx, -jnp.inf, s)
        vals = jnp.where(slot == r, m, vals)
        rows = jnp.where(slot == r, idx, rows)
    return vals, rows


def _topk_rows_untied(s, k):
    nrows, ncols = s.shape
    lane = lax.broadcasted_iota(jnp.int32, (SUBLANES, nrows), 1).astype(F32)
    sub = lax.broadcasted_iota(jnp.int32, (SUBLANES, nrows), 0)
    probe = jnp.where(sub == 0, lane, jnp.where(sub == 1, 1.0, 0.0))
    slot = lax.broadcasted_iota(jnp.int32, (k, ncols), 0)
    vals = jnp.zeros((k, ncols), F32)
    rows = jnp.zeros((k, ncols), F32)
    worst = jnp.zeros((1, ncols), F32)
    for r in range(k):
        m = s.max(axis=0, keepdims=True)
        hit = s == m
        s = jnp.where(hit, -jnp.inf, s)
        stats = jnp.dot(probe, jnp.where(hit, 1.0, 0.0), preferred_element_type=F32)
        vals = jnp.where(slot == r, m, vals)
        rows = jnp.where(slot == r, stats[0:1], rows)
        worst = jnp.maximum(worst, stats[1:2])
    return vals, rows, worst


def _pair_candidates(v1, i1, v2, i2):
    K = PEER_TOPK
    ncols = v1.shape[1]
    sub8 = lax.broadcasted_iota(jnp.int32, (SUBLANES, ncols), 0)
    sub16 = lax.broadcasted_iota(jnp.int32, (K, ncols), 0)
    nk = float(PEER_N_KEYS)
    big = float(K * K)
    vals = [v1[0:1] + v2]
    codes = [i1[0:1] * nk + i2]
    flat = [sub16.astype(F32)]
    for a in range(1, SUBLANES):
        ok = sub8 < K // (a + 1)
        vals.append(jnp.where(ok, v1[a:a + 1] + v2[0:SUBLANES], -jnp.inf))
        codes.append(i1[a:a + 1] * nk + i2[0:SUBLANES])
        flat.append(jnp.where(ok, (sub8 + a * K).astype(F32), big))
    vals.append(v1[SUBLANES:K] + v2[0:1])
    codes.append(i1[SUBLANES:K] * nk + i2[0:1])
    flat.append(((sub8 + SUBLANES) * K).astype(F32))
    return jnp.concatenate(vals, axis=0), jnp.concatenate(codes, axis=0), jnp.concatenate(flat, axis=0)


def _peer_route_kernel(q_ref, keys_ref, code_ref, gate_ref, s1_scr, s2_scr, code_scr, gate_scr, topk_scr,
                       *, chunk_tokens):
    K = PEER_TOPK
    h = pl.program_id(1)
    q = q_ref[...]
    s1_scr[...] = lax.dot_general(keys_ref[0, 0], q[:, :PEER_HALF], NT_DIMS,
                                  precision=lax.Precision.HIGHEST, preferred_element_type=F32)
    s2_scr[...] = lax.dot_general(keys_ref[0, 1], q[:, PEER_HALF:], NT_DIMS,
                                  precision=lax.Precision.HIGHEST, preferred_element_type=F32)
    row0 = pl.multiple_of(h * K, K)

    def chunk(c, carry):
        off = pl.multiple_of(c * chunk_tokens, chunk_tokens)
        cols = pl.ds(off, chunk_tokens)
        v1, i1, t1 = _topk_rows_untied(s1_scr[:, cols], K)
        v2, i2, t2 = _topk_rows_untied(s2_scr[:, cols], K)
        for slot_i, val in enumerate((v1, i1, v2, i2)):
            topk_scr[slot_i] = val

        @pl.when(jnp.max(jnp.maximum(t1, t2)) > 1.5)
        def _():
            for slot_i, s_scr in ((0, s1_scr), (2, s2_scr)):
                v, i = _topk_rows(s_scr[:, cols], K)
                topk_scr[slot_i] = v
                topk_scr[slot_i + 1] = i

        cand, code, flat = _pair_candidates(topk_scr[0], topk_scr[1], topk_scr[2], topk_scr[3])
        slot = lax.broadcasted_iota(jnp.int32, (K, chunk_tokens), 0)
        score = jnp.zeros((K, chunk_tokens), F32)
        picked = jnp.zeros((K, chunk_tokens), F32)
        for r in range(K):
            m = cand.max(axis=0, keepdims=True)
            first = jnp.where(cand == m, flat, float(K * K)).min(axis=0, keepdims=True)
            hit = flat == first
            ck = jnp.where(hit, code, -1.0).max(axis=0, keepdims=True)
            cand = jnp.where(hit, -jnp.inf, cand)
            score = jnp.where(slot == r, m, score)
            picked = jnp.where(slot == r, ck, picked)
        e = jnp.exp(score - score[0:1])
        code_scr[pl.ds(row0, K), pl.ds(off, chunk_tokens)] = picked
        gate_scr[pl.ds(row0, K), pl.ds(off, chunk_tokens)] = e / e.sum(axis=0, keepdims=True)
        return carry

    lax.fori_loop(0, q.shape[0] // chunk_tokens, chunk, 0)

    @pl.when(h == PEER_HEADS - 1)
    def _():
        code_ref[...] = code_scr[...].T.astype(jnp.int32)
        gate_ref[...] = gate_scr[...].T


def peer_route(q, sub_keys, tt=1024, chunk_tokens=1024):
    n = q.shape[0]
    tt = min(tt, n)
    chunk_tokens = min(chunk_tokens, tt)
    qd = 2 * PEER_HALF
    picks = PEER_HEADS * PEER_TOPK
    out_spec = pl.BlockSpec((tt, picks), lambda i, h: (i, 0))
    return pl.pallas_call(
        functools.partial(_peer_route_kernel, chunk_tokens=chunk_tokens),
        grid=(n // tt, PEER_HEADS),
        in_specs=[
            pl.BlockSpec((tt, qd), lambda i, h: (i, h)),
            pl.BlockSpec((1, 2, PEER_N_KEYS, PEER_HALF), lambda i, h: (h, 0, 0, 0)),
        ],
        out_specs=[out_spec, out_spec],
        out_shape=[jax.ShapeDtypeStruct((n, picks), jnp.int32),
                   jax.ShapeDtypeStruct((n, picks), F32)],
        scratch_shapes=[pltpu.VMEM((PEER_N_KEYS, tt), F32), pltpu.VMEM((PEER_N_KEYS, tt), F32),
                        pltpu.VMEM((picks, tt), F32), pltpu.VMEM((picks, tt), F32),
                        pltpu.VMEM((4, PEER_TOPK, chunk_tokens), F32)],
        compiler_params=_params(("parallel", "arbitrary")), name="peer_route",
    )(q, sub_keys)


W_ROW_PITCH = PEER_N_KEYS + SUBLANES


def _peer_weights_kernel(code_ref, gate_ref, o_ref, w_scr, *, unroll):
    nk = PEER_N_KEYS
    tt, picks = code_ref.shape
    sub = lax.broadcasted_iota(jnp.int32, (nk, picks), 0)
    zero = jnp.zeros((nk, picks), BF16)

    def one_hots(t):
        c = code_ref[pl.ds(t, 1), :]
        g = gate_ref[pl.ds(t, 1), :]
        e1 = lax.shift_right_logical(c, 7)
        e2 = lax.bitwise_and(c, nk - 1)
        at = jnp.where(sub == e1, g, 0.0).astype(BF16)
        bt = jnp.where(sub == e2, 1.0, 0.0).astype(BF16)
        return at, bt

    def body(p, carry):
        t = 2 * p
        at0, bt0 = one_hots(t)
        at1, bt1 = one_hots(t + 1)
        at = jnp.concatenate([at0, at1], axis=1)
        bt = jnp.concatenate([jnp.concatenate([bt0, zero], axis=1),
                              jnp.concatenate([zero, bt1], axis=1)], axis=0)
        w = lax.dot_general(at, bt, NT_DIMS, preferred_element_type=F32)
        row = pl.multiple_of(t * W_ROW_PITCH, SUBLANES)
        w_scr[pl.ds(row, nk), :] = w[:, :nk]
        w_scr[pl.ds(row + W_ROW_PITCH, nk), :] = w[:, nk:]
        return carry

    lax.fori_loop(0, tt // 2, body, 0, unroll=unroll)
    for e1 in range(nk):
        o_ref[:, e1 * nk:(e1 + 1) * nk] = w_scr[pl.ds(e1, tt, stride=W_ROW_PITCH), :].astype(o_ref.dtype)


def peer_dense_weights(code, gate, tt=128, unroll=64):
    n, picks = code.shape
    tt = min(tt, n)
    nk = PEER_N_KEYS
    kern = functools.partial(_peer_weights_kernel, unroll=unroll)
    return pl.pallas_call(
        kern,
        grid=(n // tt,),
        in_specs=[pl.BlockSpec((tt, picks), lambda i: (i, 0)),
                  pl.BlockSpec((tt, picks), lambda i: (i, 0))],
        out_specs=pl.BlockSpec((tt, nk * nk), lambda i: (i, 0)),
        out_shape=jax.ShapeDtypeStruct((n, nk * nk), BF16),
        scratch_shapes=[pltpu.VMEM((tt * W_ROW_PITCH, nk), F32)],
        compiler_params=_params(("parallel",)), name="peer_weights",
    )(code, gate)


def _row_rms_scale(t, g):
    return t * lax.rsqrt(jnp.mean(t * t, axis=-1, keepdims=True) + NORM_EPS) * g


def _peer_dense_kernel(x_ref, u_ref, v_ref, w_ref, h_ref, gout_ref, o_ref, acc_scr, *, n_edge, n_main):
    j = pl.program_id(1)
    edge_rows = h_ref.shape[0]

    @pl.when(j == 0)
    def _():
        acc_scr[...] = jnp.zeros(acc_scr.shape, F32)

    @pl.when(j < n_main)
    def _():
        a = lax.dot_general(x_ref[...], u_ref[...], NT_DIMS, preferred_element_type=F32)
        act = 0.5 * a * (1.0 + lax.erf(a * math.sqrt(0.5))) * w_ref[...].astype(F32)
        acc_scr[...] += jnp.dot(act.astype(BF16), v_ref[...], preferred_element_type=F32)

    @pl.when(j >= n_main)
    def _():
        r = j - n_main
        rows = pl.ds(pl.multiple_of(r * edge_rows, edge_rows), edge_rows)
        o_ref[...] = _row_rms_scale(h_ref[...] + acc_scr[rows, :], gout_ref[...])


def peer_dense_final(x, u, v, w, h, g_out, tn=1024, te=512, edge_rows=128):
    n, d = h.shape
    ne = u.shape[0]
    tn, te = min(tn, n), min(te, ne)
    edge_rows = min(edge_rows, tn)
    n_edge, n_main = tn // edge_rows, ne // te

    def main_step(j):
        return jnp.minimum(j, n_main - 1)

    def edge_chunk(i, j):
        return (i * n_edge + jnp.maximum(j - n_main, 0), 0)

    kern = functools.partial(_peer_dense_kernel, n_edge=n_edge, n_main=n_main)
    return pl.pallas_call(
        kern,
        grid=(n // tn, n_main + n_edge),
        in_specs=[
            pl.BlockSpec((tn, d), lambda i, j: (i, 0), pipeline_mode=pl.Buffered(1)),
            pl.BlockSpec((te, d), lambda i, j: (main_step(j), 0)),
            pl.BlockSpec((te, d), lambda i, j: (main_step(j), 0)),
            pl.BlockSpec((tn, te), lambda i, j: (i, main_step(j))),
            pl.BlockSpec((edge_rows, d), edge_chunk),
            pl.BlockSpec((1, d), lambda i, j: (0, 0)),
        ],
        out_specs=pl.BlockSpec((edge_rows, d), edge_chunk),
        out_shape=jax.ShapeDtypeStruct((n, d), F32),
        scratch_shapes=[pltpu.VMEM((tn, d), F32)],
        compiler_params=_params(("parallel", "arbitrary")), name="peer_dense",
    )(x, u, v, w, h, g_out.reshape(1, d))


def kernel(x, norm_mix, w_in, conv_w, conv_b, w_br_attn, w_br_conv, b_gate, rel_bias,
           w_out, norm_ffn, peer_w_q, peer_sub_keys, peer_u, peer_v, norm_final):
    batch, seq, d = x.shape
    n = batch * seq
    assert norm_mix.shape[0] == 1, "single-layer problem"
    h = x.reshape(n, d)

    hn = rmsnorm(h, norm_mix[0], BF16)
    proj = matmul(hn, w_in[0], BF16, name="in_proj")
    attn, (u_bf16, v_bf16) = moba_attention(
        proj, rel_bias, batch, seq, q_col=0, k_col=ATTN_WIDTH // HEAD_DIM, v_col=2 * ATTN_WIDTH // HEAD_DIM,
        side_tables=(peer_u[0], peer_v[0]))
    cw = 512
    c0 = 3 * ATTN_WIDTH // cw
    conv = short_conv(proj, conv_w[0].reshape(CONV_K, CONV_WIDTH), conv_b[0].reshape(1, CONV_WIDTH),
                      batch, seq, cb_col=c0, cc_col=c0 + CONV_WIDTH // cw, cu_col=c0 + 2 * CONV_WIDTH // cw)
    tn = 1024
    g0 = (3 * ATTN_WIDTH + 3 * CONV_WIDTH) // tn
    merged = branch_merge(attn, conv, w_br_attn[0], w_br_conv[0],
                          proj, b_gate[0], ga_col=g0, gc_col=g0 + d // tn, tn=tn)
    h = matmul(merged, w_out[0], F32, residual=h, tm=512, tn=1024, name="out_proj")

    q, hn2 = norm_matmul(h, norm_ffn[0], peer_w_q[0].astype(BF16), F32, name="peer_query")
    code, gate = peer_route(q, peer_sub_keys[0])
    w = peer_dense_weights(code, gate)
    out = peer_dense_final(hn2, u_bf16, v_bf16, w, h, norm_final)
    return out.reshape(batch, seq, d)
```

```python
import functools
import math

import numpy as np
import jax
import jax.numpy as jnp
from jax import lax
from jax.experimental import pallas as pl
from jax.experimental.pallas import tpu as pltpu

F32 = jnp.float32
BF16 = jnp.bfloat16

D_MODEL = 4096
ATTN_HEADS = 16
HEAD_DIM = 128
ATTN_WIDTH = ATTN_HEADS * HEAD_DIM
MOBA_BLOCK = 256
MOBA_TOPK = 3
REL_BUCKETS = 32
REL_MAX_DIST = 128
CONV_WIDTH = 2048
CONV_K = 3
PEER_HEADS = 8
PEER_N_KEYS = 128
PEER_HALF = 128
PEER_TOPK = 16
NORM_EPS = 1e-6

V7X_VMEM_BYTES = 64 * 1024 * 1024
VMEM_LIMIT = V7X_VMEM_BYTES - 8 * 1024 * 1024
LANES = 128
SUBLANES = 8
NT_DIMS = (((1,), (1,)), ((), ()))
LOG2_E = math.log2(math.e)


def _params(semantics):
    return pltpu.CompilerParams(dimension_semantics=semantics, vmem_limit_bytes=VMEM_LIMIT)


def _rmsnorm_kernel(x_ref, g_ref, o_ref):
    x = x_ref[...]
    ms = jnp.mean(x * x, axis=-1, keepdims=True)
    o_ref[...] = (x * lax.rsqrt(ms + NORM_EPS) * g_ref[...]).astype(o_ref.dtype)


def rmsnorm(x, g, out_dtype, rows=256):
    n, d = x.shape
    rows = min(rows, n)
    row_spec = pl.BlockSpec((rows, d), lambda i: (i, 0))
    return pl.pallas_call(
        _rmsnorm_kernel, grid=(n // rows,),
        in_specs=[row_spec, pl.BlockSpec((1, d), lambda i: (0, 0))], out_specs=row_spec,
        out_shape=jax.ShapeDtypeStruct((n, d), out_dtype),
        compiler_params=_params(("parallel",)), name="rmsnorm",
    )(x, g.reshape(1, d))


def _matmul_kernel(a_ref, b_ref, *rest, n_col_blocks, has_residual):
    r_ref = rest[0] if has_residual else None
    o_ref, b_scr = rest[-2], rest[-1]
    jj = pl.program_id(0)
    i = pl.program_id(1)
    kc = b_ref.shape[0]

    def round_chunk():
        rows = pl.ds(pl.multiple_of(i * kc, kc), kc)
        b_scr[jj % 2, rows, :] = b_ref[...].astype(BF16)

    def multiply():
        acc = jnp.dot(a_ref[...], b_scr[(jj + 1) % 2], preferred_element_type=F32)
        if has_residual:
            acc = r_ref[...] + acc
        o_ref[...] = acc.astype(o_ref.dtype)

    pl.when(jj == 0)(round_chunk)
    pl.when(jj == n_col_blocks)(multiply)

    @pl.when((jj > 0) & (jj < n_col_blocks))
    def _():
        multiply()
        round_chunk()


def matmul(a, b, out_dtype, residual=None, tm=1024, tn=1024, name="matmul"):
    m, k = a.shape
    _, n = b.shape
    tm, tn = min(tm, m), min(tn, n)
    ni, nj = m // tm, n // tn
    assert k % ni == 0
    kc = k // ni

    def out_index(jj, i):
        return (jnp.where(jj == 0, 0, i), jnp.maximum(jj - 1, 0))

    a_spec = pl.BlockSpec((tm, k), lambda jj, i: (jnp.where(jj == 0, 0, i), 0))
    b_spec = pl.BlockSpec((kc, tn), lambda jj, i: (jnp.where(jj < nj, i, ni - 1), jnp.minimum(jj, nj - 1)))
    o_spec = pl.BlockSpec((tm, tn), out_index)
    ins, specs = (a, b), [a_spec, b_spec]
    if residual is not None:
        ins, specs = ins + (residual,), specs + [o_spec]
    kern = functools.partial(_matmul_kernel, n_col_blocks=nj, has_residual=residual is not None)
    return pl.pallas_call(
        kern, grid=(nj + 1, ni), in_specs=specs, out_specs=o_spec,
        out_shape=jax.ShapeDtypeStruct((m, n), out_dtype),
        scratch_shapes=[pltpu.VMEM((2, k, tn), BF16)],
        compiler_params=_params(("arbitrary", "arbitrary")), name=name,
    )(*ins)


NORM_CHUNK_ROWS = 64


def _rmsnorm_chunks(load_rows, g_ref, dst_ref, unrolled=False):
    def body(c, carry):
        start = c * NORM_CHUNK_ROWS
        rows = pl.ds(start if unrolled else pl.multiple_of(start, NORM_CHUNK_ROWS), NORM_CHUNK_ROWS)
        x = load_rows(rows)
        ms = jnp.mean(x * x, axis=-1, keepdims=True)
        dst_ref[rows, :] = (x * lax.rsqrt(ms + NORM_EPS) * g_ref[...]).astype(dst_ref.dtype)
        return carry

    n_chunks = dst_ref.shape[0] // NORM_CHUNK_ROWS
    if unrolled:
        for c in range(n_chunks):
            body(c, 0)
    else:
        lax.fori_loop(0, n_chunks, body, 0)


def _norm_matmul_kernel(x_ref, g_ref, b_ref, o_ref, hn_ref, hn_scr, *, n_row_tiles):
    s = pl.program_id(0)

    def normalise():
        _rmsnorm_chunks(lambda rows: x_ref[rows, :], g_ref, hn_scr.at[s % 2], unrolled=True)
        hn_ref[...] = hn_scr[s % 2]

    def multiply():
        o_ref[...] = jnp.dot(hn_scr[(s + 1) % 2], b_ref[...], preferred_element_type=F32).astype(o_ref.dtype)

    @pl.when(s == 0)
    def _():
        normalise()

    @pl.when((s > 0) & (s < n_row_tiles))
    def _():
        multiply()
        normalise()

    @pl.when(s == n_row_tiles)
    def _():
        multiply()


def norm_matmul(x, g, b, out_dtype, tm=256, name="norm_matmul"):
    m, k = x.shape
    _, n = b.shape
    tm = min(tm, m)
    ni = m // tm
    kern = functools.partial(_norm_matmul_kernel, n_row_tiles=ni)
    row_tile = lambda s: (jnp.minimum(s, ni - 1), 0)
    return pl.pallas_call(
        kern, grid=(ni + 1,),
        in_specs=[pl.BlockSpec((tm, k), row_tile),
                  pl.BlockSpec((1, k), lambda s: (0, 0)),
                  pl.BlockSpec((k, n), lambda s: (0, 0), pipeline_mode=pl.Buffered(1))],
        out_specs=[pl.BlockSpec((tm, n), lambda s: (jnp.maximum(s - 1, 0), 0)),
                   pl.BlockSpec((tm, k), row_tile)],
        out_shape=[jax.ShapeDtypeStruct((m, n), out_dtype), jax.ShapeDtypeStruct((m, k), BF16)],
        scratch_shapes=[pltpu.VMEM((2, tm, k), BF16)],
        compiler_params=_params(("arbitrary",)), name=name,
    )(x, g.reshape(1, k), b)


def _t5_bucket_table(max_dist):
    dist = np.arange(max_dist, dtype=np.int32)
    max_exact = REL_BUCKETS // 2
    d32 = np.maximum(dist, 1).astype(np.float32)
    ratio = (np.log(d32 / np.float32(max_exact)) / np.float32(math.log(REL_MAX_DIST / max_exact))
             * np.float32(REL_BUCKETS - max_exact))
    large = max_exact + ratio.astype(np.int32)
    large = np.minimum(large, REL_BUCKETS - 1)
    return np.where(dist < max_exact, dist, large).astype(np.int32)


def _moba_bucket_tile():
    L = MOBA_BLOCK
    key = np.arange(2 * L)[:, None]
    qi = np.arange(L)[None, :]
    dist = np.maximum(qi - key + L, 0)
    return _t5_bucket_table(2 * L)[dist]


def _moba_kernel(tab_ref, q_ref, k_ref, v_ref, bucket_ref, *rest, n_blocks, n_side, steps_per_table):
    side_in, o_ref, side_out, bias_scr = rest[:n_side], rest[n_side], rest[n_side + 1:-1], rest[-1]
    L = MOBA_BLOCK
    h = pl.program_id(0)
    b = pl.program_id(1)

    step = h * pl.num_programs(1) + b
    for t in range(n_side):
        @pl.when((step >= t * steps_per_table) & (step < (t + 1) * steps_per_table))
        def _(t=t):
            def body(c, carry):
                rows = pl.ds(pl.multiple_of(c * NORM_CHUNK_ROWS, NORM_CHUNK_ROWS), NORM_CHUNK_ROWS)
                side_out[t][rows, :] = side_in[t][rows, :].astype(BF16)
                return carry
            lax.fori_loop(0, side_in[t].shape[0] // NORM_CHUNK_ROWS, body, 0)

    @pl.when(b == 0)
    def _():
        bk = bucket_ref[...]
        acc = jnp.zeros(bk.shape, F32)
        for kk in range(REL_BUCKETS):
            acc = jnp.where(bk == kk, tab_ref[kk, h], acc)
        bias_scr[...] = acc * LOG2_E

    neg_inf = jnp.float32(-jnp.inf)
    far = tab_ref[REL_BUCKETS - 1, h] * LOG2_E
    scale = HEAD_DIM ** -0.5 * LOG2_E
    k_means = [jnp.mean(k_ref[n * L:(n + 1) * L, :].astype(F32), axis=0, keepdims=True)
               for n in range(n_blocks)]
    pad = jnp.zeros((SUBLANES - n_blocks % SUBLANES, HEAD_DIM), F32) if n_blocks % SUBLANES else None
    k_mean = jnp.concatenate(k_means + ([pad] if pad is not None else []), axis=0)
    v_t = v_ref[...].astype(F32).T.astype(BF16)
    key_i = lax.broadcasted_iota(jnp.int32, (L, L), 0)
    qry_i = lax.broadcasted_iota(jnp.int32, (L, L), 1)

    for qb in range(n_blocks):
        q = q_ref[qb * L:(qb + 1) * L, :]
        sel = None
        if qb > MOBA_TOPK:
            gate = lax.dot_general(k_mean, q.astype(F32), NT_DIMS,
                                   precision=lax.Precision.HIGHEST, preferred_element_type=F32)
            g = [gate[m:m + 1, :] for m in range(qb)]
            sel = []
            for n in range(qb):
                rank = jnp.zeros((1, L), F32)
                for m in range(qb):
                    if m != n:
                        beats = (g[m] > g[n]) | (g[m] == g[n]) if m < n else (g[m] > g[n])
                        rank = rank + jnp.where(beats, 1.0, 0.0)
                sel.append(jnp.where(rank < MOBA_TOPK, 1.0, 0.0))
        logits = []
        for n in range(qb + 1):
            s = lax.dot_general(k_ref[n * L:(n + 1) * L, :], q, NT_DIMS, preferred_element_type=F32)
            if n == qb:
                logits.append(jnp.where(qry_i >= key_i, s * scale + bias_scr[L:2 * L, :], neg_inf))
            else:
                lg = s * scale + (bias_scr[0:L, :] if n == qb - 1 else far)
                logits.append(lg if sel is None else jnp.where(sel[n] > 0.5, lg, neg_inf))
        mx = logits[0].max(axis=0, keepdims=True)
        for lg in logits[1:]:
            mx = jnp.maximum(mx, lg.max(axis=0, keepdims=True))
        denom = jnp.zeros((1, L), F32)
        acc = jnp.zeros((HEAD_DIM, L), F32)
        for n, lg in enumerate(logits):
            p = jnp.exp2(lg - mx)
            denom = denom + p.sum(axis=0, keepdims=True)
            acc = acc + jnp.dot(v_t[:, n * L:(n + 1) * L], p.astype(BF16), preferred_element_type=F32)
        o_ref[qb * L:(qb + 1) * L, :] = (acc / denom).T.astype(o_ref.dtype)


def moba_attention(proj, rel_bias, batch, seq, q_col, k_col, v_col, side_tables=()):
    L = MOBA_BLOCK
    n_blocks = seq // L
    bucket = jnp.asarray(_moba_bucket_tile())
    n_side = len(side_tables)
    n_steps = ATTN_HEADS * batch
    steps_per_table = n_steps // max(n_side, 1)
    side_specs, side_shapes = [], []
    for t, tab in enumerate(side_tables):
        rows, cols = tab.shape
        assert rows % steps_per_table == 0
        def index_map(h, b, t=t):
            return (jnp.clip(h * batch + b - t * steps_per_table, 0, steps_per_table - 1), 0)
        side_specs.append(pl.BlockSpec((rows // steps_per_table, cols), index_map))
        side_shapes.append(jax.ShapeDtypeStruct(tab.shape, BF16))
    kern = functools.partial(_moba_kernel, n_blocks=n_blocks, n_side=n_side, steps_per_table=steps_per_table)
    outs = pl.pallas_call(
        kern,
        grid=(ATTN_HEADS, batch),
        in_specs=[
            pl.BlockSpec(memory_space=pltpu.SMEM),
            pl.BlockSpec((seq, HEAD_DIM), lambda h, b: (b, q_col + h)),
            pl.BlockSpec((seq, HEAD_DIM), lambda h, b: (b, k_col + h)),
            pl.BlockSpec((seq, HEAD_DIM), lambda h, b: (b, v_col + h)),
            pl.BlockSpec((2 * L, L), lambda h, b: (0, 0)),
        ] + side_specs,
        out_specs=[pl.BlockSpec((seq, HEAD_DIM), lambda h, b: (b, h))] + side_specs,
        out_shape=[jax.ShapeDtypeStruct((batch * seq, ATTN_WIDTH), BF16)] + side_shapes,
        scratch_shapes=[pltpu.VMEM((2 * L, L), F32)],
        compiler_params=_params(("arbitrary", "arbitrary")), name="moba",
    )(rel_bias, proj, proj, proj, bucket, *side_tables)
    return outs[0], tuple(outs[1:])


def _conv_kernel(cb_ref, cc_ref, cu_ref, w_ref, b_ref, o_ref):
    g = cc_ref[...].astype(F32) * cu_ref[...].astype(F32)
    row = lax.broadcasted_iota(jnp.int32, g.shape, 0)
    y = w_ref[CONV_K - 1:CONV_K, :] * g
    for s in range(1, CONV_K):
        shifted = jnp.where(row >= s, pltpu.roll(g, s, axis=0), 0.0)
        y = y + w_ref[CONV_K - 1 - s:CONV_K - s, :] * shifted
    o_ref[...] = (cb_ref[...].astype(F32) * (y + b_ref[...])).astype(o_ref.dtype)


def short_conv(proj, conv_w, conv_b, batch, seq, cb_col, cc_col, cu_col, cw=512):
    width = conv_w.shape[1]
    return pl.pallas_call(
        _conv_kernel,
        grid=(batch, width // cw),
        in_specs=[
            pl.BlockSpec((seq, cw), lambda b, c: (b, cb_col + c)),
            pl.BlockSpec((seq, cw), lambda b, c: (b, cc_col + c)),
            pl.BlockSpec((seq, cw), lambda b, c: (b, cu_col + c)),
            pl.BlockSpec((CONV_K, cw), lambda b, c: (0, c)),
            pl.BlockSpec((1, cw), lambda b, c: (0, c)),
        ],
        out_specs=pl.BlockSpec((seq, cw), lambda b, c: (b, c)),
        out_shape=jax.ShapeDtypeStruct((batch * seq, width), BF16),
        compiler_params=_params(("parallel", "parallel")), name="short_conv",
    )(proj, proj, proj, conv_w, conv_b)


def _merge_kernel(attn_ref, conv_ref, wa_ref, wc_ref, ga_ref, gc_ref, bg_ref, o_ref, wa_scr, wc_scr,
                  *, n_col_blocks):
    jj = pl.program_id(0)
    i = pl.program_id(1)

    def round_chunks():
        for w_ref, w_scr in ((wa_ref, wa_scr), (wc_ref, wc_scr)):
            kc = w_ref.shape[0]
            w_scr[jj % 2, pl.ds(pl.multiple_of(i * kc, kc), kc), :] = w_ref[...].astype(BF16)

    def multiply():
        za = jnp.dot(attn_ref[...], wa_scr[(jj + 1) % 2], preferred_element_type=F32)
        zc = jnp.dot(conv_ref[...], wc_scr[(jj + 1) % 2], preferred_element_type=F32)
        ga = jax.nn.sigmoid(ga_ref[...].astype(F32) + bg_ref[0:1, :])
        gc = jax.nn.sigmoid(gc_ref[...].astype(F32) + bg_ref[1:2, :])
        o_ref[...] = (ga * za + gc * zc).astype(o_ref.dtype)

    pl.when(jj == 0)(round_chunks)
    pl.when(jj == n_col_blocks)(multiply)

    @pl.when((jj > 0) & (jj < n_col_blocks))
    def _():
        multiply()
        round_chunks()


def branch_merge(attn, conv, wa, wc, proj, b_gate, ga_col, gc_col, tm=512, tn=1024):
    m, ka = attn.shape
    kc = conv.shape[1]
    n = wa.shape[1]
    tm, tn = min(tm, m), min(tn, n)
    ni, nj = m // tm, n // tn
    assert ka % ni == 0 and kc % ni == 0

    def rows(jj, i):
        return jnp.where(jj == 0, 0, i)

    def col(jj):
        return jnp.maximum(jj - 1, 0)

    def weight_index(jj, i):
        return (jnp.where(jj < nj, i, ni - 1), jnp.minimum(jj, nj - 1))

    kern = functools.partial(_merge_kernel, n_col_blocks=nj)
    return pl.pallas_call(
        kern,
        grid=(nj + 1, ni),
        in_specs=[
            pl.BlockSpec((tm, ka), lambda jj, i: (rows(jj, i), 0)),
            pl.BlockSpec((tm, kc), lambda jj, i: (rows(jj, i), 0)),
            pl.BlockSpec((ka // ni, tn), weight_index),
            pl.BlockSpec((kc // ni, tn), weight_index),
            pl.BlockSpec((tm, tn), lambda jj, i: (rows(jj, i), ga_col + col(jj))),
            pl.BlockSpec((tm, tn), lambda jj, i: (rows(jj, i), gc_col + col(jj))),
            pl.BlockSpec((2, tn), lambda jj, i: (0, col(jj))),
        ],
        out_specs=pl.BlockSpec((tm, tn), lambda jj, i: (rows(jj, i), col(jj))),
        out_shape=jax.ShapeDtypeStruct((m, n), BF16),
        scratch_shapes=[pltpu.VMEM((2, ka, tn), BF16), pltpu.VMEM((2, kc, tn), BF16)],
        compiler_params=_params(("arbitrary", "arbitrary")), name="branch_merge",
    )(attn, conv, wa, wc, proj, proj, b_gate)


def _topk_rows(s, k):
    nrows, ncols = s.shape
    rowi = lax.broadcasted_iota(jnp.int32, s.shape, 0).astype(F32)
    slot = lax.broadcasted_iota(jnp.int32, (k, ncols), 0)
    vals = jnp.zeros((k, ncols), F32)
    rows = jnp.zeros((k, ncols), F32)
    for r in range(k):
        m = s.max(axis=0, keepdims=True)
        idx = jnp.where(s == m, rowi, float(nrows)).min(axis=0, keepdims=True)
        s = jnp.where(rowi == idx, -jnp.inf, s)
        vals = jnp.where(slot == r, m, vals)
        rows = jnp.where(slot == r, idx, rows)
    return vals, rows


def _topk_rows_untied(s, k):
    nrows, ncols = s.shape
    probe_rows = 2 * SUBLANES
    lane = lax.broadcasted_iota(jnp.int32, (probe_rows, nrows), 1).astype(F32)
    sub = lax.broadcasted_iota(jnp.int32, (probe_rows, nrows), 0)
    probe = jnp.where(sub == 0, lane, jnp.where(sub == 1, 1.0, 0.0)).astype(BF16)
    slot = lax.broadcasted_iota(jnp.int32, (k, ncols), 0)
    vals = jnp.zeros((k, ncols), F32)
    rows = jnp.zeros((k, ncols), F32)
    worst = jnp.zeros((1, ncols), F32)
    for r in range(k):
        m = s.max(axis=0, keepdims=True)
        hit = s == m
        s = jnp.where(hit, -jnp.inf, s)
        stats = jnp.dot(probe, jnp.where(hit, 1.0, 0.0).astype(BF16), preferred_element_type=F32)
        vals = jnp.where(slot == r, m, vals)
        rows = jnp.where(slot == r, stats[0:1], rows)
        worst = jnp.maximum(worst, stats[1:2])
    return vals, rows, worst


def _pair_candidates(v1, i1, v2, i2):
    K = PEER_TOPK
    ncols = v1.shape[1]
    sub8 = lax.broadcasted_iota(jnp.int32, (SUBLANES, ncols), 0)
    sub16 = lax.broadcasted_iota(jnp.int32, (K, ncols), 0)
    nk = float(PEER_N_KEYS)
    big = float(K * K)
    vals = [v1[0:1] + v2]
    codes = [i1[0:1] * nk + i2]
    flat = [sub16.astype(F32)]
    for a in range(1, SUBLANES):
        ok = sub8 < K // (a + 1)
        vals.append(jnp.where(ok, v1[a:a + 1] + v2[0:SUBLANES], -jnp.inf))
        codes.append(i1[a:a + 1] * nk + i2[0:SUBLANES])
        flat.append(jnp.where(ok, (sub8 + a * K).astype(F32), big))
    vals.append(v1[SUBLANES:K] + v2[0:1])
    codes.append(i1[SUBLANES:K] * nk + i2[0:1])
    flat.append(((sub8 + SUBLANES) * K).astype(F32))
    return jnp.concatenate(vals, axis=0), jnp.concatenate(codes, axis=0), jnp.concatenate(flat, axis=0)


def _peer_route_kernel(q_ref, keys_ref, code_ref, gate_ref, s1_scr, s2_scr, code_scr, gate_scr, topk_scr,
                       *, chunk_tokens):
    K = PEER_TOPK
    h = pl.program_id(1)
    q = q_ref[...]
    s1_scr[...] = lax.dot_general(keys_ref[0, 0], q[:, :PEER_HALF], NT_DIMS,
                                  precision=lax.Precision.HIGHEST, preferred_element_type=F32)
    s2_scr[...] = lax.dot_general(keys_ref[0, 1], q[:, PEER_HALF:], NT_DIMS,
                                  precision=lax.Precision.HIGHEST, preferred_element_type=F32)
    row0 = pl.multiple_of(h * K, K)

    def chunk(c, carry):
        off = pl.multiple_of(c * chunk_tokens, chunk_tokens)
        cols = pl.ds(off, chunk_tokens)
        v1, i1, t1 = _topk_rows_untied(s1_scr[:, cols], K)
        v2, i2, t2 = _topk_rows_untied(s2_scr[:, cols], K)
        for slot_i, val in enumerate((v1, i1, v2, i2)):
            topk_scr[slot_i] = val

        @pl.when(jnp.max(jnp.maximum(t1, t2)) > 1.5)
        def _():
            for slot_i, s_scr in ((0, s1_scr), (2, s2_scr)):
                v, i = _topk_rows(s_scr[:, cols], K)
                topk_scr[slot_i] = v
                topk_scr[slot_i + 1] = i

        cand, code, flat = _pair_candidates(topk_scr[0], topk_scr[1], topk_scr[2], topk_scr[3])
        slot = lax.broadcasted_iota(jnp.int32, (K, chunk_tokens), 0)
        score = jnp.zeros((K, chunk_tokens), F32)
        picked = jnp.zeros((K, chunk_tokens), F32)
        for r in range(K):
            m = cand.max(axis=0, keepdims=True)
            first = jnp.where(cand == m, flat, float(K * K)).min(axis=0, keepdims=True)
            hit = flat == first
            ck = jnp.where(hit, code, -1.0).max(axis=0, keepdims=True)
            cand = jnp.where(hit, -jnp.inf, cand)
            score = jnp.where(slot == r, m, score)
            picked = jnp.where(slot == r, ck, picked)
        e = jnp.exp(score - score[0:1])
        code_scr[pl.ds(row0, K), pl.ds(off, chunk_tokens)] = picked
        gate_scr[pl.ds(row0, K), pl.ds(off, chunk_tokens)] = e / e.sum(axis=0, keepdims=True)
        return carry

    lax.fori_loop(0, q.shape[0] // chunk_tokens, chunk, 0)

    @pl.when(h == PEER_HEADS - 1)
    def _():
        code_ref[...] = code_scr[...].T.astype(jnp.int32)
        gate_ref[...] = gate_scr[...].T


def peer_route(q, sub_keys, tt=1024, chunk_tokens=1024):
    n = q.shape[0]
    tt = min(tt, n)
    chunk_tokens = min(chunk_tokens, tt)
    qd = 2 * PEER_HALF
    picks = PEER_HEADS * PEER_TOPK
    out_spec = pl.BlockSpec((tt, picks), lambda i, h: (i, 0))
    return pl.pallas_call(
        functools.partial(_peer_route_kernel, chunk_tokens=chunk_tokens),
        grid=(n // tt, PEER_HEADS),
        in_specs=[
            pl.BlockSpec((tt, qd), lambda i, h: (i, h)),
            pl.BlockSpec((1, 2, PEER_N_KEYS, PEER_HALF), lambda i, h: (h, 0, 0, 0)),
        ],
        out_specs=[out_spec, out_spec],
        out_shape=[jax.ShapeDtypeStruct((n, picks), jnp.int32),
                   jax.ShapeDtypeStruct((n, picks), F32)],
        scratch_shapes=[pltpu.VMEM((PEER_N_KEYS, tt), F32), pltpu.VMEM((PEER_N_KEYS, tt), F32),
                        pltpu.VMEM((picks, tt), F32), pltpu.VMEM((picks, tt), F32),
                        pltpu.VMEM((4, PEER_TOPK, chunk_tokens), F32)],
        compiler_params=_params(("parallel", "arbitrary")), name="peer_route",
    )(q, sub_keys)


W_ROW_PITCH = PEER_N_KEYS + SUBLANES


def _peer_weights_kernel(code_ref, gate_ref, o_ref, w_scr, *, unroll):
    nk = PEER_N_KEYS
    tt, picks = code_ref.shape
    sub = lax.broadcasted_iota(jnp.int32, (nk, picks), 0)
    zero = jnp.zeros((nk, picks), BF16)

    def one_hots(t):
        c = code_ref[pl.ds(t, 1), :]
        g = gate_ref[pl.ds(t, 1), :]
        e1 = lax.shift_right_logical(c, 7)
        e2 = lax.bitwise_and(c, nk - 1)
        at = jnp.where(sub == e1, g, 0.0).astype(BF16)
        bt = jnp.where(sub == e2, 1.0, 0.0).astype(BF16)
        return at, bt

    def body(p, carry):
        t = 2 * p
        at0, bt0 = one_hots(t)
        at1, bt1 = one_hots(t + 1)
        at = jnp.concatenate([at0, at1], axis=1)
        bt = jnp.concatenate([jnp.concatenate([bt0, zero], axis=1),
                              jnp.concatenate([zero, bt1], axis=1)], axis=0)
        w = lax.dot_general(at, bt, NT_DIMS, preferred_element_type=F32)
        row = pl.multiple_of(t * W_ROW_PITCH, SUBLANES)
        w_scr[pl.ds(row, nk), :] = w[:, :nk]
        w_scr[pl.ds(row + W_ROW_PITCH, nk), :] = w[:, nk:]
        return carry

    lax.fori_loop(0, tt // 2, body, 0, unroll=unroll)
    for e1 in range(nk):
        o_ref[:, e1 * nk:(e1 + 1) * nk] = w_scr[pl.ds(e1, tt, stride=W_ROW_PITCH), :].astype(o_ref.dtype)


def peer_dense_weights(code, gate, tt=128, unroll=64):
    n, picks = code.shape
    tt = min(tt, n)
    nk = PEER_N_KEYS
    kern = functools.partial(_peer_weights_kernel, unroll=unroll)
    return pl.pallas_call(
        kern,
        grid=(n // tt,),
        in_specs=[pl.BlockSpec((tt, picks), lambda i: (i, 0)),
                  pl.BlockSpec((tt, picks), lambda i: (i, 0))],
        out_specs=pl.BlockSpec((tt, nk * nk), lambda i: (i, 0)),
        out_shape=jax.ShapeDtypeStruct((n, nk * nk), BF16),
        scratch_shapes=[pltpu.VMEM((tt * W_ROW_PITCH, nk), F32)],
        compiler_params=_params(("parallel",)), name="peer_weights",
    )(code, gate)


def _row_rms_scale(t, g):
    return t * lax.rsqrt(jnp.mean(t * t, axis=-1, keepdims=True) + NORM_EPS) * g


def _peer_dense_kernel(x_ref, u_ref, v_ref, w_ref, h_ref, gout_ref, o_ref, acc_scr, *, n_edge, n_main):
    j = pl.program_id(1)
    edge_rows = h_ref.shape[0]

    @pl.when(j == 0)
    def _():
        acc_scr[...] = jnp.zeros(acc_scr.shape, F32)

    @pl.when(j < n_main)
    def _():
        a = lax.dot_general(x_ref[...], u_ref[...], NT_DIMS, preferred_element_type=F32)
        act = 0.5 * a * (1.0 + lax.erf(a * math.sqrt(0.5))) * w_ref[...].astype(F32)
        acc_scr[...] += jnp.dot(act.astype(BF16), v_ref[...], preferred_element_type=F32)

    @pl.when(j >= n_main)
    def _():
        r = j - n_main
        rows = pl.ds(pl.multiple_of(r * edge_rows, edge_rows), edge_rows)
        o_ref[...] = _row_rms_scale(h_ref[...] + acc_scr[rows, :], gout_ref[...])


def peer_dense_final(x, u, v, w, h, g_out, tn=1024, te=512, edge_rows=128):
    n, d = h.shape
    ne = u.shape[0]
    tn, te = min(tn, n), min(te, ne)
    edge_rows = min(edge_rows, tn)
    n_edge, n_main = tn // edge_rows, ne // te

    def main_step(j):
        return jnp.minimum(j, n_main - 1)

    def edge_chunk(i, j):
        return (i * n_edge + jnp.maximum(j - n_main, 0), 0)

    kern = functools.partial(_peer_dense_kernel, n_edge=n_edge, n_main=n_main)
    return pl.pallas_call(
        kern,
        grid=(n // tn, n_main + n_edge),
        in_specs=[
            pl.BlockSpec((tn, d), lambda i, j: (i, 0), pipeline_mode=pl.Buffered(1)),
            pl.BlockSpec((te, d), lambda i, j: (main_step(j), 0)),
            pl.BlockSpec((te, d), lambda i, j: (main_step(j), 0)),
            pl.BlockSpec((tn, te), lambda i, j: (i, main_step(j))),
            pl.BlockSpec((edge_rows, d), edge_chunk),
            pl.BlockSpec((1, d), lambda i, j: (0, 0)),
        ],
        out_specs=pl.BlockSpec((edge_rows, d), edge_chunk),
        out_shape=jax.ShapeDtypeStruct((n, d), F32),
        scratch_shapes=[pltpu.VMEM((tn, d), F32)],
        compiler_params=_params(("parallel", "arbitrary")), name="peer_dense",
    )(x, u, v, w, h, g_out.reshape(1, d))


def kernel(x, norm_mix, w_in, conv_w, conv_b, w_br_attn, w_br_conv, b_gate, rel_bias,
           w_out, norm_ffn, peer_w_q, peer_sub_keys, peer_u, peer_v, norm_final):
    batch, seq, d = x.shape
    n = batch * seq
    assert norm_mix.shape[0] == 1, "single-layer problem"
    h = x.reshape(n, d)

    hn = rmsnorm(h, norm_mix[0], BF16)
    proj = matmul(hn, w_in[0], BF16, name="in_proj")
    attn, (u_bf16, v_bf16) = moba_attention(
        proj, rel_bias, batch, seq, q_col=0, k_col=ATTN_WIDTH // HEAD_DIM, v_col=2 * ATTN_WIDTH // HEAD_DIM,
        side_tables=(peer_u[0], peer_v[0]))
    cw = 512
    c0 = 3 * ATTN_WIDTH // cw
    conv = short_conv(proj, conv_w[0].reshape(CONV_K, CONV_WIDTH), conv_b[0].reshape(1, CONV_WIDTH),
                      batch, seq, cb_col=c0, cc_col=c0 + CONV_WIDTH // cw, cu_col=c0 + 2 * CONV_WIDTH // cw)
    tn = 1024
    g0 = (3 * ATTN_WIDTH + 3 * CONV_WIDTH) // tn
    merged = branch_merge(attn, conv, w_br_attn[0], w_br_conv[0],
                          proj, b_gate[0], ga_col=g0, gc_col=g0 + d // tn, tn=tn)
    h = matmul(merged, w_out[0], F32, residual=h, tm=512, tn=1024, name="out_proj")

    q, hn2 = norm_matmul(h, norm_ffn[0], peer_w_q[0].astype(BF16), F32, name="peer_query")
    code, gate = peer_route(q, peer_sub_keys[0])
    w = peer_dense_weights(code, gate)
    out = peer_dense_final(hn2, u_bf16, v_bf16, w, h, norm_final)
    return out.reshape(batch, seq, d)
```

```python
import functools
import math

import numpy as np
import jax
import jax.numpy as jnp
from jax import lax
from jax.experimental import pallas as pl
from jax.experimental.pallas import tpu as pltpu

F32 = jnp.float32
BF16 = jnp.bfloat16

D_MODEL = 4096
ATTN_HEADS = 16
HEAD_DIM = 128
ATTN_WIDTH = ATTN_HEADS * HEAD_DIM
MOBA_BLOCK = 256
MOBA_TOPK = 3
REL_BUCKETS = 32
REL_MAX_DIST = 128
CONV_WIDTH = 2048
CONV_K = 3
PEER_HEADS = 8
PEER_N_KEYS = 128
PEER_HALF = 128
PEER_TOPK = 16
NORM_EPS = 1e-6

V7X_VMEM_BYTES = 64 * 1024 * 1024
VMEM_LIMIT = V7X_VMEM_BYTES - 8 * 1024 * 1024
LANES = 128
SUBLANES = 8
NT_DIMS = (((1,), (1,)), ((), ()))
LOG2_E = math.log2(math.e)


def _params(semantics):
    return pltpu.CompilerParams(dimension_semantics=semantics, vmem_limit_bytes=VMEM_LIMIT)


def _rmsnorm_kernel(x_ref, g_ref, o_ref):
    x = x_ref[...]
    ms = jnp.mean(x * x, axis=-1, keepdims=True)
    o_ref[...] = (x * lax.rsqrt(ms + NORM_EPS) * g_ref[...]).astype(o_ref.dtype)


def rmsnorm(x, g, out_dtype, rows=256):
    n, d = x.shape
    rows = min(rows, n)
    row_spec = pl.BlockSpec((rows, d), lambda i: (i, 0))
    return pl.pallas_call(
        _rmsnorm_kernel, grid=(n // rows,),
        in_specs=[row_spec, pl.BlockSpec((1, d), lambda i: (0, 0))], out_specs=row_spec,
        out_shape=jax.ShapeDtypeStruct((n, d), out_dtype),
        compiler_params=_params(("parallel",)), name="rmsnorm",
    )(x, g.reshape(1, d))


def _matmul_kernel(a_ref, b_ref, *rest, n_col_blocks, has_residual):
    r_ref = rest[0] if has_residual else None
    o_ref, b_scr = rest[-2], rest[-1]
    jj = pl.program_id(0)
    i = pl.program_id(1)
    kc = b_ref.shape[0]

    def round_chunk():
        rows = pl.ds(pl.multiple_of(i * kc, kc), kc)
        b_scr[jj % 2, rows, :] = b_ref[...].astype(BF16)

    def multiply():
        acc = jnp.dot(a_ref[...], b_scr[(jj + 1) % 2], preferred_element_type=F32)
        if has_residual:
            acc = r_ref[...] + acc
        o_ref[...] = acc.astype(o_ref.dtype)

    pl.when(jj == 0)(round_chunk)
    pl.when(jj == n_col_blocks)(multiply)

    @pl.when((jj > 0) & (jj < n_col_blocks))
    def _():
        multiply()
        round_chunk()


def matmul(a, b, out_dtype, residual=None, tm=1024, tn=1024, name="matmul"):
    m, k = a.shape
    _, n = b.shape
    tm, tn = min(tm, m), min(tn, n)
    ni, nj = m // tm, n // tn
    assert k % ni == 0
    kc = k // ni

    def out_index(jj, i):
        return (jnp.where(jj == 0, 0, i), jnp.maximum(jj - 1, 0))

    a_spec = pl.BlockSpec((tm, k), lambda jj, i: (jnp.where(jj == 0, 0, i), 0))
    b_spec = pl.BlockSpec((kc, tn), lambda jj, i: (jnp.where(jj < nj, i, ni - 1), jnp.minimum(jj, nj - 1)))
    o_spec = pl.BlockSpec((tm, tn), out_index)
    ins, specs = (a, b), [a_spec, b_spec]
    if residual is not None:
        ins, specs = ins + (residual,), specs + [o_spec]
    kern = functools.partial(_matmul_kernel, n_col_blocks=nj, has_residual=residual is not None)
    return pl.pallas_call(
        kern, grid=(nj + 1, ni), in_specs=specs, out_specs=o_spec,
        out_shape=jax.ShapeDtypeStruct((m, n), out_dtype),
        scratch_shapes=[pltpu.VMEM((2, k, tn), BF16)],
        compiler_params=_params(("arbitrary", "arbitrary")), name=name,
    )(*ins)


NORM_CHUNK_ROWS = 64


def _rmsnorm_chunks(load_rows, g_ref, dst_ref, unrolled=False):
    def body(c, carry):
        start = c * NORM_CHUNK_ROWS
        rows = pl.ds(start if unrolled else pl.multiple_of(start, NORM_CHUNK_ROWS), NORM_CHUNK_ROWS)
        x = load_rows(rows)
        ms = jnp.mean(x * x, axis=-1, keepdims=True)
        dst_ref[rows, :] = (x * lax.rsqrt(ms + NORM_EPS) * g_ref[...]).astype(dst_ref.dtype)
        return carry

    n_chunks = dst_ref.shape[0] // NORM_CHUNK_ROWS
    if unrolled:
        for c in range(n_chunks):
            body(c, 0)
    else:
        lax.fori_loop(0, n_chunks, body, 0)


def _norm_matmul_kernel(x_ref, g_ref, b_ref, o_ref, hn_ref, hn_scr, *, n_row_tiles):
    s = pl.program_id(0)

    def normalise():
        _rmsnorm_chunks(lambda rows: x_ref[rows, :], g_ref, hn_scr.at[s % 2], unrolled=True)
        hn_ref[...] = hn_scr[s % 2]

    def multiply():
        o_ref[...] = jnp.dot(hn_scr[(s + 1) % 2], b_ref[...], preferred_element_type=F32).astype(o_ref.dtype)

    @pl.when(s == 0)
    def _():
        normalise()

    @pl.when((s > 0) & (s < n_row_tiles))
    def _():
        multiply()
        normalise()

    @pl.when(s == n_row_tiles)
    def _():
        multiply()


def norm_matmul(x, g, b, out_dtype, tm=256, name="norm_matmul"):
    m, k = x.shape
    _, n = b.shape
    tm = min(tm, m)
    ni = m // tm
    kern = functools.partial(_norm_matmul_kernel, n_row_tiles=ni)
    row_tile = lambda s: (jnp.minimum(s, ni - 1), 0)
    return pl.pallas_call(
        kern, grid=(ni + 1,),
        in_specs=[pl.BlockSpec((tm, k), row_tile),
                  pl.BlockSpec((1, k), lambda s: (0, 0)),
                  pl.BlockSpec((k, n), lambda s: (0, 0), pipeline_mode=pl.Buffered(1))],
        out_specs=[pl.BlockSpec((tm, n), lambda s: (jnp.maximum(s - 1, 0), 0)),
                   pl.BlockSpec((tm, k), row_tile)],
        out_shape=[jax.ShapeDtypeStruct((m, n), out_dtype), jax.ShapeDtypeStruct((m, k), BF16)],
        scratch_shapes=[pltpu.VMEM((2, tm, k), BF16)],
        compiler_params=_params(("arbitrary",)), name=name,
    )(x, g.reshape(1, k), b)


def _t5_bucket_table(max_dist):
    dist = np.arange(max_dist, dtype=np.int32)
    max_exact = REL_BUCKETS // 2
    d32 = np.maximum(dist, 1).astype(np.float32)
    ratio = (np.log(d32 / np.float32(max_exact)) / np.float32(math.log(REL_MAX_DIST / max_exact))
             * np.float32(REL_BUCKETS - max_exact))
    large = max_exact + ratio.astype(np.int32)
    large = np.minimum(large, REL_BUCKETS - 1)
    return np.where(dist < max_exact, dist, large).astype(np.int32)


def _moba_bucket_tile():
    L = MOBA_BLOCK
    key = np.arange(2 * L)[:, None]
    qi = np.arange(L)[None, :]
    dist = np.maximum(qi - key + L, 0)
    return _t5_bucket_table(2 * L)[dist]


def _moba_kernel(tab_ref, q_ref, k_ref, v_ref, bucket_ref, *rest, n_blocks, n_side, steps_per_table):
    side_in, o_ref, side_out, bias_scr = rest[:n_side], rest[n_side], rest[n_side + 1:-1], rest[-1]
    L = MOBA_BLOCK
    h = pl.program_id(0)
    b = pl.program_id(1)

    step = h * pl.num_programs(1) + b
    for t in range(n_side):
        @pl.when((step >= t * steps_per_table) & (step < (t + 1) * steps_per_table))
        def _(t=t):
            def body(c, carry):
                rows = pl.ds(pl.multiple_of(c * NORM_CHUNK_ROWS, NORM_CHUNK_ROWS), NORM_CHUNK_ROWS)
                side_out[t][rows, :] = side_in[t][rows, :].astype(BF16)
                return carry
            lax.fori_loop(0, side_in[t].shape[0] // NORM_CHUNK_ROWS, body, 0)

    @pl.when(b == 0)
    def _():
        bk = bucket_ref[...]
        acc = jnp.zeros(bk.shape, F32)
        for kk in range(REL_BUCKETS):
            acc = jnp.where(bk == kk, tab_ref[kk, h], acc)
        bias_scr[...] = acc * LOG2_E

    neg_inf = jnp.float32(-jnp.inf)
    far = tab_ref[REL_BUCKETS - 1, h] * LOG2_E
    scale = HEAD_DIM ** -0.5 * LOG2_E
    k_means = [jnp.mean(k_ref[n * L:(n + 1) * L, :].astype(F32), axis=0, keepdims=True)
               for n in range(n_blocks)]
    pad = jnp.zeros((SUBLANES - n_blocks % SUBLANES, HEAD_DIM), F32) if n_blocks % SUBLANES else None
    k_mean = jnp.concatenate(k_means + ([pad] if pad is not None else []), axis=0)
    v_t = v_ref[...].astype(F32).T.astype(BF16)
    key_i = lax.broadcasted_iota(jnp.int32, (L, L), 0)
    qry_i = lax.broadcasted_iota(jnp.int32, (L, L), 1)

    for qb in range(n_blocks):
        q = q_ref[qb * L:(qb + 1) * L, :]
        sel = None
        if qb > MOBA_TOPK:
            gate = lax.dot_general(k_mean, q.astype(F32), NT_DIMS,
                                   precision=lax.Precision.HIGHEST, preferred_element_type=F32)
            g = [gate[m:m + 1, :] for m in range(qb)]
            sel = []
            for n in range(qb):
                rank = jnp.zeros((1, L), F32)
                for m in range(qb):
                    if m != n:
                        beats = (g[m] > g[n]) | (g[m] == g[n]) if m < n else (g[m] > g[n])
                        rank = rank + jnp.where(beats, 1.0, 0.0)
                sel.append(jnp.where(rank < MOBA_TOPK, 1.0, 0.0))
        logits = []
        for n in range(qb + 1):
            s = lax.dot_general(k_ref[n * L:(n + 1) * L, :], q, NT_DIMS, preferred_element_type=F32)
            if n == qb:
                logits.append(jnp.where(qry_i >= key_i, s * scale + bias_scr[L:2 * L, :], neg_inf))
            else:
                lg = s * scale + (bias_scr[0:L, :] if n == qb - 1 else far)
                logits.append(lg if sel is None else jnp.where(sel[n] > 0.5, lg, neg_inf))
        mx = logits[0].max(axis=0, keepdims=True)
        for lg in logits[1:]:
            mx = jnp.maximum(mx, lg.max(axis=0, keepdims=True))
        denom = jnp.zeros((1, L), F32)
        acc = jnp.zeros((HEAD_DIM, L), F32)
        for n, lg in enumerate(logits):
            p = jnp.exp2(lg - mx)
            denom = denom + p.sum(axis=0, keepdims=True)
            acc = acc + jnp.dot(v_t[:, n * L:(n + 1) * L], p.astype(BF16), preferred_element_type=F32)
        o_ref[qb * L:(qb + 1) * L, :] = (acc / denom).T.astype(o_ref.dtype)


def moba_attention(proj, rel_bias, batch, seq, q_col, k_col, v_col, side_tables=()):
    L = MOBA_BLOCK
    n_blocks = seq // L
    bucket = jnp.asarray(_moba_bucket_tile())
    n_side = len(side_tables)
    n_steps = ATTN_HEADS * batch
    steps_per_table = n_steps // max(n_side, 1)
    side_specs, side_shapes = [], []
    for t, tab in enumerate(side_tables):
        rows, cols = tab.shape
        assert rows % steps_per_table == 0
        def index_map(h, b, t=t):
            return (jnp.clip(h * batch + b - t * steps_per_table, 0, steps_per_table - 1), 0)
        side_specs.append(pl.BlockSpec((rows // steps_per_table, cols), index_map))
        side_shapes.append(jax.ShapeDtypeStruct(tab.shape, BF16))
    kern = functools.partial(_moba_kernel, n_blocks=n_blocks, n_side=n_side, steps_per_table=steps_per_table)
    outs = pl.pallas_call(
        kern,
        grid=(ATTN_HEADS, batch),
        in_specs=[
            pl.BlockSpec(memory_space=pltpu.SMEM),
            pl.BlockSpec((seq, HEAD_DIM), lambda h, b: (b, q_col + h)),
            pl.BlockSpec((seq, HEAD_DIM), lambda h, b: (b, k_col + h)),
            pl.BlockSpec((seq, HEAD_DIM), lambda h, b: (b, v_col + h)),
            pl.BlockSpec((2 * L, L), lambda h, b: (0, 0)),
        ] + side_specs,
        out_specs=[pl.BlockSpec((seq, HEAD_DIM), lambda h, b: (b, h))] + side_specs,
        out_shape=[jax.ShapeDtypeStruct((batch * seq, ATTN_WIDTH), BF16)] + side_shapes,
        scratch_shapes=[pltpu.VMEM((2 * L, L), F32)],
        compiler_params=_params(("arbitrary", "arbitrary")), name="moba",
    )(rel_bias, proj, proj, proj, bucket, *side_tables)
    return outs[0], tuple(outs[1:])


def _conv_kernel(cb_ref, cc_ref, cu_ref, w_ref, b_ref, o_ref):
    g = cc_ref[...].astype(F32) * cu_ref[...].astype(F32)
    row = lax.broadcasted_iota(jnp.int32, g.shape, 0)
    y = w_ref[CONV_K - 1:CONV_K, :] * g
    for s in range(1, CONV_K):
        shifted = jnp.where(row >= s, pltpu.roll(g, s, axis=0), 0.0)
        y = y + w_ref[CONV_K - 1 - s:CONV_K - s, :] * shifted
    o_ref[...] = (cb_ref[...].astype(F32) * (y + b_ref[...])).astype(o_ref.dtype)


def short_conv(proj, conv_w, conv_b, batch, seq, cb_col, cc_col, cu_col, cw=512):
    width = conv_w.shape[1]
    return pl.pallas_call(
        _conv_kernel,
        grid=(batch, width // cw),
        in_specs=[
            pl.BlockSpec((seq, cw), lambda b, c: (b, cb_col + c)),
            pl.BlockSpec((seq, cw), lambda b, c: (b, cc_col + c)),
            pl.BlockSpec((seq, cw), lambda b, c: (b, cu_col + c)),
            pl.BlockSpec((CONV_K, cw), lambda b, c: (0, c)),
            pl.BlockSpec((1, cw), lambda b, c: (0, c)),
        ],
        out_specs=pl.BlockSpec((seq, cw), lambda b, c: (b, c)),
        out_shape=jax.ShapeDtypeStruct((batch * seq, width), BF16),
        compiler_params=_params(("parallel", "parallel")), name="short_conv",
    )(proj, proj, proj, conv_w, conv_b)


def _merge_kernel(attn_ref, conv_ref, wa_ref, wc_ref, ga_ref, gc_ref, bg_ref, o_ref, wa_scr, wc_scr,
                  *, n_col_blocks):
    jj = pl.program_id(0)
    i = pl.program_id(1)

    def round_chunks():
        for w_ref, w_scr in ((wa_ref, wa_scr), (wc_ref, wc_scr)):
            kc = w_ref.shape[0]
            w_scr[jj % 2, pl.ds(pl.multiple_of(i * kc, kc), kc), :] = w_ref[...].astype(BF16)

    def multiply():
        za = jnp.dot(attn_ref[...], wa_scr[(jj + 1) % 2], preferred_element_type=F32)
        zc = jnp.dot(conv_ref[...], wc_scr[(jj + 1) % 2], preferred_element_type=F32)
        ga = jax.nn.sigmoid(ga_ref[...].astype(F32) + bg_ref[0:1, :])
        gc = jax.nn.sigmoid(gc_ref[...].astype(F32) + bg_ref[1:2, :])
        o_ref[...] = (ga * za + gc * zc).astype(o_ref.dtype)

    pl.when(jj == 0)(round_chunks)
    pl.when(jj == n_col_blocks)(multiply)

    @pl.when((jj > 0) & (jj < n_col_blocks))
    def _():
        multiply()
        round_chunks()


def branch_merge(attn, conv, wa, wc, proj, b_gate, ga_col, gc_col, tm=512, tn=1024):
    m, ka = attn.shape
    kc = conv.shape[1]
    n = wa.shape[1]
    tm, tn = min(tm, m), min(tn, n)
    ni, nj = m // tm, n // tn
    assert ka % ni == 0 and kc % ni == 0

    def rows(jj, i):
        return jnp.where(jj == 0, 0, i)

    def col(jj):
        return jnp.maximum(jj - 1, 0)

    def weight_index(jj, i):
        return (jnp.where(jj < nj, i, ni - 1), jnp.minimum(jj, nj - 1))

    kern = functools.partial(_merge_kernel, n_col_blocks=nj)
    return pl.pallas_call(
        kern,
        grid=(nj + 1, ni),
        in_specs=[
            pl.BlockSpec((tm, ka), lambda jj, i: (rows(jj, i), 0)),
            pl.BlockSpec((tm, kc), lambda jj, i: (rows(jj, i), 0)),
            pl.BlockSpec((ka // ni, tn), weight_index),
            pl.BlockSpec((kc // ni, tn), weight_index),
            pl.BlockSpec((tm, tn), lambda jj, i: (rows(jj, i), ga_col + col(jj))),
            pl.BlockSpec((tm, tn), lambda jj, i: (rows(jj, i), gc_col + col(jj))),
            pl.BlockSpec((2, tn), lambda jj, i: (0, col(jj))),
        ],
        out_specs=pl.BlockSpec((tm, tn), lambda jj, i: (rows(jj, i), col(jj))),
        out_shape=jax.ShapeDtypeStruct((m, n), BF16),
        scratch_shapes=[pltpu.VMEM((2, ka, tn), BF16), pltpu.VMEM((2, kc, tn), BF16)],
        compiler_params=_params(("arbitrary", "arbitrary")), name="branch_merge",
    )(attn, conv, wa, wc, proj, proj, b_gate)


def _topk_rows(s, k, vals_ref, rows_ref):
    nrows = s.shape[0]
    rowi = lax.broadcasted_iota(jnp.int32, s.shape, 0).astype(F32)
    for r in range(k):
        m = s.max(axis=0, keepdims=True)
        idx = jnp.where(s == m, rowi, float(nrows)).min(axis=0, keepdims=True)
        s = jnp.where(rowi == idx, -jnp.inf, s)
        vals_ref[r:r + 1, :] = m
        rows_ref[r:r + 1, :] = idx


def _topk_rows_untied(s, k, vals_ref, rows_ref):
    nrows, ncols = s.shape
    probe_rows = 2 * SUBLANES
    lane = lax.broadcasted_iota(jnp.int32, (probe_rows, nrows), 1).astype(F32)
    sub = lax.broadcasted_iota(jnp.int32, (probe_rows, nrows), 0)
    probe = jnp.where(sub == 0, lane, jnp.where(sub == 1, 1.0, 0.0)).astype(BF16)
    worst = jnp.zeros((1, ncols), F32)
    for r in range(k):
        m = s.max(axis=0, keepdims=True)
        hit = s == m
        s = jnp.where(hit, -jnp.inf, s)
        stats = jnp.dot(probe, jnp.where(hit, 1.0, 0.0).astype(BF16), preferred_element_type=F32)
        vals_ref[r:r + 1, :] = m
        rows_ref[r:r + 1, :] = stats[0:1]
        worst = jnp.maximum(worst, stats[1:2])
    return worst


def _pair_candidates(v1, v2):
    K = PEER_TOPK
    ncols = v1.shape[1]
    sub8 = lax.broadcasted_iota(jnp.int32, (SUBLANES, ncols), 0)
    sub16 = lax.broadcasted_iota(jnp.int32, (K, ncols), 0)
    big = float(K * K)
    vals = [v1[0:1] + v2]
    flat = [sub16.astype(F32)]
    for a in range(1, SUBLANES):
        ok = sub8 < K // (a + 1)
        vals.append(jnp.where(ok, v1[a:a + 1] + v2[0:SUBLANES], -jnp.inf))
        flat.append(jnp.where(ok, (sub8 + a * K).astype(F32), big))
    vals.append(v1[SUBLANES:K] + v2[0:1])
    flat.append(((sub8 + SUBLANES) * K).astype(F32))
    return jnp.concatenate(vals, axis=0), jnp.concatenate(flat, axis=0)


def _select_rows(table, which):
    out = jnp.zeros(which.shape, F32)
    for a in range(table.shape[0]):
        out = jnp.where(which == float(a), table[a:a + 1], out)
    return out


def _peer_route_kernel(q_ref, keys_ref, code_ref, gate_ref, s1_scr, s2_scr, code_scr, gate_scr, topk_scr,
                       *, chunk_tokens):
    K = PEER_TOPK
    h = pl.program_id(1)
    q = q_ref[...]
    s1_scr[...] = lax.dot_general(keys_ref[0, 0], q[:, :PEER_HALF], NT_DIMS,
                                  precision=lax.Precision.HIGHEST, preferred_element_type=F32)
    s2_scr[...] = lax.dot_general(keys_ref[0, 1], q[:, PEER_HALF:], NT_DIMS,
                                  precision=lax.Precision.HIGHEST, preferred_element_type=F32)
    row0 = pl.multiple_of(h * K, K)

    def chunk(c, carry):
        off = pl.multiple_of(c * chunk_tokens, chunk_tokens)
        cols = pl.ds(off, chunk_tokens)
        t1 = _topk_rows_untied(s1_scr[:, cols], K, topk_scr.at[0], topk_scr.at[1])
        t2 = _topk_rows_untied(s2_scr[:, cols], K, topk_scr.at[2], topk_scr.at[3])

        @pl.when(jnp.max(jnp.maximum(t1, t2)) > 1.5)
        def _():
            _topk_rows(s1_scr[:, cols], K, topk_scr.at[0], topk_scr.at[1])
            _topk_rows(s2_scr[:, cols], K, topk_scr.at[2], topk_scr.at[3])

        cand, flat = _pair_candidates(topk_scr[0], topk_scr[2])
        for r in range(K):
            m = cand.max(axis=0, keepdims=True)
            first = jnp.where(cand == m, flat, float(K * K)).min(axis=0, keepdims=True)
            cand = jnp.where(flat == first, -jnp.inf, cand)
            topk_scr[4, r:r + 1, :] = m
            topk_scr[5, r:r + 1, :] = first
        score = topk_scr[4]
        e = jnp.exp(score - score[0:1])
        pair = topk_scr[5]
        a = jnp.floor(pair * (1.0 / K))
        e1 = _select_rows(topk_scr[1], a)
        e2 = _select_rows(topk_scr[3], pair - a * K)
        code_scr[pl.ds(row0, K), pl.ds(off, chunk_tokens)] = e1 * float(PEER_N_KEYS) + e2
        gate_scr[pl.ds(row0, K), pl.ds(off, chunk_tokens)] = e / e.sum(axis=0, keepdims=True)
        return carry

    lax.fori_loop(0, q.shape[0] // chunk_tokens, chunk, 0)

    @pl.when(h == PEER_HEADS - 1)
    def _():
        code_ref[...] = code_scr[...].T.astype(jnp.int32)
        gate_ref[...] = gate_scr[...].T


def peer_route(q, sub_keys, tt=1024, chunk_tokens=1024):
    n = q.shape[0]
    tt = min(tt, n)
    chunk_tokens = min(chunk_tokens, tt)
    qd = 2 * PEER_HALF
    picks = PEER_HEADS * PEER_TOPK
    out_spec = pl.BlockSpec((tt, picks), lambda i, h: (i, 0))
    return pl.pallas_call(
        functools.partial(_peer_route_kernel, chunk_tokens=chunk_tokens),
        grid=(n // tt, PEER_HEADS),
        in_specs=[
            pl.BlockSpec((tt, qd), lambda i, h: (i, h)),
            pl.BlockSpec((1, 2, PEER_N_KEYS, PEER_HALF), lambda i, h: (h, 0, 0, 0)),
        ],
        out_specs=[out_spec, out_spec],
        out_shape=[jax.ShapeDtypeStruct((n, picks), jnp.int32),
                   jax.ShapeDtypeStruct((n, picks), F32)],
        scratch_shapes=[pltpu.VMEM((PEER_N_KEYS, tt), F32), pltpu.VMEM((PEER_N_KEYS, tt), F32),
                        pltpu.VMEM((picks, tt), F32), pltpu.VMEM((picks, tt), F32),
                        pltpu.VMEM((6, PEER_TOPK, chunk_tokens), F32)],
        compiler_params=_params(("parallel", "arbitrary")), name="peer_route",
    )(q, sub_keys)


W_ROW_PITCH = PEER_N_KEYS + SUBLANES


def _peer_weights_kernel(code_ref, gate_ref, o_ref, w_scr, *, unroll):
    nk = PEER_N_KEYS
    tt, picks = code_ref.shape
    sub = lax.broadcasted_iota(jnp.int32, (nk, picks), 0)
    zero = jnp.zeros((nk, picks), BF16)

    def one_hots(t):
        c = code_ref[pl.ds(t, 1), :]
        g = gate_ref[pl.ds(t, 1), :]
        e1 = lax.shift_right_logical(c, 7)
        e2 = lax.bitwise_and(c, nk - 1)
        at = jnp.where(sub == e1, g, 0.0).astype(BF16)
        bt = jnp.where(sub == e2, 1.0, 0.0).astype(BF16)
        return at, bt

    def body(p, carry):
        t = 2 * p
        at0, bt0 = one_hots(t)
        at1, bt1 = one_hots(t + 1)
        at = jnp.concatenate([at0, at1], axis=1)
        bt = jnp.concatenate([jnp.concatenate([bt0, zero], axis=1),
                              jnp.concatenate([zero, bt1], axis=1)], axis=0)
        w = lax.dot_general(at, bt, NT_DIMS, preferred_element_type=F32)
        row = pl.multiple_of(t * W_ROW_PITCH, SUBLANES)
        w_scr[pl.ds(row, nk), :] = w[:, :nk]
        w_scr[pl.ds(row + W_ROW_PITCH, nk), :] = w[:, nk:]
        return carry

    lax.fori_loop(0, tt // 2, body, 0, unroll=unroll)
    for e1 in range(nk):
        o_ref[:, e1 * nk:(e1 + 1) * nk] = w_scr[pl.ds(e1, tt, stride=W_ROW_PITCH), :].astype(o_ref.dtype)


def peer_dense_weights(code, gate, tt=128, unroll=64):
    n, picks = code.shape
    tt = min(tt, n)
    nk = PEER_N_KEYS
    kern = functools.partial(_peer_weights_kernel, unroll=unroll)
    return pl.pallas_call(
        kern,
        grid=(n // tt,),
        in_specs=[pl.BlockSpec((tt, picks), lambda i: (i, 0)),
                  pl.BlockSpec((tt, picks), lambda i: (i, 0))],
        out_specs=pl.BlockSpec((tt, nk * nk), lambda i: (i, 0)),
        out_shape=jax.ShapeDtypeStruct((n, nk * nk), BF16),
        scratch_shapes=[pltpu.VMEM((tt * W_ROW_PITCH, nk), F32)],
        compiler_params=_params(("parallel",)), name="peer_weights",
    )(code, gate)


def _row_rms_scale(t, g):
    return t * lax.rsqrt(jnp.mean(t * t, axis=-1, keepdims=True) + NORM_EPS) * g


def _peer_dense_kernel(x_ref, u_ref, v_ref, w_ref, h_ref, gout_ref, o_ref, acc_scr, *, n_edge, n_main):
    j = pl.program_id(1)
    edge_rows = h_ref.shape[0]

    @pl.when(j == 0)
    def _():
        acc_scr[...] = jnp.zeros(acc_scr.shape, F32)

    @pl.when(j < n_main)
    def _():
        a = lax.dot_general(x_ref[...], u_ref[...], NT_DIMS, preferred_element_type=F32)
        act = 0.5 * a * (1.0 + lax.erf(a * math.sqrt(0.5))) * w_ref[...].astype(F32)
        acc_scr[...] += jnp.dot(act.astype(BF16), v_ref[...], preferred_element_type=F32)

    @pl.when(j >= n_main)
    def _():
        r = j - n_main
        rows = pl.ds(pl.multiple_of(r * edge_rows, edge_rows), edge_rows)
        o_ref[...] = _row_rms_scale(h_ref[...] + acc_scr[rows, :], gout_ref[...])


def peer_dense_final(x, u, v, w, h, g_out, tn=1024, te=512, edge_rows=128):
    n, d = h.shape
    ne = u.shape[0]
    tn, te = min(tn, n), min(te, ne)
    edge_rows = min(edge_rows, tn)
    n_edge, n_main = tn // edge_rows, ne // te

    def main_step(j):
        return jnp.minimum(j, n_main - 1)

    def edge_chunk(i, j):
        return (i * n_edge + jnp.maximum(j - n_main, 0), 0)

    kern = functools.partial(_peer_dense_kernel, n_edge=n_edge, n_main=n_main)
    return pl.pallas_call(
        kern,
        grid=(n // tn, n_main + n_edge),
        in_specs=[
            pl.BlockSpec((tn, d), lambda i, j: (i, 0), pipeline_mode=pl.Buffered(1)),
            pl.BlockSpec((te, d), lambda i, j: (main_step(j), 0)),
            pl.BlockSpec((te, d), lambda i, j: (main_step(j), 0)),
            pl.BlockSpec((tn, te), lambda i, j: (i, main_step(j))),
            pl.BlockSpec((edge_rows, d), edge_chunk),
            pl.BlockSpec((1, d), lambda i, j: (0, 0)),
        ],
        out_specs=pl.BlockSpec((edge_rows, d), edge_chunk),
        out_shape=jax.ShapeDtypeStruct((n, d), F32),
        scratch_shapes=[pltpu.VMEM((tn, d), F32)],
        compiler_params=_params(("parallel", "arbitrary")), name="peer_dense",
    )(x, u, v, w, h, g_out.reshape(1, d))


def kernel(x, norm_mix, w_in, conv_w, conv_b, w_br_attn, w_br_conv, b_gate, rel_bias,
           w_out, norm_ffn, peer_w_q, peer_sub_keys, peer_u, peer_v, norm_final):
    batch, seq, d = x.shape
    n = batch * seq
    assert norm_mix.shape[0] == 1, "single-layer problem"
    h = x.reshape(n, d)

    hn = rmsnorm(h, norm_mix[0], BF16)
    proj = matmul(hn, w_in[0], BF16, name="in_proj")
    attn, (u_bf16, v_bf16) = moba_attention(
        proj, rel_bias, batch, seq, q_col=0, k_col=ATTN_WIDTH // HEAD_DIM, v_col=2 * ATTN_WIDTH // HEAD_DIM,
        side_tables=(peer_u[0], peer_v[0]))
    cw = 512
    c0 = 3 * ATTN_WIDTH // cw
    conv = short_conv(proj, conv_w[0].reshape(CONV_K, CONV_WIDTH), conv_b[0].reshape(1, CONV_WIDTH),
                      batch, seq, cb_col=c0, cc_col=c0 + CONV_WIDTH // cw, cu_col=c0 + 2 * CONV_WIDTH // cw)
    tn = 1024
    g0 = (3 * ATTN_WIDTH + 3 * CONV_WIDTH) // tn
    merged = branch_merge(attn, conv, w_br_attn[0], w_br_conv[0],
                          proj, b_gate[0], ga_col=g0, gc_col=g0 + d // tn, tn=tn)
    h = matmul(merged, w_out[0], F32, residual=h, tm=512, tn=1024, name="out_proj")

    q, hn2 = norm_matmul(h, norm_ffn[0], peer_w_q[0].astype(BF16), F32, name="peer_query")
    code, gate = peer_route(q, peer_sub_keys[0])
    w = peer_dense_weights(code, gate)
    out = peer_dense_final(hn2, u_bf16, v_bf16, w, h, norm_final)
    return out.reshape(batch, seq, d)
```

```python
import functools
import math

import numpy as np
import jax
import jax.numpy as jnp
from jax import lax
from jax.experimental import pallas as pl
from jax.experimental.pallas import tpu as pltpu

F32 = jnp.float32
BF16 = jnp.bfloat16

D_MODEL = 4096
ATTN_HEADS = 16
HEAD_DIM = 128
ATTN_WIDTH = ATTN_HEADS * HEAD_DIM
MOBA_BLOCK = 256
MOBA_TOPK = 3
REL_BUCKETS = 32
REL_MAX_DIST = 128
CONV_WIDTH = 2048
CONV_K = 3
PEER_HEADS = 8
PEER_N_KEYS = 128
PEER_HALF = 128
PEER_TOPK = 16
NORM_EPS = 1e-6

V7X_VMEM_BYTES = 64 * 1024 * 1024
VMEM_LIMIT = V7X_VMEM_BYTES - 2 * 1024 * 1024
LANES = 128
SUBLANES = 8
NT_DIMS = (((1,), (1,)), ((), ()))
LOG2_E = math.log2(math.e)


def _params(semantics):
    return pltpu.CompilerParams(dimension_semantics=semantics, vmem_limit_bytes=VMEM_LIMIT)


def _rmsnorm_kernel(x_ref, g_ref, o_ref):
    x = x_ref[...]
    ms = jnp.mean(x * x, axis=-1, keepdims=True)
    o_ref[...] = (x * lax.rsqrt(ms + NORM_EPS) * g_ref[...]).astype(o_ref.dtype)


def rmsnorm(x, g, out_dtype, rows=256):
    n, d = x.shape
    rows = min(rows, n)
    row_spec = pl.BlockSpec((rows, d), lambda i: (i, 0))
    return pl.pallas_call(
        _rmsnorm_kernel, grid=(n // rows,),
        in_specs=[row_spec, pl.BlockSpec((1, d), lambda i: (0, 0))], out_specs=row_spec,
        out_shape=jax.ShapeDtypeStruct((n, d), out_dtype),
        compiler_params=_params(("parallel",)), name="rmsnorm",
    )(x, g.reshape(1, d))


def _matmul_kernel(a_ref, b_ref, *rest, n_col_blocks, has_residual):
    r_ref = rest[0] if has_residual else None
    o_ref, b_scr = rest[-2], rest[-1]
    jj = pl.program_id(0)
    i = pl.program_id(1)
    kc = b_ref.shape[0]

    def round_chunk():
        rows = pl.ds(pl.multiple_of(i * kc, kc), kc)
        b_scr[jj % 2, rows, :] = b_ref[...].astype(BF16)

    def multiply():
        acc = jnp.dot(a_ref[...], b_scr[(jj + 1) % 2], preferred_element_type=F32)
        if has_residual:
            acc = r_ref[...] + acc
        o_ref[...] = acc.astype(o_ref.dtype)

    pl.when(jj == 0)(round_chunk)
    pl.when(jj == n_col_blocks)(multiply)

    @pl.when((jj > 0) & (jj < n_col_blocks))
    def _():
        multiply()
        round_chunk()


def matmul(a, b, out_dtype, residual=None, tm=1024, tn=1024, name="matmul"):
    m, k = a.shape
    _, n = b.shape
    tm, tn = min(tm, m), min(tn, n)
    ni, nj = m // tm, n // tn
    assert k % ni == 0
    kc = k // ni

    def out_index(jj, i):
        return (jnp.where(jj == 0, 0, i), jnp.maximum(jj - 1, 0))

    a_spec = pl.BlockSpec((tm, k), lambda jj, i: (jnp.where(jj == 0, 0, i), 0))
    b_spec = pl.BlockSpec((kc, tn), lambda jj, i: (jnp.where(jj < nj, i, ni - 1), jnp.minimum(jj, nj - 1)))
    o_spec = pl.BlockSpec((tm, tn), out_index)
    ins, specs = (a, b), [a_spec, b_spec]
    if residual is not None:
        ins, specs = ins + (residual,), specs + [o_spec]
    kern = functools.partial(_matmul_kernel, n_col_blocks=nj, has_residual=residual is not None)
    return pl.pallas_call(
        kern, grid=(nj + 1, ni), in_specs=specs, out_specs=o_spec,
        out_shape=jax.ShapeDtypeStruct((m, n), out_dtype),
        scratch_shapes=[pltpu.VMEM((2, k, tn), BF16)],
        compiler_params=_params(("arbitrary", "arbitrary")), name=name,
    )(*ins)


NORM_CHUNK_ROWS = 64


def _rmsnorm_chunks(load_rows, g_ref, dst_ref, unrolled=False):
    def body(c, carry):
        start = c * NORM_CHUNK_ROWS
        rows = pl.ds(start if unrolled else pl.multiple_of(start, NORM_CHUNK_ROWS), NORM_CHUNK_ROWS)
        x = load_rows(rows)
        ms = jnp.mean(x * x, axis=-1, keepdims=True)
        dst_ref[rows, :] = (x * lax.rsqrt(ms + NORM_EPS) * g_ref[...]).astype(dst_ref.dtype)
        return carry

    n_chunks = dst_ref.shape[0] // NORM_CHUNK_ROWS
    if unrolled:
        for c in range(n_chunks):
            body(c, 0)
    else:
        lax.fori_loop(0, n_chunks, body, 0)


def _norm_matmul_kernel(x_ref, g_ref, b_ref, o_ref, hn_ref, hn_scr, *, n_row_tiles):
    s = pl.program_id(0)

    def normalise():
        _rmsnorm_chunks(lambda rows: x_ref[rows, :], g_ref, hn_scr.at[s % 2], unrolled=True)
        hn_ref[...] = hn_scr[s % 2]

    def multiply():
        o_ref[...] = jnp.dot(hn_scr[(s + 1) % 2], b_ref[...], preferred_element_type=F32).astype(o_ref.dtype)

    @pl.when(s == 0)
    def _():
        normalise()

    @pl.when((s > 0) & (s < n_row_tiles))
    def _():
        multiply()
        normalise()

    @pl.when(s == n_row_tiles)
    def _():
        multiply()


def norm_matmul(x, g, b, out_dtype, tm=256, name="norm_matmul"):
    m, k = x.shape
    _, n = b.shape
    tm = min(tm, m)
    ni = m // tm
    kern = functools.partial(_norm_matmul_kernel, n_row_tiles=ni)
    row_tile = lambda s: (jnp.minimum(s, ni - 1), 0)
    return pl.pallas_call(
        kern, grid=(ni + 1,),
        in_specs=[pl.BlockSpec((tm, k), row_tile),
                  pl.BlockSpec((1, k), lambda s: (0, 0)),
                  pl.BlockSpec((k, n), lambda s: (0, 0), pipeline_mode=pl.Buffered(1))],
        out_specs=[pl.BlockSpec((tm, n), lambda s: (jnp.maximum(s - 1, 0), 0)),
                   pl.BlockSpec((tm, k), row_tile)],
        out_shape=[jax.ShapeDtypeStruct((m, n), out_dtype), jax.ShapeDtypeStruct((m, k), BF16)],
        scratch_shapes=[pltpu.VMEM((2, tm, k), BF16)],
        compiler_params=_params(("arbitrary",)), name=name,
    )(x, g.reshape(1, k), b)


def _t5_bucket_table(max_dist):
    dist = np.arange(max_dist, dtype=np.int32)
    max_exact = REL_BUCKETS // 2
    d32 = np.maximum(dist, 1).astype(np.float32)
    ratio = (np.log(d32 / np.float32(max_exact)) / np.float32(math.log(REL_MAX_DIST / max_exact))
             * np.float32(REL_BUCKETS - max_exact))
    large = max_exact + ratio.astype(np.int32)
    large = np.minimum(large, REL_BUCKETS - 1)
    return np.where(dist < max_exact, dist, large).astype(np.int32)


def _moba_bucket_tile():
    L = MOBA_BLOCK
    key = np.arange(2 * L)[:, None]
    qi = np.arange(L)[None, :]
    dist = np.maximum(qi - key + L, 0)
    return _t5_bucket_table(2 * L)[dist]


def _moba_kernel(tab_ref, q_ref, k_ref, v_ref, bucket_ref, *rest, n_blocks, n_side, steps_per_table):
    side_in, o_ref, side_out, bias_scr = rest[:n_side], rest[n_side], rest[n_side + 1:-1], rest[-1]
    L = MOBA_BLOCK
    h = pl.program_id(0)
    b = pl.program_id(1)

    step = h * pl.num_programs(1) + b
    for t in range(n_side):
        @pl.when((step >= t * steps_per_table) & (step < (t + 1) * steps_per_table))
        def _(t=t):
            def body(c, carry):
                rows = pl.ds(pl.multiple_of(c * NORM_CHUNK_ROWS, NORM_CHUNK_ROWS), NORM_CHUNK_ROWS)
                side_out[t][rows, :] = side_in[t][rows, :].astype(BF16)
                return carry
            lax.fori_loop(0, side_in[t].shape[0] // NORM_CHUNK_ROWS, body, 0)

    @pl.when(b == 0)
    def _():
        bk = bucket_ref[...]
        acc = jnp.zeros(bk.shape, F32)
        for kk in range(REL_BUCKETS):
            acc = jnp.where(bk == kk, tab_ref[kk, h], acc)
        bias_scr[...] = acc * LOG2_E

    neg_inf = jnp.float32(-jnp.inf)
    far = tab_ref[REL_BUCKETS - 1, h] * LOG2_E
    scale = HEAD_DIM ** -0.5 * LOG2_E
    k_means = [jnp.mean(k_ref[n * L:(n + 1) * L, :].astype(F32), axis=0, keepdims=True)
               for n in range(n_blocks)]
    pad = jnp.zeros((SUBLANES - n_blocks % SUBLANES, HEAD_DIM), F32) if n_blocks % SUBLANES else None
    k_mean = jnp.concatenate(k_means + ([pad] if pad is not None else []), axis=0)
    v_t = v_ref[...].astype(F32).T.astype(BF16)
    key_i = lax.broadcasted_iota(jnp.int32, (L, L), 0)
    qry_i = lax.broadcasted_iota(jnp.int32, (L, L), 1)

    for qb in range(n_blocks):
        q = q_ref[qb * L:(qb + 1) * L, :]
        sel = None
        if qb > MOBA_TOPK:
            gate = lax.dot_general(k_mean, q.astype(F32), NT_DIMS,
                                   precision=lax.Precision.HIGHEST, preferred_element_type=F32)
            g = [gate[m:m + 1, :] for m in range(qb)]
            sel = []
            for n in range(qb):
                rank = jnp.zeros((1, L), F32)
                for m in range(qb):
                    if m != n:
                        beats = (g[m] > g[n]) | (g[m] == g[n]) if m < n else (g[m] > g[n])
                        rank = rank + jnp.where(beats, 1.0, 0.0)
                sel.append(jnp.where(rank < MOBA_TOPK, 1.0, 0.0))
        logits = []
        for n in range(qb + 1):
            s = lax.dot_general(k_ref[n * L:(n + 1) * L, :], q, NT_DIMS, preferred_element_type=F32)
            if n == qb:
                logits.append(jnp.where(qry_i >= key_i, s * scale + bias_scr[L:2 * L, :], neg_inf))
            else:
                lg = s * scale + (bias_scr[0:L, :] if n == qb - 1 else far)
                logits.append(lg if sel is None else jnp.where(sel[n] > 0.5, lg, neg_inf))
        mx = logits[0].max(axis=0, keepdims=True)
        for lg in logits[1:]:
            mx = jnp.maximum(mx, lg.max(axis=0, keepdims=True))
        denom = jnp.zeros((1, L), F32)
        acc = jnp.zeros((HEAD_DIM, L), F32)
        for n, lg in enumerate(logits):
            p = jnp.exp2(lg - mx)
            denom = denom + p.sum(axis=0, keepdims=True)
            acc = acc + jnp.dot(v_t[:, n * L:(n + 1) * L], p.astype(BF16), preferred_element_type=F32)
        o_ref[qb * L:(qb + 1) * L, :] = (acc / denom).T.astype(o_ref.dtype)


def moba_attention(proj, rel_bias, batch, seq, q_col, k_col, v_col, side_tables=()):
    L = MOBA_BLOCK
    n_blocks = seq // L
    bucket = jnp.asarray(_moba_bucket_tile())
    n_side = len(side_tables)
    n_steps = ATTN_HEADS * batch
    steps_per_table = n_steps // max(n_side, 1)
    side_specs, side_shapes = [], []
    for t, tab in enumerate(side_tables):
        rows, cols = tab.shape
        assert rows % steps_per_table == 0
        def index_map(h, b, t=t):
            return (jnp.clip(h * batch + b - t * steps_per_table, 0, steps_per_table - 1), 0)
        side_specs.append(pl.BlockSpec((rows // steps_per_table, cols), index_map))
        side_shapes.append(jax.ShapeDtypeStruct(tab.shape, BF16))
    kern = functools.partial(_moba_kernel, n_blocks=n_blocks, n_side=n_side, steps_per_table=steps_per_table)
    outs = pl.pallas_call(
        kern,
        grid=(ATTN_HEADS, batch),
        in_specs=[
            pl.BlockSpec(memory_space=pltpu.SMEM),
            pl.BlockSpec((seq, HEAD_DIM), lambda h, b: (b, q_col + h)),
            pl.BlockSpec((seq, HEAD_DIM), lambda h, b: (b, k_col + h)),
            pl.BlockSpec((seq, HEAD_DIM), lambda h, b: (b, v_col + h)),
            pl.BlockSpec((2 * L, L), lambda h, b: (0, 0)),
        ] + side_specs,
        out_specs=[pl.BlockSpec((seq, HEAD_DIM), lambda h, b: (b, h))] + side_specs,
        out_shape=[jax.ShapeDtypeStruct((batch * seq, ATTN_WIDTH), BF16)] + side_shapes,
        scratch_shapes=[pltpu.VMEM((2 * L, L), F32)],
        compiler_params=_params(("arbitrary", "arbitrary")), name="moba",
    )(rel_bias, proj, proj, proj, bucket, *side_tables)
    return outs[0], tuple(outs[1:])


def _conv_kernel(cb_ref, cc_ref, cu_ref, w_ref, b_ref, o_ref):
    g = cc_ref[...].astype(F32) * cu_ref[...].astype(F32)
    row = lax.broadcasted_iota(jnp.int32, g.shape, 0)
    y = w_ref[CONV_K - 1:CONV_K, :] * g
    for s in range(1, CONV_K):
        shifted = jnp.where(row >= s, pltpu.roll(g, s, axis=0), 0.0)
        y = y + w_ref[CONV_K - 1 - s:CONV_K - s, :] * shifted
    o_ref[...] = (cb_ref[...].astype(F32) * (y + b_ref[...])).astype(o_ref.dtype)


def short_conv(proj, conv_w, conv_b, batch, seq, cb_col, cc_col, cu_col, cw=512):
    width = conv_w.shape[1]
    return pl.pallas_call(
        _conv_kernel,
        grid=(batch, width // cw),
        in_specs=[
            pl.BlockSpec((seq, cw), lambda b, c: (b, cb_col + c)),
            pl.BlockSpec((seq, cw), lambda b, c: (b, cc_col + c)),
            pl.BlockSpec((seq, cw), lambda b, c: (b, cu_col + c)),
            pl.BlockSpec((CONV_K, cw), lambda b, c: (0, c)),
            pl.BlockSpec((1, cw), lambda b, c: (0, c)),
        ],
        out_specs=pl.BlockSpec((seq, cw), lambda b, c: (b, c)),
        out_shape=jax.ShapeDtypeStruct((batch * seq, width), BF16),
        compiler_params=_params(("parallel", "parallel")), name="short_conv",
    )(proj, proj, proj, conv_w, conv_b)


def _merge_kernel(attn_ref, conv_ref, wa_ref, wc_ref, ga_ref, gc_ref, bg_ref, o_ref, wa_scr, wc_scr,
                  *, n_col_blocks):
    jj = pl.program_id(0)
    i = pl.program_id(1)

    def round_chunks():
        for w_ref, w_scr in ((wa_ref, wa_scr), (wc_ref, wc_scr)):
            kc = w_ref.shape[0]
            w_scr[jj % 2, pl.ds(pl.multiple_of(i * kc, kc), kc), :] = w_ref[...].astype(BF16)

    def multiply():
        za = jnp.dot(attn_ref[...], wa_scr[(jj + 1) % 2], preferred_element_type=F32)
        zc = jnp.dot(conv_ref[...], wc_scr[(jj + 1) % 2], preferred_element_type=F32)
        ga = jax.nn.sigmoid(ga_ref[...].astype(F32) + bg_ref[0:1, :])
        gc = jax.nn.sigmoid(gc_ref[...].astype(F32) + bg_ref[1:2, :])
        o_ref[...] = (ga * za + gc * zc).astype(o_ref.dtype)

    pl.when(jj == 0)(round_chunks)
    pl.when(jj == n_col_blocks)(multiply)

    @pl.when((jj > 0) & (jj < n_col_blocks))
    def _():
        multiply()
        round_chunks()


def branch_merge(attn, conv, wa, wc, proj, b_gate, ga_col, gc_col, tm=1024, tn=1024):
    m, ka = attn.shape
    kc = conv.shape[1]
    n = wa.shape[1]
    tm, tn = min(tm, m), min(tn, n)
    ni, nj = m // tm, n // tn
    assert ka % ni == 0 and kc % ni == 0

    def rows(jj, i):
        return jnp.where(jj == 0, 0, i)

    def col(jj):
        return jnp.maximum(jj - 1, 0)

    def weight_index(jj, i):
        return (jnp.where(jj < nj, i, ni - 1), jnp.minimum(jj, nj - 1))

    kern = functools.partial(_merge_kernel, n_col_blocks=nj)
    return pl.pallas_call(
        kern,
        grid=(nj + 1, ni),
        in_specs=[
            pl.BlockSpec((tm, ka), lambda jj, i: (rows(jj, i), 0)),
            pl.BlockSpec((tm, kc), lambda jj, i: (rows(jj, i), 0)),
            pl.BlockSpec((ka // ni, tn), weight_index),
            pl.BlockSpec((kc // ni, tn), weight_index),
            pl.BlockSpec((tm, tn), lambda jj, i: (rows(jj, i), ga_col + col(jj))),
            pl.BlockSpec((tm, tn), lambda jj, i: (rows(jj, i), gc_col + col(jj))),
            pl.BlockSpec((2, tn), lambda jj, i: (0, col(jj))),
        ],
        out_specs=pl.BlockSpec((tm, tn), lambda jj, i: (rows(jj, i), col(jj))),
        out_shape=jax.ShapeDtypeStruct((m, n), BF16),
        scratch_shapes=[pltpu.VMEM((2, ka, tn), BF16), pltpu.VMEM((2, kc, tn), BF16)],
        compiler_params=_params(("arbitrary", "arbitrary")), name="branch_merge",
    )(attn, conv, wa, wc, proj, proj, b_gate)


def _topk_rows(s, k, vals_ref, rows_ref):
    nrows = s.shape[0]
    rowi = lax.broadcasted_iota(jnp.int32, s.shape, 0).astype(F32)
    for r in range(k):
        m = s.max(axis=0, keepdims=True)
        idx = jnp.where(s == m, rowi, float(nrows)).min(axis=0, keepdims=True)
        s = jnp.where(rowi == idx, -jnp.inf, s)
        vals_ref[r:r + 1, :] = m
        rows_ref[r:r + 1, :] = idx


def _topk_rows_untied(s, k, vals_ref, rows_ref):
    nrows, ncols = s.shape
    probe_rows = 2 * SUBLANES
    lane = lax.broadcasted_iota(jnp.int32, (probe_rows, nrows), 1).astype(F32)
    sub = lax.broadcasted_iota(jnp.int32, (probe_rows, nrows), 0)
    probe = jnp.where(sub == 0, lane, jnp.where(sub == 1, 1.0, 0.0)).astype(BF16)
    worst = jnp.zeros((1, ncols), F32)
    for r in range(k):
        m = s.max(axis=0, keepdims=True)
        hit = s == m
        s = jnp.where(hit, -jnp.inf, s)
        stats = jnp.dot(probe, jnp.where(hit, 1.0, 0.0).astype(BF16), preferred_element_type=F32)
        vals_ref[r:r + 1, :] = m
        rows_ref[r:r + 1, :] = stats[0:1]
        worst = jnp.maximum(worst, stats[1:2])
    return worst


def _pair_candidates(v1, v2):
    K = PEER_TOPK
    ncols = v1.shape[1]
    sub8 = lax.broadcasted_iota(jnp.int32, (SUBLANES, ncols), 0)
    sub16 = lax.broadcasted_iota(jnp.int32, (K, ncols), 0)
    big = float(K * K)
    vals = [v1[0:1] + v2]
    flat = [sub16.astype(F32)]
    for a in range(1, SUBLANES):
        ok = sub8 < K // (a + 1)
        vals.append(jnp.where(ok, v1[a:a + 1] + v2[0:SUBLANES], -jnp.inf))
        flat.append(jnp.where(ok, (sub8 + a * K).astype(F32), big))
    vals.append(v1[SUBLANES:K] + v2[0:1])
    flat.append(((sub8 + SUBLANES) * K).astype(F32))
    return jnp.concatenate(vals, axis=0), jnp.concatenate(flat, axis=0)


def _select_rows(table, which):
    out = jnp.zeros(which.shape, F32)
    for a in range(table.shape[0]):
        out = jnp.where(which == float(a), table[a:a + 1], out)
    return out


def _peer_route_kernel(q_ref, keys_ref, code_ref, gate_ref, s1_scr, s2_scr, code_scr, gate_scr, topk_scr,
                       *, chunk_tokens):
    K = PEER_TOPK
    h = pl.program_id(1)
    q = q_ref[...]
    s1_scr[...] = lax.dot_general(keys_ref[0, 0], q[:, :PEER_HALF], NT_DIMS,
                                  precision=lax.Precision.HIGHEST, preferred_element_type=F32)
    s2_scr[...] = lax.dot_general(keys_ref[0, 1], q[:, PEER_HALF:], NT_DIMS,
                                  precision=lax.Precision.HIGHEST, preferred_element_type=F32)
    row0 = pl.multiple_of(h * K, K)

    def chunk(c, carry):
        off = pl.multiple_of(c * chunk_tokens, chunk_tokens)
        cols = pl.ds(off, chunk_tokens)
        t1 = _topk_rows_untied(s1_scr[:, cols], K, topk_scr.at[0], topk_scr.at[1])
        t2 = _topk_rows_untied(s2_scr[:, cols], K, topk_scr.at[2], topk_scr.at[3])

        @pl.when(jnp.max(jnp.maximum(t1, t2)) > 1.5)
        def _():
            _topk_rows(s1_scr[:, cols], K, topk_scr.at[0], topk_scr.at[1])
            _topk_rows(s2_scr[:, cols], K, topk_scr.at[2], topk_scr.at[3])

        cand, flat = _pair_candidates(topk_scr[0], topk_scr[2])
        for r in range(K):
            m = cand.max(axis=0, keepdims=True)
            first = jnp.where(cand == m, flat, float(K * K)).min(axis=0, keepdims=True)
            cand = jnp.where(flat == first, -jnp.inf, cand)
            topk_scr[4, r:r + 1, :] = m
            topk_scr[5, r:r + 1, :] = first
        score = topk_scr[4]
        e = jnp.exp(score - score[0:1])
        pair = topk_scr[5]
        a = jnp.floor(pair * (1.0 / K))
        e1 = _select_rows(topk_scr[1], a)
        e2 = _select_rows(topk_scr[3], pair - a * K)
        code_scr[pl.ds(row0, K), pl.ds(off, chunk_tokens)] = e1 * float(PEER_N_KEYS) + e2
        gate_scr[pl.ds(row0, K), pl.ds(off, chunk_tokens)] = e / e.sum(axis=0, keepdims=True)
        return carry

    lax.fori_loop(0, q.shape[0] // chunk_tokens, chunk, 0)

    @pl.when(h == PEER_HEADS - 1)
    def _():
        code_ref[...] = code_scr[...].T.astype(jnp.int32)
        gate_ref[...] = gate_scr[...].T


def peer_route(q, sub_keys, tt=1024, chunk_tokens=1024):
    n = q.shape[0]
    tt = min(tt, n)
    chunk_tokens = min(chunk_tokens, tt)
    qd = 2 * PEER_HALF
    picks = PEER_HEADS * PEER_TOPK
    out_spec = pl.BlockSpec((tt, picks), lambda i, h: (i, 0))
    return pl.pallas_call(
        functools.partial(_peer_route_kernel, chunk_tokens=chunk_tokens),
        grid=(n // tt, PEER_HEADS),
        in_specs=[
            pl.BlockSpec((tt, qd), lambda i, h: (i, h)),
            pl.BlockSpec((1, 2, PEER_N_KEYS, PEER_HALF), lambda i, h: (h, 0, 0, 0)),
        ],
        out_specs=[out_spec, out_spec],
        out_shape=[jax.ShapeDtypeStruct((n, picks), jnp.int32),
                   jax.ShapeDtypeStruct((n, picks), F32)],
        scratch_shapes=[pltpu.VMEM((PEER_N_KEYS, tt), F32), pltpu.VMEM((PEER_N_KEYS, tt), F32),
                        pltpu.VMEM((picks, tt), F32), pltpu.VMEM((picks, tt), F32),
                        pltpu.VMEM((6, PEER_TOPK, chunk_tokens), F32)],
        compiler_params=_params(("parallel", "arbitrary")), name="peer_route",
    )(q, sub_keys)


W_ROW_PITCH = PEER_N_KEYS + SUBLANES


def _peer_weights_kernel(code_ref, gate_ref, o_ref, w_scr, *, unroll):
    nk = PEER_N_KEYS
    tt, picks = code_ref.shape
    sub = lax.broadcasted_iota(jnp.int32, (nk, picks), 0)
    zero = jnp.zeros((nk, picks), BF16)

    def one_hots(t):
        c = code_ref[pl.ds(t, 1), :]
        g = gate_ref[pl.ds(t, 1), :]
        e1 = lax.shift_right_logical(c, 7)
        e2 = lax.bitwise_and(c, nk - 1)
        at = jnp.where(sub == e1, g, 0.0).astype(BF16)
        bt = jnp.where(sub == e2, 1.0, 0.0).astype(BF16)
        return at, bt

    def body(p, carry):
        t = 2 * p
        at0, bt0 = one_hots(t)
        at1, bt1 = one_hots(t + 1)
        at = jnp.concatenate([at0, at1], axis=1)
        bt = jnp.concatenate([jnp.concatenate([bt0, zero], axis=1),
                              jnp.concatenate([zero, bt1], axis=1)], axis=0)
        w = lax.dot_general(at, bt, NT_DIMS, preferred_element_type=F32)
        row = pl.multiple_of(t * W_ROW_PITCH, SUBLANES)
        w_scr[pl.ds(row, nk), :] = w[:, :nk]
        w_scr[pl.ds(row + W_ROW_PITCH, nk), :] = w[:, nk:]
        return carry

    lax.fori_loop(0, tt // 2, body, 0, unroll=unroll)
    for e1 in range(nk):
        o_ref[:, e1 * nk:(e1 + 1) * nk] = w_scr[pl.ds(e1, tt, stride=W_ROW_PITCH), :].astype(o_ref.dtype)


def peer_dense_weights(code, gate, tt=128, unroll=64):
    n, picks = code.shape
    tt = min(tt, n)
    nk = PEER_N_KEYS
    kern = functools.partial(_peer_weights_kernel, unroll=unroll)
    return pl.pallas_call(
        kern,
        grid=(n // tt,),
        in_specs=[pl.BlockSpec((tt, picks), lambda i: (i, 0)),
                  pl.BlockSpec((tt, picks), lambda i: (i, 0))],
        out_specs=pl.BlockSpec((tt, nk * nk), lambda i: (i, 0)),
        out_shape=jax.ShapeDtypeStruct((n, nk * nk), BF16),
        scratch_shapes=[pltpu.VMEM((tt * W_ROW_PITCH, nk), F32)],
        compiler_params=_params(("parallel",)), name="peer_weights",
    )(code, gate)


def _row_rms_scale(t, g):
    return t * lax.rsqrt(jnp.mean(t * t, axis=-1, keepdims=True) + NORM_EPS) * g


def _peer_dense_kernel(x_ref, u_ref, v_ref, w_ref, h_ref, gout_ref, o_ref, acc_scr, *, n_edge, n_main):
    j = pl.program_id(1)
    edge_rows = h_ref.shape[0]

    @pl.when(j == 0)
    def _():
        acc_scr[...] = jnp.zeros(acc_scr.shape, F32)

    @pl.when(j < n_main)
    def _():
        a = lax.dot_general(x_ref[...], u_ref[...], NT_DIMS, preferred_element_type=F32)
        act = 0.5 * a * (1.0 + lax.erf(a * math.sqrt(0.5))) * w_ref[...].astype(F32)
        acc_scr[...] += jnp.dot(act.astype(BF16), v_ref[...], preferred_element_type=F32)

    @pl.when(j >= n_main)
    def _():
        r = j - n_main
        rows = pl.ds(pl.multiple_of(r * edge_rows, edge_rows), edge_rows)
        o_ref[...] = _row_rms_scale(h_ref[...] + acc_scr[rows, :], gout_ref[...])


def peer_dense_final(x, u, v, w, h, g_out, tn=1024, te=512, edge_rows=128):
    n, d = h.shape
    ne = u.shape[0]
    tn, te = min(tn, n), min(te, ne)
    edge_rows = min(edge_rows, tn)
    n_edge, n_main = tn // edge_rows, ne // te

    def main_step(j):
        return jnp.minimum(j, n_main - 1)

    def edge_chunk(i, j):
        return (i * n_edge + jnp.maximum(j - n_main, 0), 0)

    kern = functools.partial(_peer_dense_kernel, n_edge=n_edge, n_main=n_main)
    return pl.pallas_call(
        kern,
        grid=(n // tn, n_main + n_edge),
        in_specs=[
            pl.BlockSpec((tn, d), lambda i, j: (i, 0), pipeline_mode=pl.Buffered(1)),
            pl.BlockSpec((te, d), lambda i, j: (main_step(j), 0)),
            pl.BlockSpec((te, d), lambda i, j: (main_step(j), 0)),
            pl.BlockSpec((tn, te), lambda i, j: (i, main_step(j))),
            pl.BlockSpec((edge_rows, d), edge_chunk),
            pl.BlockSpec((1, d), lambda i, j: (0, 0)),
        ],
        out_specs=pl.BlockSpec((edge_rows, d), edge_chunk),
        out_shape=jax.ShapeDtypeStruct((n, d), F32),
        scratch_shapes=[pltpu.VMEM((tn, d), F32)],
        compiler_params=_params(("parallel", "arbitrary")), name="peer_dense",
    )(x, u, v, w, h, g_out.reshape(1, d))


def kernel(x, norm_mix, w_in, conv_w, conv_b, w_br_attn, w_br_conv, b_gate, rel_bias,
           w_out, norm_ffn, peer_w_q, peer_sub_keys, peer_u, peer_v, norm_final):
    batch, seq, d = x.shape
    n = batch * seq
    assert norm_mix.shape[0] == 1, "single-layer problem"
    h = x.reshape(n, d)

    hn = rmsnorm(h, norm_mix[0], BF16)
    proj = matmul(hn, w_in[0], BF16, name="in_proj")
    attn, (u_bf16, v_bf16) = moba_attention(
        proj, rel_bias, batch, seq, q_col=0, k_col=ATTN_WIDTH // HEAD_DIM, v_col=2 * ATTN_WIDTH // HEAD_DIM,
        side_tables=(peer_u[0], peer_v[0]))
    cw = 512
    c0 = 3 * ATTN_WIDTH // cw
    conv = short_conv(proj, conv_w[0].reshape(CONV_K, CONV_WIDTH), conv_b[0].reshape(1, CONV_WIDTH),
                      batch, seq, cb_col=c0, cc_col=c0 + CONV_WIDTH // cw, cu_col=c0 + 2 * CONV_WIDTH // cw)
    tn = 1024
    g0 = (3 * ATTN_WIDTH + 3 * CONV_WIDTH) // tn
    merged = branch_merge(attn, conv, w_br_attn[0], w_br_conv[0],
                          proj, b_gate[0], ga_col=g0, gc_col=g0 + d // tn, tn=tn)
    h = matmul(merged, w_out[0], F32, residual=h, tm=1024, tn=1024, name="out_proj")

    q, hn2 = norm_matmul(h, norm_ffn[0], peer_w_q[0].astype(BF16), F32, name="peer_query")
    code, gate = peer_route(q, peer_sub_keys[0])
    w = peer_dense_weights(code, gate)
    out = peer_dense_final(hn2, u_bf16, v_bf16, w, h, norm_final)
    return out.reshape(batch, seq, d)
```

```python
import functools
import math

import numpy as np
import jax
import jax.numpy as jnp
from jax import lax
from jax.experimental import pallas as pl
from jax.experimental.pallas import tpu as pltpu

F32 = jnp.float32
BF16 = jnp.bfloat16

D_MODEL = 4096
ATTN_HEADS = 16
HEAD_DIM = 128
ATTN_WIDTH = ATTN_HEADS * HEAD_DIM
MOBA_BLOCK = 256
MOBA_TOPK = 3
REL_BUCKETS = 32
REL_MAX_DIST = 128
CONV_WIDTH = 2048
CONV_K = 3
PEER_HEADS = 8
PEER_N_KEYS = 128
PEER_HALF = 128
PEER_TOPK = 16
NORM_EPS = 1e-6

V7X_VMEM_BYTES = 64 * 1024 * 1024
VMEM_LIMIT = V7X_VMEM_BYTES - 2 * 1024 * 1024
LANES = 128
SUBLANES = 8
NT_DIMS = (((1,), (1,)), ((), ()))
LOG2_E = math.log2(math.e)


def _params(semantics):
    return pltpu.CompilerParams(dimension_semantics=semantics, vmem_limit_bytes=VMEM_LIMIT)


def _rmsnorm_kernel(x_ref, g_ref, o_ref):
    x = x_ref[...]
    ms = jnp.mean(x * x, axis=-1, keepdims=True)
    o_ref[...] = (x * lax.rsqrt(ms + NORM_EPS) * g_ref[...]).astype(o_ref.dtype)


def rmsnorm(x, g, out_dtype, rows=256):
    n, d = x.shape
    rows = min(rows, n)
    row_spec = pl.BlockSpec((rows, d), lambda i: (i, 0))
    return pl.pallas_call(
        _rmsnorm_kernel, grid=(n // rows,),
        in_specs=[row_spec, pl.BlockSpec((1, d), lambda i: (0, 0))], out_specs=row_spec,
        out_shape=jax.ShapeDtypeStruct((n, d), out_dtype),
        compiler_params=_params(("parallel",)), name="rmsnorm",
    )(x, g.reshape(1, d))


def _matmul_kernel(a_ref, b_ref, *rest, n_col_blocks, has_residual):
    r_ref = rest[0] if has_residual else None
    o_ref, b_scr = rest[-2], rest[-1]
    jj = pl.program_id(0)
    i = pl.program_id(1)
    kc = b_ref.shape[0]

    def round_chunk():
        rows = pl.ds(pl.multiple_of(i * kc, kc), kc)
        b_scr[jj % 2, rows, :] = b_ref[...].astype(BF16)

    def multiply():
        acc = jnp.dot(a_ref[...], b_scr[(jj + 1) % 2], preferred_element_type=F32)
        if has_residual:
            acc = r_ref[...] + acc
        o_ref[...] = acc.astype(o_ref.dtype)

    pl.when(jj == 0)(round_chunk)
    pl.when(jj == n_col_blocks)(multiply)

    @pl.when((jj > 0) & (jj < n_col_blocks))
    def _():
        multiply()
        round_chunk()


def matmul(a, b, out_dtype, residual=None, tm=1024, tn=1024, name="matmul"):
    m, k = a.shape
    _, n = b.shape
    tm, tn = min(tm, m), min(tn, n)
    ni, nj = m // tm, n // tn
    assert k % ni == 0
    kc = k // ni

    def out_index(jj, i):
        return (jnp.where(jj == 0, 0, i), jnp.maximum(jj - 1, 0))

    a_spec = pl.BlockSpec((tm, k), lambda jj, i: (jnp.where(jj == 0, 0, i), 0))
    b_spec = pl.BlockSpec((kc, tn), lambda jj, i: (jnp.where(jj < nj, i, ni - 1), jnp.minimum(jj, nj - 1)))
    o_spec = pl.BlockSpec((tm, tn), out_index)
    ins, specs = (a, b), [a_spec, b_spec]
    if residual is not None:
        ins, specs = ins + (residual,), specs + [o_spec]
    kern = functools.partial(_matmul_kernel, n_col_blocks=nj, has_residual=residual is not None)
    return pl.pallas_call(
        kern, grid=(nj + 1, ni), in_specs=specs, out_specs=o_spec,
        out_shape=jax.ShapeDtypeStruct((m, n), out_dtype),
        scratch_shapes=[pltpu.VMEM((2, k, tn), BF16)],
        compiler_params=_params(("arbitrary", "arbitrary")), name=name,
    )(*ins)


NORM_CHUNK_ROWS = 64


def _rmsnorm_chunks(load_rows, g_ref, dst_ref, unrolled=False):
    def body(c, carry):
        start = c * NORM_CHUNK_ROWS
        rows = pl.ds(start if unrolled else pl.multiple_of(start, NORM_CHUNK_ROWS), NORM_CHUNK_ROWS)
        x = load_rows(rows)
        ms = jnp.mean(x * x, axis=-1, keepdims=True)
        dst_ref[rows, :] = (x * lax.rsqrt(ms + NORM_EPS) * g_ref[...]).astype(dst_ref.dtype)
        return carry

    n_chunks = dst_ref.shape[0] // NORM_CHUNK_ROWS
    if unrolled:
        for c in range(n_chunks):
            body(c, 0)
    else:
        lax.fori_loop(0, n_chunks, body, 0)


def _norm_matmul_kernel(x_ref, g_ref, b_ref, o_ref, hn_ref, hn_scr, *, n_row_tiles):
    s = pl.program_id(0)

    def normalise():
        _rmsnorm_chunks(lambda rows: x_ref[rows, :], g_ref, hn_scr.at[s % 2], unrolled=True)
        hn_ref[...] = hn_scr[s % 2]

    def multiply():
        o_ref[...] = jnp.dot(hn_scr[(s + 1) % 2], b_ref[...], preferred_element_type=F32).astype(o_ref.dtype)

    @pl.when(s == 0)
    def _():
        normalise()

    @pl.when((s > 0) & (s < n_row_tiles))
    def _():
        multiply()
        normalise()

    @pl.when(s == n_row_tiles)
    def _():
        multiply()


def norm_matmul(x, g, b, out_dtype, tm=512, name="norm_matmul"):
    m, k = x.shape
    _, n = b.shape
    tm = min(tm, m)
    ni = m // tm
    kern = functools.partial(_norm_matmul_kernel, n_row_tiles=ni)
    row_tile = lambda s: (jnp.minimum(s, ni - 1), 0)
    return pl.pallas_call(
        kern, grid=(ni + 1,),
        in_specs=[pl.BlockSpec((tm, k), row_tile),
                  pl.BlockSpec((1, k), lambda s: (0, 0)),
                  pl.BlockSpec((k, n), lambda s: (0, 0), pipeline_mode=pl.Buffered(1))],
        out_specs=[pl.BlockSpec((tm, n), lambda s: (jnp.maximum(s - 1, 0), 0)),
                   pl.BlockSpec((tm, k), row_tile)],
        out_shape=[jax.ShapeDtypeStruct((m, n), out_dtype), jax.ShapeDtypeStruct((m, k), BF16)],
        scratch_shapes=[pltpu.VMEM((2, tm, k), BF16)],
        compiler_params=_params(("arbitrary",)), name=name,
    )(x, g.reshape(1, k), b)


def _t5_bucket_table(max_dist):
    dist = np.arange(max_dist, dtype=np.int32)
    max_exact = REL_BUCKETS // 2
    d32 = np.maximum(dist, 1).astype(np.float32)
    ratio = (np.log(d32 / np.float32(max_exact)) / np.float32(math.log(REL_MAX_DIST / max_exact))
             * np.float32(REL_BUCKETS - max_exact))
    large = max_exact + ratio.astype(np.int32)
    large = np.minimum(large, REL_BUCKETS - 1)
    return np.where(dist < max_exact, dist, large).astype(np.int32)


def _moba_bucket_tile():
    L = MOBA_BLOCK
    key = np.arange(2 * L)[:, None]
    qi = np.arange(L)[None, :]
    dist = np.maximum(qi - key + L, 0)
    return _t5_bucket_table(2 * L)[dist]


def _moba_kernel(tab_ref, q_ref, k_ref, v_ref, bucket_ref, *rest, n_blocks, n_side, steps_per_table):
    side_in, o_ref, side_out, bias_scr = rest[:n_side], rest[n_side], rest[n_side + 1:-1], rest[-1]
    L = MOBA_BLOCK
    h = pl.program_id(0)
    b = pl.program_id(1)

    step = h * pl.num_programs(1) + b
    for t in range(n_side):
        @pl.when((step >= t * steps_per_table) & (step < (t + 1) * steps_per_table))
        def _(t=t):
            def body(c, carry):
                rows = pl.ds(pl.multiple_of(c * NORM_CHUNK_ROWS, NORM_CHUNK_ROWS), NORM_CHUNK_ROWS)
                side_out[t][rows, :] = side_in[t][rows, :].astype(BF16)
                return carry
            lax.fori_loop(0, side_in[t].shape[0] // NORM_CHUNK_ROWS, body, 0)

    @pl.when(b == 0)
    def _():
        bk = bucket_ref[...]
        acc = jnp.zeros(bk.shape, F32)
        for kk in range(REL_BUCKETS):
            acc = jnp.where(bk == kk, tab_ref[kk, h], acc)
        bias_scr[...] = acc * LOG2_E

    neg_inf = jnp.float32(-jnp.inf)
    far = tab_ref[REL_BUCKETS - 1, h] * LOG2_E
    scale = HEAD_DIM ** -0.5 * LOG2_E
    k_means = [jnp.mean(k_ref[n * L:(n + 1) * L, :].astype(F32), axis=0, keepdims=True)
               for n in range(n_blocks)]
    pad = jnp.zeros((SUBLANES - n_blocks % SUBLANES, HEAD_DIM), F32) if n_blocks % SUBLANES else None
    k_mean = jnp.concatenate(k_means + ([pad] if pad is not None else []), axis=0)
    v_t = v_ref[...].astype(F32).T.astype(BF16)
    key_i = lax.broadcasted_iota(jnp.int32, (L, L), 0)
    qry_i = lax.broadcasted_iota(jnp.int32, (L, L), 1)

    for qb in range(n_blocks):
        q = q_ref[qb * L:(qb + 1) * L, :]
        sel = None
        if qb > MOBA_TOPK:
            gate = lax.dot_general(k_mean, q.astype(F32), NT_DIMS,
                                   precision=lax.Precision.HIGHEST, preferred_element_type=F32)
            g = [gate[m:m + 1, :] for m in range(qb)]
            sel = []
            for n in range(qb):
                rank = jnp.zeros((1, L), F32)
                for m in range(qb):
                    if m != n:
                        beats = (g[m] > g[n]) | (g[m] == g[n]) if m < n else (g[m] > g[n])
                        rank = rank + jnp.where(beats, 1.0, 0.0)
                sel.append(jnp.where(rank < MOBA_TOPK, 1.0, 0.0))
        logits = []
        for n in range(qb + 1):
            s = lax.dot_general(k_ref[n * L:(n + 1) * L, :], q, NT_DIMS, preferred_element_type=F32)
            if n == qb:
                logits.append(jnp.where(qry_i >= key_i, s * scale + bias_scr[L:2 * L, :], neg_inf))
            else:
                lg = s * scale + (bias_scr[0:L, :] if n == qb - 1 else far)
                logits.append(lg if sel is None else jnp.where(sel[n] > 0.5, lg, neg_inf))
        mx = logits[0].max(axis=0, keepdims=True)
        for lg in logits[1:]:
            mx = jnp.maximum(mx, lg.max(axis=0, keepdims=True))
        denom = jnp.zeros((1, L), F32)
        acc = jnp.zeros((HEAD_DIM, L), F32)
        for n, lg in enumerate(logits):
            p = jnp.exp2(lg - mx)
            denom = denom + p.sum(axis=0, keepdims=True)
            acc = acc + jnp.dot(v_t[:, n * L:(n + 1) * L], p.astype(BF16), preferred_element_type=F32)
        o_ref[qb * L:(qb + 1) * L, :] = (acc / denom).T.astype(o_ref.dtype)


def moba_attention(proj, rel_bias, batch, seq, q_col, k_col, v_col, side_tables=()):
    L = MOBA_BLOCK
    n_blocks = seq // L
    bucket = jnp.asarray(_moba_bucket_tile())
    n_side = len(side_tables)
    n_steps = ATTN_HEADS * batch
    steps_per_table = n_steps // max(n_side, 1)
    side_specs, side_shapes = [], []
    for t, tab in enumerate(side_tables):
        rows, cols = tab.shape
        assert rows % steps_per_table == 0
        def index_map(h, b, t=t):
            return (jnp.clip(h * batch + b - t * steps_per_table, 0, steps_per_table - 1), 0)
        side_specs.append(pl.BlockSpec((rows // steps_per_table, cols), index_map))
        side_shapes.append(jax.ShapeDtypeStruct(tab.shape, BF16))
    kern = functools.partial(_moba_kernel, n_blocks=n_blocks, n_side=n_side, steps_per_table=steps_per_table)
    outs = pl.pallas_call(
        kern,
        grid=(ATTN_HEADS, batch),
        in_specs=[
            pl.BlockSpec(memory_space=pltpu.SMEM),
            pl.BlockSpec((seq, HEAD_DIM), lambda h, b: (b, q_col + h)),
            pl.BlockSpec((seq, HEAD_DIM), lambda h, b: (b, k_col + h)),
            pl.BlockSpec((seq, HEAD_DIM), lambda h, b: (b, v_col + h)),
            pl.BlockSpec((2 * L, L), lambda h, b: (0, 0)),
        ] + side_specs,
        out_specs=[pl.BlockSpec((seq, HEAD_DIM), lambda h, b: (b, h))] + side_specs,
        out_shape=[jax.ShapeDtypeStruct((batch * seq, ATTN_WIDTH), BF16)] + side_shapes,
        scratch_shapes=[pltpu.VMEM((2 * L, L), F32)],
        compiler_params=_params(("arbitrary", "arbitrary")), name="moba",
    )(rel_bias, proj, proj, proj, bucket, *side_tables)
    return outs[0], tuple(outs[1:])


def _conv_kernel(cb_ref, cc_ref, cu_ref, w_ref, b_ref, o_ref):
    g = cc_ref[...].astype(F32) * cu_ref[...].astype(F32)
    row = lax.broadcasted_iota(jnp.int32, g.shape, 0)
    y = w_ref[CONV_K - 1:CONV_K, :] * g
    for s in range(1, CONV_K):
        shifted = jnp.where(row >= s, pltpu.roll(g, s, axis=0), 0.0)
        y = y + w_ref[CONV_K - 1 - s:CONV_K - s, :] * shifted
    o_ref[...] = (cb_ref[...].astype(F32) * (y + b_ref[...])).astype(o_ref.dtype)


def short_conv(proj, conv_w, conv_b, batch, seq, cb_col, cc_col, cu_col, cw=512):
    width = conv_w.shape[1]
    return pl.pallas_call(
        _conv_kernel,
        grid=(batch, width // cw),
        in_specs=[
            pl.BlockSpec((seq, cw), lambda b, c: (b, cb_col + c)),
            pl.BlockSpec((seq, cw), lambda b, c: (b, cc_col + c)),
            pl.BlockSpec((seq, cw), lambda b, c: (b, cu_col + c)),
            pl.BlockSpec((CONV_K, cw), lambda b, c: (0, c)),
            pl.BlockSpec((1, cw), lambda b, c: (0, c)),
        ],
        out_specs=pl.BlockSpec((seq, cw), lambda b, c: (b, c)),
        out_shape=jax.ShapeDtypeStruct((batch * seq, width), BF16),
        compiler_params=_params(("parallel", "parallel")), name="short_conv",
    )(proj, proj, proj, conv_w, conv_b)


def _merge_kernel(attn_ref, conv_ref, wa_ref, wc_ref, ga_ref, gc_ref, bg_ref, o_ref, wa_scr, wc_scr,
                  *, n_col_blocks):
    jj = pl.program_id(0)
    i = pl.program_id(1)

    def round_chunks():
        for w_ref, w_scr in ((wa_ref, wa_scr), (wc_ref, wc_scr)):
            kc = w_ref.shape[0]
            w_scr[jj % 2, pl.ds(pl.multiple_of(i * kc, kc), kc), :] = w_ref[...].astype(BF16)

    def multiply():
        za = jnp.dot(attn_ref[...], wa_scr[(jj + 1) % 2], preferred_element_type=F32)
        zc = jnp.dot(conv_ref[...], wc_scr[(jj + 1) % 2], preferred_element_type=F32)
        ga = jax.nn.sigmoid(ga_ref[...].astype(F32) + bg_ref[0:1, :])
        gc = jax.nn.sigmoid(gc_ref[...].astype(F32) + bg_ref[1:2, :])
        o_ref[...] = (ga * za + gc * zc).astype(o_ref.dtype)

    pl.when(jj == 0)(round_chunks)
    pl.when(jj == n_col_blocks)(multiply)

    @pl.when((jj > 0) & (jj < n_col_blocks))
    def _():
        multiply()
        round_chunks()


def branch_merge(attn, conv, wa, wc, proj, b_gate, ga_col, gc_col, tm=1024, tn=1024):
    m, ka = attn.shape
    kc = conv.shape[1]
    n = wa.shape[1]
    tm, tn = min(tm, m), min(tn, n)
    ni, nj = m // tm, n // tn
    assert ka % ni == 0 and kc % ni == 0

    def rows(jj, i):
        return jnp.where(jj == 0, 0, i)

    def col(jj):
        return jnp.maximum(jj - 1, 0)

    def weight_index(jj, i):
        return (jnp.where(jj < nj, i, ni - 1), jnp.minimum(jj, nj - 1))

    kern = functools.partial(_merge_kernel, n_col_blocks=nj)
    return pl.pallas_call(
        kern,
        grid=(nj + 1, ni),
        in_specs=[
            pl.BlockSpec((tm, ka), lambda jj, i: (rows(jj, i), 0)),
            pl.BlockSpec((tm, kc), lambda jj, i: (rows(jj, i), 0)),
            pl.BlockSpec((ka // ni, tn), weight_index),
            pl.BlockSpec((kc // ni, tn), weight_index),
            pl.BlockSpec((tm, tn), lambda jj, i: (rows(jj, i), ga_col + col(jj))),
            pl.BlockSpec((tm, tn), lambda jj, i: (rows(jj, i), gc_col + col(jj))),
            pl.BlockSpec((2, tn), lambda jj, i: (0, col(jj))),
        ],
        out_specs=pl.BlockSpec((tm, tn), lambda jj, i: (rows(jj, i), col(jj))),
        out_shape=jax.ShapeDtypeStruct((m, n), BF16),
        scratch_shapes=[pltpu.VMEM((2, ka, tn), BF16), pltpu.VMEM((2, kc, tn), BF16)],
        compiler_params=_params(("arbitrary", "arbitrary")), name="branch_merge",
    )(attn, conv, wa, wc, proj, proj, b_gate)


def _topk_rows(s, k, vals_ref, rows_ref):
    nrows = s.shape[0]
    rowi = lax.broadcasted_iota(jnp.int32, s.shape, 0).astype(F32)
    for r in range(k):
        m = s.max(axis=0, keepdims=True)
        idx = jnp.where(s == m, rowi, float(nrows)).min(axis=0, keepdims=True)
        s = jnp.where(rowi == idx, -jnp.inf, s)
        vals_ref[r:r + 1, :] = m
        rows_ref[r:r + 1, :] = idx


def _topk_rows_untied(s, k, vals_ref, rows_ref):
    nrows, ncols = s.shape
    probe_rows = 2 * SUBLANES
    lane = lax.broadcasted_iota(jnp.int32, (probe_rows, nrows), 1).astype(F32)
    sub = lax.broadcasted_iota(jnp.int32, (probe_rows, nrows), 0)
    probe = jnp.where(sub == 0, lane, jnp.where(sub == 1, 1.0, 0.0)).astype(BF16)
    worst = jnp.zeros((1, ncols), F32)
    for r in range(k):
        m = s.max(axis=0, keepdims=True)
        hit = s == m
        s = jnp.where(hit, -jnp.inf, s)
        stats = jnp.dot(probe, jnp.where(hit, 1.0, 0.0).astype(BF16), preferred_element_type=F32)
        vals_ref[r:r + 1, :] = m
        rows_ref[r:r + 1, :] = stats[0:1]
        worst = jnp.maximum(worst, stats[1:2])
    return worst


def _pair_candidates(v1, v2):
    K = PEER_TOPK
    ncols = v1.shape[1]
    sub8 = lax.broadcasted_iota(jnp.int32, (SUBLANES, ncols), 0)
    sub16 = lax.broadcasted_iota(jnp.int32, (K, ncols), 0)
    big = float(K * K)
    vals = [v1[0:1] + v2]
    flat = [sub16.astype(F32)]
    for a in range(1, SUBLANES):
        ok = sub8 < K // (a + 1)
        vals.append(jnp.where(ok, v1[a:a + 1] + v2[0:SUBLANES], -jnp.inf))
        flat.append(jnp.where(ok, (sub8 + a * K).astype(F32), big))
    vals.append(v1[SUBLANES:K] + v2[0:1])
    flat.append(((sub8 + SUBLANES) * K).astype(F32))
    return jnp.concatenate(vals, axis=0), jnp.concatenate(flat, axis=0)


def _select_rows(table, which):
    out = jnp.zeros(which.shape, F32)
    for a in range(table.shape[0]):
        out = jnp.where(which == float(a), table[a:a + 1], out)
    return out


def _peer_route_kernel(q_ref, keys_ref, code_ref, gate_ref, s1_scr, s2_scr, code_scr, gate_scr, topk_scr,
                       *, chunk_tokens):
    K = PEER_TOPK
    h = pl.program_id(1)
    q = q_ref[...]
    s1_scr[...] = lax.dot_general(keys_ref[0, 0], q[:, :PEER_HALF], NT_DIMS,
                                  precision=lax.Precision.HIGHEST, preferred_element_type=F32)
    s2_scr[...] = lax.dot_general(keys_ref[0, 1], q[:, PEER_HALF:], NT_DIMS,
                                  precision=lax.Precision.HIGHEST, preferred_element_type=F32)
    row0 = pl.multiple_of(h * K, K)

    def chunk(c, carry):
        off = pl.multiple_of(c * chunk_tokens, chunk_tokens)
        cols = pl.ds(off, chunk_tokens)
        t1 = _topk_rows_untied(s1_scr[:, cols], K, topk_scr.at[0], topk_scr.at[1])
        t2 = _topk_rows_untied(s2_scr[:, cols], K, topk_scr.at[2], topk_scr.at[3])

        @pl.when(jnp.max(jnp.maximum(t1, t2)) > 1.5)
        def _():
            _topk_rows(s1_scr[:, cols], K, topk_scr.at[0], topk_scr.at[1])
            _topk_rows(s2_scr[:, cols], K, topk_scr.at[2], topk_scr.at[3])

        cand, flat = _pair_candidates(topk_scr[0], topk_scr[2])
        for r in range(K):
            m = cand.max(axis=0, keepdims=True)
            first = jnp.where(cand == m, flat, float(K * K)).min(axis=0, keepdims=True)
            cand = jnp.where(flat == first, -jnp.inf, cand)
            topk_scr[4, r:r + 1, :] = m
            topk_scr[5, r:r + 1, :] = first
        score = topk_scr[4]
        e = jnp.exp(score - score[0:1])
        pair = topk_scr[5]
        a = jnp.floor(pair * (1.0 / K))
        e1 = _select_rows(topk_scr[1], a)
        e2 = _select_rows(topk_scr[3], pair - a * K)
        code_scr[pl.ds(row0, K), pl.ds(off, chunk_tokens)] = e1 * float(PEER_N_KEYS) + e2
        gate_scr[pl.ds(row0, K), pl.ds(off, chunk_tokens)] = e / e.sum(axis=0, keepdims=True)
        return carry

    lax.fori_loop(0, q.shape[0] // chunk_tokens, chunk, 0)

    @pl.when(h == PEER_HEADS - 1)
    def _():
        code_ref[...] = code_scr[...].T.astype(jnp.int32)
        gate_ref[...] = gate_scr[...].T


def peer_route(q, sub_keys, tt=1024, chunk_tokens=1024):
    n = q.shape[0]
    tt = min(tt, n)
    chunk_tokens = min(chunk_tokens, tt)
    qd = 2 * PEER_HALF
    picks = PEER_HEADS * PEER_TOPK
    out_spec = pl.BlockSpec((tt, picks), lambda i, h: (i, 0))
    return pl.pallas_call(
        functools.partial(_peer_route_kernel, chunk_tokens=chunk_tokens),
        grid=(n // tt, PEER_HEADS),
        in_specs=[
            pl.BlockSpec((tt, qd), lambda i, h: (i, h)),
            pl.BlockSpec((1, 2, PEER_N_KEYS, PEER_HALF), lambda i, h: (h, 0, 0, 0)),
        ],
        out_specs=[out_spec, out_spec],
        out_shape=[jax.ShapeDtypeStruct((n, picks), jnp.int32),
                   jax.ShapeDtypeStruct((n, picks), F32)],
        scratch_shapes=[pltpu.VMEM((PEER_N_KEYS, tt), F32), pltpu.VMEM((PEER_N_KEYS, tt), F32),
                        pltpu.VMEM((picks, tt), F32), pltpu.VMEM((picks, tt), F32),
                        pltpu.VMEM((6, PEER_TOPK, chunk_tokens), F32)],
        compiler_params=_params(("parallel", "arbitrary")), name="peer_route",
    )(q, sub_keys)


W_ROW_PITCH = PEER_N_KEYS + SUBLANES


def _peer_weights_kernel(code_ref, gate_ref, o_ref, w_scr, *, unroll):
    nk = PEER_N_KEYS
    tt, picks = code_ref.shape
    sub = lax.broadcasted_iota(jnp.int32, (nk, picks), 0)
    zero = jnp.zeros((nk, picks), BF16)

    def one_hots(t):
        c = code_ref[pl.ds(t, 1), :]
        g = gate_ref[pl.ds(t, 1), :]
        e1 = lax.shift_right_logical(c, 7)
        e2 = lax.bitwise_and(c, nk - 1)
        at = jnp.where(sub == e1, g, 0.0).astype(BF16)
        bt = jnp.where(sub == e2, 1.0, 0.0).astype(BF16)
        return at, bt

    def body(p, carry):
        t = 2 * p
        at0, bt0 = one_hots(t)
        at1, bt1 = one_hots(t + 1)
        at = jnp.concatenate([at0, at1], axis=1)
        bt = jnp.concatenate([jnp.concatenate([bt0, zero], axis=1),
                              jnp.concatenate([zero, bt1], axis=1)], axis=0)
        w = lax.dot_general(at, bt, NT_DIMS, preferred_element_type=F32)
        row = pl.multiple_of(t * W_ROW_PITCH, SUBLANES)
        w_scr[pl.ds(row, nk), :] = w[:, :nk]
        w_scr[pl.ds(row + W_ROW_PITCH, nk), :] = w[:, nk:]
        return carry

    lax.fori_loop(0, tt // 2, body, 0, unroll=unroll)
    for e1 in range(nk):
        o_ref[:, e1 * nk:(e1 + 1) * nk] = w_scr[pl.ds(e1, tt, stride=W_ROW_PITCH), :].astype(o_ref.dtype)


def peer_dense_weights(code, gate, tt=128, unroll=64):
    n, picks = code.shape
    tt = min(tt, n)
    nk = PEER_N_KEYS
    kern = functools.partial(_peer_weights_kernel, unroll=unroll)
    return pl.pallas_call(
        kern,
        grid=(n // tt,),
        in_specs=[pl.BlockSpec((tt, picks), lambda i: (i, 0)),
                  pl.BlockSpec((tt, picks), lambda i: (i, 0))],
        out_specs=pl.BlockSpec((tt, nk * nk), lambda i: (i, 0)),
        out_shape=jax.ShapeDtypeStruct((n, nk * nk), BF16),
        scratch_shapes=[pltpu.VMEM((tt * W_ROW_PITCH, nk), F32)],
        compiler_params=_params(("parallel",)), name="peer_weights",
    )(code, gate)


def _row_rms_scale(t, g):
    return t * lax.rsqrt(jnp.mean(t * t, axis=-1, keepdims=True) + NORM_EPS) * g


def _peer_dense_kernel(x_ref, u_ref, v_ref, w_ref, h_ref, gout_ref, o_ref, acc_scr, *, n_edge, n_main):
    j = pl.program_id(1)
    edge_rows = h_ref.shape[0]

    @pl.when(j == 0)
    def _():
        acc_scr[...] = jnp.zeros(acc_scr.shape, F32)

    @pl.when(j < n_main)
    def _():
        a = lax.dot_general(x_ref[...], u_ref[...], NT_DIMS, preferred_element_type=F32)
        act = 0.5 * a * (1.0 + lax.erf(a * math.sqrt(0.5))) * w_ref[...].astype(F32)
        acc_scr[...] += jnp.dot(act.astype(BF16), v_ref[...], preferred_element_type=F32)

    @pl.when(j >= n_main)
    def _():
        r = j - n_main
        rows = pl.ds(pl.multiple_of(r * edge_rows, edge_rows), edge_rows)
        o_ref[...] = _row_rms_scale(h_ref[...] + acc_scr[rows, :], gout_ref[...])


def peer_dense_final(x, u, v, w, h, g_out, tn=1024, te=512, edge_rows=128):
    n, d = h.shape
    ne = u.shape[0]
    tn, te = min(tn, n), min(te, ne)
    edge_rows = min(edge_rows, tn)
    n_edge, n_main = tn // edge_rows, ne // te

    def main_step(j):
        return jnp.minimum(j, n_main - 1)

    def edge_chunk(i, j):
        return (i * n_edge + jnp.maximum(j - n_main, 0), 0)

    kern = functools.partial(_peer_dense_kernel, n_edge=n_edge, n_main=n_main)
    return pl.pallas_call(
        kern,
        grid=(n // tn, n_main + n_edge),
        in_specs=[
            pl.BlockSpec((tn, d), lambda i, j: (i, 0)),
            pl.BlockSpec((te, d), lambda i, j: (main_step(j), 0)),
            pl.BlockSpec((te, d), lambda i, j: (main_step(j), 0)),
            pl.BlockSpec((tn, te), lambda i, j: (i, main_step(j))),
            pl.BlockSpec((edge_rows, d), edge_chunk),
            pl.BlockSpec((1, d), lambda i, j: (0, 0)),
        ],
        out_specs=pl.BlockSpec((edge_rows, d), edge_chunk),
        out_shape=jax.ShapeDtypeStruct((n, d), F32),
        scratch_shapes=[pltpu.VMEM((tn, d), F32)],
        compiler_params=_params(("parallel", "arbitrary")), name="peer_dense",
    )(x, u, v, w, h, g_out.reshape(1, d))


def kernel(x, norm_mix, w_in, conv_w, conv_b, w_br_attn, w_br_conv, b_gate, rel_bias,
           w_out, norm_ffn, peer_w_q, peer_sub_keys, peer_u, peer_v, norm_final):
    batch, seq, d = x.shape
    n = batch * seq
    assert norm_mix.shape[0] == 1, "single-layer problem"
    h = x.reshape(n, d)

    hn = rmsnorm(h, norm_mix[0], BF16)
    proj = matmul(hn, w_in[0], BF16, name="in_proj")
    attn, (u_bf16, v_bf16) = moba_attention(
        proj, rel_bias, batch, seq, q_col=0, k_col=ATTN_WIDTH // HEAD_DIM, v_col=2 * ATTN_WIDTH // HEAD_DIM,
        side_tables=(peer_u[0], peer_v[0]))
    cw = 512
    c0 = 3 * ATTN_WIDTH // cw
    conv = short_conv(proj, conv_w[0].reshape(CONV_K, CONV_WIDTH), conv_b[0].reshape(1, CONV_WIDTH),
                      batch, seq, cb_col=c0, cc_col=c0 + CONV_WIDTH // cw, cu_col=c0 + 2 * CONV_WIDTH // cw)
    tn = 1024
    g0 = (3 * ATTN_WIDTH + 3 * CONV_WIDTH) // tn
    merged = branch_merge(attn, conv, w_br_attn[0], w_br_conv[0],
                          proj, b_gate[0], ga_col=g0, gc_col=g0 + d // tn, tn=tn)
    h = matmul(merged, w_out[0], F32, residual=h, tm=1024, tn=1024, name="out_proj")

    q, hn2 = norm_matmul(h, norm_ffn[0], peer_w_q[0].astype(BF16), F32, name="peer_query")
    code, gate = peer_route(q, peer_sub_keys[0])
    w = peer_dense_weights(code, gate)
    out = peer_dense_final(hn2, u_bf16, v_bf16, w, h, norm_final)
    return out.reshape(batch, seq, d)
```

```python
import functools
import math

import numpy as np
import jax
import jax.numpy as jnp
from jax import lax
from jax.experimental import pallas as pl
from jax.experimental.pallas import tpu as pltpu

F32 = jnp.float32
BF16 = jnp.bfloat16

D_MODEL = 4096
ATTN_HEADS = 16
HEAD_DIM = 128
ATTN_WIDTH = ATTN_HEADS * HEAD_DIM
MOBA_BLOCK = 256
MOBA_TOPK = 3
REL_BUCKETS = 32
REL_MAX_DIST = 128
CONV_WIDTH = 2048
CONV_K = 3
PEER_HEADS = 8
PEER_N_KEYS = 128
PEER_HALF = 128
PEER_TOPK = 16
NORM_EPS = 1e-6

V7X_VMEM_BYTES = 64 * 1024 * 1024
VMEM_LIMIT = V7X_VMEM_BYTES - 2 * 1024 * 1024
LANES = 128
SUBLANES = 8
NT_DIMS = (((1,), (1,)), ((), ()))
LOG2_E = math.log2(math.e)


def _params(semantics):
    return pltpu.CompilerParams(dimension_semantics=semantics, vmem_limit_bytes=VMEM_LIMIT)


def _rmsnorm_kernel(x_ref, g_ref, o_ref):
    x = x_ref[...]
    ms = jnp.mean(x * x, axis=-1, keepdims=True)
    o_ref[...] = (x * lax.rsqrt(ms + NORM_EPS) * g_ref[...]).astype(o_ref.dtype)


def rmsnorm(x, g, out_dtype, rows=256):
    n, d = x.shape
    rows = min(rows, n)
    row_spec = pl.BlockSpec((rows, d), lambda i: (i, 0))
    return pl.pallas_call(
        _rmsnorm_kernel, grid=(n // rows,),
        in_specs=[row_spec, pl.BlockSpec((1, d), lambda i: (0, 0))], out_specs=row_spec,
        out_shape=jax.ShapeDtypeStruct((n, d), out_dtype),
        compiler_params=_params(("parallel",)), name="rmsnorm",
    )(x, g.reshape(1, d))


def _matmul_kernel(a_ref, b_ref, *rest, n_col_blocks, has_residual, n_side_blocks):
    rest = list(rest)
    b_scr = rest.pop()
    r_ref = rest.pop(0) if has_residual else None
    side_in = rest.pop(0) if n_side_blocks else None
    o_ref = rest.pop(0)
    jj = pl.program_id(0)
    i = pl.program_id(1)
    kc = b_ref.shape[0]

    if n_side_blocks:
        @pl.when(jj * pl.num_programs(1) + i < n_side_blocks)
        def _():
            rest[0][...] = side_in[...].astype(BF16)

    def round_chunk():
        rows = pl.ds(pl.multiple_of(i * kc, kc), kc)
        b_scr[jj % 2, rows, :] = b_ref[...].astype(BF16)

    def multiply():
        acc = jnp.dot(a_ref[...], b_scr[(jj + 1) % 2], preferred_element_type=F32)
        if has_residual:
            acc = r_ref[...] + acc
        o_ref[...] = acc.astype(o_ref.dtype)

    pl.when(jj == 0)(round_chunk)
    pl.when(jj == n_col_blocks)(multiply)

    @pl.when((jj > 0) & (jj < n_col_blocks))
    def _():
        multiply()
        round_chunk()


SIDE_BLOCK_ROWS = 32


def matmul(a, b, out_dtype, residual=None, side_table=None, tm=1024, tn=1024, name="matmul"):
    m, k = a.shape
    _, n = b.shape
    tm, tn = min(tm, m), min(tn, n)
    ni, nj = m // tm, n // tn
    assert k % ni == 0
    kc = k // ni

    def out_index(jj, i):
        return (jnp.where(jj == 0, 0, i), jnp.maximum(jj - 1, 0))

    a_spec = pl.BlockSpec((tm, k), lambda jj, i: (jnp.where(jj == 0, 0, i), 0))
    b_spec = pl.BlockSpec((kc, tn), lambda jj, i: (jnp.where(jj < nj, i, ni - 1), jnp.minimum(jj, nj - 1)))
    o_spec = pl.BlockSpec((tm, tn), out_index)
    ins, specs = (a, b), [a_spec, b_spec]
    out_specs, out_shape = o_spec, jax.ShapeDtypeStruct((m, n), out_dtype)
    if residual is not None:
        ins, specs = ins + (residual,), specs + [o_spec]
    n_side_blocks = 0
    if side_table is not None:
        rows, cols = side_table.shape
        n_side_blocks = rows // SIDE_BLOCK_ROWS
        assert rows % SIDE_BLOCK_ROWS == 0 and n_side_blocks <= (nj + 1) * ni
        side_spec = pl.BlockSpec((SIDE_BLOCK_ROWS, cols),
                                 lambda jj, i: (jnp.minimum(jj * ni + i, n_side_blocks - 1), 0))
        ins, specs = ins + (side_table,), specs + [side_spec]
        out_specs = [o_spec, side_spec]
        out_shape = [out_shape, jax.ShapeDtypeStruct(side_table.shape, BF16)]
    kern = functools.partial(_matmul_kernel, n_col_blocks=nj, has_residual=residual is not None,
                             n_side_blocks=n_side_blocks)
    return pl.pallas_call(
        kern, grid=(nj + 1, ni), in_specs=specs, out_specs=out_specs, out_shape=out_shape,
        scratch_shapes=[pltpu.VMEM((2, k, tn), BF16)],
        compiler_params=_params(("arbitrary", "arbitrary")), name=name,
    )(*ins)


NORM_CHUNK_ROWS = 64


def _rmsnorm_chunks(load_rows, g_ref, dst_ref, unrolled=False):
    def body(c, carry):
        start = c * NORM_CHUNK_ROWS
        rows = pl.ds(start if unrolled else pl.multiple_of(start, NORM_CHUNK_ROWS), NORM_CHUNK_ROWS)
        x = load_rows(rows)
        ms = jnp.mean(x * x, axis=-1, keepdims=True)
        dst_ref[rows, :] = (x * lax.rsqrt(ms + NORM_EPS) * g_ref[...]).astype(dst_ref.dtype)
        return carry

    n_chunks = dst_ref.shape[0] // NORM_CHUNK_ROWS
    if unrolled:
        for c in range(n_chunks):
            body(c, 0)
    else:
        lax.fori_loop(0, n_chunks, body, 0)


def _norm_matmul_kernel(x_ref, g_ref, b_ref, o_ref, hn_ref, hn_scr, *, n_row_tiles):
    s = pl.program_id(0)

    def normalise():
        _rmsnorm_chunks(lambda rows: x_ref[rows, :], g_ref, hn_scr.at[s % 2], unrolled=True)
        hn_ref[...] = hn_scr[s % 2]

    def multiply():
        o_ref[...] = jnp.dot(hn_scr[(s + 1) % 2], b_ref[...], preferred_element_type=F32).astype(o_ref.dtype)

    @pl.when(s == 0)
    def _():
        normalise()

    @pl.when((s > 0) & (s < n_row_tiles))
    def _():
        multiply()
        normalise()

    @pl.when(s == n_row_tiles)
    def _():
        multiply()


def norm_matmul(x, g, b, out_dtype, tm=256, name="norm_matmul"):
    m, k = x.shape
    _, n = b.shape
    tm = min(tm, m)
    ni = m // tm
    kern = functools.partial(_norm_matmul_kernel, n_row_tiles=ni)
    row_tile = lambda s: (jnp.minimum(s, ni - 1), 0)
    return pl.pallas_call(
        kern, grid=(ni + 1,),
        in_specs=[pl.BlockSpec((tm, k), row_tile),
                  pl.BlockSpec((1, k), lambda s: (0, 0)),
                  pl.BlockSpec((k, n), lambda s: (0, 0), pipeline_mode=pl.Buffered(1))],
        out_specs=[pl.BlockSpec((tm, n), lambda s: (jnp.maximum(s - 1, 0), 0)),
                   pl.BlockSpec((tm, k), row_tile)],
        out_shape=[jax.ShapeDtypeStruct((m, n), out_dtype), jax.ShapeDtypeStruct((m, k), BF16)],
        scratch_shapes=[pltpu.VMEM((2, tm, k), BF16)],
        compiler_params=_params(("arbitrary",)), name=name,
    )(x, g.reshape(1, k), b)


def _t5_bucket_table(max_dist):
    dist = np.arange(max_dist, dtype=np.int32)
    max_exact = REL_BUCKETS // 2
    d32 = np.maximum(dist, 1).astype(np.float32)
    ratio = (np.log(d32 / np.float32(max_exact)) / np.float32(math.log(REL_MAX_DIST / max_exact))
             * np.float32(REL_BUCKETS - max_exact))
    large = max_exact + ratio.astype(np.int32)
    large = np.minimum(large, REL_BUCKETS - 1)
    return np.where(dist < max_exact, dist, large).astype(np.int32)


def _moba_bucket_tile():
    L = MOBA_BLOCK
    key = np.arange(2 * L)[:, None]
    qi = np.arange(L)[None, :]
    dist = np.maximum(qi - key + L, 0)
    return _t5_bucket_table(2 * L)[dist]


def _moba_kernel(tab_ref, q_ref, k_ref, v_ref, bucket_ref, *rest, n_blocks, n_side, steps_per_table):
    side_in, o_ref, side_out, bias_scr = rest[:n_side], rest[n_side], rest[n_side + 1:-1], rest[-1]
    L = MOBA_BLOCK
    h = pl.program_id(0)
    b = pl.program_id(1)

    step = h * pl.num_programs(1) + b
    for t in range(n_side):
        @pl.when((step >= t * steps_per_table) & (step < (t + 1) * steps_per_table))
        def _(t=t):
            def body(c, carry):
                rows = pl.ds(pl.multiple_of(c * NORM_CHUNK_ROWS, NORM_CHUNK_ROWS), NORM_CHUNK_ROWS)
                side_out[t][rows, :] = side_in[t][rows, :].astype(BF16)
                return carry
            lax.fori_loop(0, side_in[t].shape[0] // NORM_CHUNK_ROWS, body, 0)

    @pl.when(b == 0)
    def _():
        bk = bucket_ref[...]
        acc = jnp.zeros(bk.shape, F32)
        for kk in range(REL_BUCKETS):
            acc = jnp.where(bk == kk, tab_ref[kk, h], acc)
        bias_scr[...] = acc * LOG2_E

    neg_inf = jnp.float32(-jnp.inf)
    far = tab_ref[REL_BUCKETS - 1, h] * LOG2_E
    scale = HEAD_DIM ** -0.5 * LOG2_E
    k_means = [jnp.mean(k_ref[n * L:(n + 1) * L, :].astype(F32), axis=0, keepdims=True)
               for n in range(n_blocks)]
    pad = jnp.zeros((SUBLANES - n_blocks % SUBLANES, HEAD_DIM), F32) if n_blocks % SUBLANES else None
    k_mean = jnp.concatenate(k_means + ([pad] if pad is not None else []), axis=0)
    v_t = v_ref[...].astype(F32).T.astype(BF16)
    key_i = lax.broadcasted_iota(jnp.int32, (L, L), 0)
    qry_i = lax.broadcasted_iota(jnp.int32, (L, L), 1)

    for qb in range(n_blocks):
        q = q_ref[qb * L:(qb + 1) * L, :]
        sel = None
        if qb > MOBA_TOPK:
            gate = lax.dot_general(k_mean, q.astype(F32), NT_DIMS,
                                   precision=lax.Precision.HIGHEST, preferred_element_type=F32)
            g = [gate[m:m + 1, :] for m in range(qb)]
            sel = []
            for n in range(qb):
                rank = jnp.zeros((1, L), F32)
                for m in range(qb):
                    if m != n:
                        beats = (g[m] > g[n]) | (g[m] == g[n]) if m < n else (g[m] > g[n])
                        rank = rank + jnp.where(beats, 1.0, 0.0)
                sel.append(jnp.where(rank < MOBA_TOPK, 1.0, 0.0))
        logits = []
        for n in range(qb + 1):
            s = lax.dot_general(k_ref[n * L:(n + 1) * L, :], q, NT_DIMS, preferred_element_type=F32)
            if n == qb:
                logits.append(jnp.where(qry_i >= key_i, s * scale + bias_scr[L:2 * L, :], neg_inf))
            else:
                lg = s * scale + (bias_scr[0:L, :] if n == qb - 1 else far)
                logits.append(lg if sel is None else jnp.where(sel[n] > 0.5, lg, neg_inf))
        mx = logits[0].max(axis=0, keepdims=True)
        for lg in logits[1:]:
            mx = jnp.maximum(mx, lg.max(axis=0, keepdims=True))
        denom = jnp.zeros((1, L), F32)
        acc = jnp.zeros((HEAD_DIM, L), F32)
        for n, lg in enumerate(logits):
            p = jnp.exp2(lg - mx)
            denom = denom + p.sum(axis=0, keepdims=True)
            acc = acc + jnp.dot(v_t[:, n * L:(n + 1) * L], p.astype(BF16), preferred_element_type=F32)
        o_ref[qb * L:(qb + 1) * L, :] = (acc / denom).T.astype(o_ref.dtype)


def moba_attention(proj, rel_bias, batch, seq, q_col, k_col, v_col, side_tables=()):
    L = MOBA_BLOCK
    n_blocks = seq // L
    bucket = jnp.asarray(_moba_bucket_tile())
    n_side = len(side_tables)
    n_steps = ATTN_HEADS * batch
    steps_per_table = n_steps // max(n_side, 1)
    side_specs, side_shapes = [], []
    for t, tab in enumerate(side_tables):
        rows, cols = tab.shape
        assert rows % steps_per_table == 0
        def index_map(h, b, t=t):
            return (jnp.clip(h * batch + b - t * steps_per_table, 0, steps_per_table - 1), 0)
        side_specs.append(pl.BlockSpec((rows // steps_per_table, cols), index_map))
        side_shapes.append(jax.ShapeDtypeStruct(tab.shape, BF16))
    kern = functools.partial(_moba_kernel, n_blocks=n_blocks, n_side=n_side, steps_per_table=steps_per_table)
    outs = pl.pallas_call(
        kern,
        grid=(ATTN_HEADS, batch),
        in_specs=[
            pl.BlockSpec(memory_space=pltpu.SMEM),
            pl.BlockSpec((seq, HEAD_DIM), lambda h, b: (b, q_col + h)),
            pl.BlockSpec((seq, HEAD_DIM), lambda h, b: (b, k_col + h)),
            pl.BlockSpec((seq, HEAD_DIM), lambda h, b: (b, v_col + h)),
            pl.BlockSpec((2 * L, L), lambda h, b: (0, 0)),
        ] + side_specs,
        out_specs=[pl.BlockSpec((seq, HEAD_DIM), lambda h, b: (b, h))] + side_specs,
        out_shape=[jax.ShapeDtypeStruct((batch * seq, ATTN_WIDTH), BF16)] + side_shapes,
        scratch_shapes=[pltpu.VMEM((2 * L, L), F32)],
        compiler_params=_params(("arbitrary", "arbitrary")), name="moba",
    )(rel_bias, proj, proj, proj, bucket, *side_tables)
    return outs[0], tuple(outs[1:])


def _conv_kernel(cb_ref, cc_ref, cu_ref, w_ref, b_ref, o_ref):
    g = cc_ref[...].astype(F32) * cu_ref[...].astype(F32)
    row = lax.broadcasted_iota(jnp.int32, g.shape, 0)
    y = w_ref[CONV_K - 1:CONV_K, :] * g
    for s in range(1, CONV_K):
        shifted = jnp.where(row >= s, pltpu.roll(g, s, axis=0), 0.0)
        y = y + w_ref[CONV_K - 1 - s:CONV_K - s, :] * shifted
    o_ref[...] = (cb_ref[...].astype(F32) * (y + b_ref[...])).astype(o_ref.dtype)


def short_conv(proj, conv_w, conv_b, batch, seq, cb_col, cc_col, cu_col, cw=512):
    width = conv_w.shape[1]
    return pl.pallas_call(
        _conv_kernel,
        grid=(batch, width // cw),
        in_specs=[
            pl.BlockSpec((seq, cw), lambda b, c: (b, cb_col + c)),
            pl.BlockSpec((seq, cw), lambda b, c: (b, cc_col + c)),
            pl.BlockSpec((seq, cw), lambda b, c: (b, cu_col + c)),
            pl.BlockSpec((CONV_K, cw), lambda b, c: (0, c)),
            pl.BlockSpec((1, cw), lambda b, c: (0, c)),
        ],
        out_specs=pl.BlockSpec((seq, cw), lambda b, c: (b, c)),
        out_shape=jax.ShapeDtypeStruct((batch * seq, width), BF16),
        compiler_params=_params(("parallel", "parallel")), name="short_conv",
    )(proj, proj, proj, conv_w, conv_b)


def _merge_kernel(attn_ref, conv_ref, wa_ref, wc_ref, ga_ref, gc_ref, bg_ref, o_ref, wa_scr, wc_scr,
                  *, n_col_blocks):
    jj = pl.program_id(0)
    i = pl.program_id(1)

    def round_chunks():
        for w_ref, w_scr in ((wa_ref, wa_scr), (wc_ref, wc_scr)):
            kc = w_ref.shape[0]
            w_scr[jj % 2, pl.ds(pl.multiple_of(i * kc, kc), kc), :] = w_ref[...].astype(BF16)

    def multiply():
        za = jnp.dot(attn_ref[...], wa_scr[(jj + 1) % 2], preferred_element_type=F32)
        zc = jnp.dot(conv_ref[...], wc_scr[(jj + 1) % 2], preferred_element_type=F32)
        ga = jax.nn.sigmoid(ga_ref[...].astype(F32) + bg_ref[0:1, :])
        gc = jax.nn.sigmoid(gc_ref[...].astype(F32) + bg_ref[1:2, :])
        o_ref[...] = (ga * za + gc * zc).astype(o_ref.dtype)

    pl.when(jj == 0)(round_chunks)
    pl.when(jj == n_col_blocks)(multiply)

    @pl.when((jj > 0) & (jj < n_col_blocks))
    def _():
        multiply()
        round_chunks()


def branch_merge(attn, conv, wa, wc, proj, b_gate, ga_col, gc_col, tm=1024, tn=1024):
    m, ka = attn.shape
    kc = conv.shape[1]
    n = wa.shape[1]
    tm, tn = min(tm, m), min(tn, n)
    ni, nj = m // tm, n // tn
    assert ka % ni == 0 and kc % ni == 0

    def rows(jj, i):
        return jnp.where(jj == 0, 0, i)

    def col(jj):
        return jnp.maximum(jj - 1, 0)

    def weight_index(jj, i):
        return (jnp.where(jj < nj, i, ni - 1), jnp.minimum(jj, nj - 1))

    kern = functools.partial(_merge_kernel, n_col_blocks=nj)
    return pl.pallas_call(
        kern,
        grid=(nj + 1, ni),
        in_specs=[
            pl.BlockSpec((tm, ka), lambda jj, i: (rows(jj, i), 0)),
            pl.BlockSpec((tm, kc), lambda jj, i: (rows(jj, i), 0)),
            pl.BlockSpec((ka // ni, tn), weight_index),
            pl.BlockSpec((kc // ni, tn), weight_index),
            pl.BlockSpec((tm, tn), lambda jj, i: (rows(jj, i), ga_col + col(jj))),
            pl.BlockSpec((tm, tn), lambda jj, i: (rows(jj, i), gc_col + col(jj))),
            pl.BlockSpec((2, tn), lambda jj, i: (0, col(jj))),
        ],
        out_specs=pl.BlockSpec((tm, tn), lambda jj, i: (rows(jj, i), col(jj))),
        out_shape=jax.ShapeDtypeStruct((m, n), BF16),
        scratch_shapes=[pltpu.VMEM((2, ka, tn), BF16), pltpu.VMEM((2, kc, tn), BF16)],
        compiler_params=_params(("arbitrary", "arbitrary")), name="branch_merge",
    )(attn, conv, wa, wc, proj, proj, b_gate)


def _topk_rows(s, k, vals_ref, rows_ref):
    nrows = s.shape[0]
    rowi = lax.broadcasted_iota(jnp.int32, s.shape, 0).astype(F32)
    for r in range(k):
        m = s.max(axis=0, keepdims=True)
        idx = jnp.where(s == m, rowi, float(nrows)).min(axis=0, keepdims=True)
        s = jnp.where(rowi == idx, -jnp.inf, s)
        vals_ref[r:r + 1, :] = m
        rows_ref[r:r + 1, :] = idx


def _topk_rows_untied(s, k, vals_ref, rows_ref):
    nrows, ncols = s.shape
    probe_rows = 2 * SUBLANES
    lane = lax.broadcasted_iota(jnp.int32, (probe_rows, nrows), 1).astype(F32)
    sub = lax.broadcasted_iota(jnp.int32, (probe_rows, nrows), 0)
    probe = jnp.where(sub == 0, lane, jnp.where(sub == 1, 1.0, 0.0)).astype(BF16)
    worst = jnp.zeros((1, ncols), F32)
    for r in range(k):
        m = s.max(axis=0, keepdims=True)
        hit = s == m
        s = jnp.where(hit, -jnp.inf, s)
        stats = jnp.dot(probe, jnp.where(hit, 1.0, 0.0).astype(BF16), preferred_element_type=F32)
        vals_ref[r:r + 1, :] = m
        rows_ref[r:r + 1, :] = stats[0:1]
        worst = jnp.maximum(worst, stats[1:2])
    return worst


def _pair_candidates(v1, v2):
    K = PEER_TOPK
    ncols = v1.shape[1]
    sub8 = lax.broadcasted_iota(jnp.int32, (SUBLANES, ncols), 0)
    sub16 = lax.broadcasted_iota(jnp.int32, (K, ncols), 0)
    big = float(K * K)
    vals = [v1[0:1] + v2]
    flat = [sub16.astype(F32)]
    for a in range(1, SUBLANES):
        ok = sub8 < K // (a + 1)
        vals.append(jnp.where(ok, v1[a:a + 1] + v2[0:SUBLANES], -jnp.inf))
        flat.append(jnp.where(ok, (sub8 + a * K).astype(F32), big))
    vals.append(v1[SUBLANES:K] + v2[0:1])
    flat.append(((sub8 + SUBLANES) * K).astype(F32))
    return jnp.concatenate(vals, axis=0), jnp.concatenate(flat, axis=0)


def _select_rows(table, which):
    out = jnp.zeros(which.shape, F32)
    for a in range(table.shape[0]):
        out = jnp.where(which == float(a), table[a:a + 1], out)
    return out


def _peer_route_kernel(q_ref, keys_ref, code_ref, gate_ref, s1_scr, s2_scr, code_scr, gate_scr, topk_scr,
                       *, chunk_tokens):
    K = PEER_TOPK
    h = pl.program_id(1)
    q = q_ref[...]
    s1_scr[...] = lax.dot_general(keys_ref[0, 0], q[:, :PEER_HALF], NT_DIMS,
                                  precision=lax.Precision.HIGHEST, preferred_element_type=F32)
    s2_scr[...] = lax.dot_general(keys_ref[0, 1], q[:, PEER_HALF:], NT_DIMS,
                                  precision=lax.Precision.HIGHEST, preferred_element_type=F32)
    row0 = pl.multiple_of(h * K, K)

    def chunk(c, carry):
        off = pl.multiple_of(c * chunk_tokens, chunk_tokens)
        cols = pl.ds(off, chunk_tokens)
        t1 = _topk_rows_untied(s1_scr[:, cols], K, topk_scr.at[0], topk_scr.at[1])
        t2 = _topk_rows_untied(s2_scr[:, cols], K, topk_scr.at[2], topk_scr.at[3])

        @pl.when(jnp.max(jnp.maximum(t1, t2)) > 1.5)
        def _():
            _topk_rows(s1_scr[:, cols], K, topk_scr.at[0], topk_scr.at[1])
            _topk_rows(s2_scr[:, cols], K, topk_scr.at[2], topk_scr.at[3])

        cand, flat = _pair_candidates(topk_scr[0], topk_scr[2])
        for r in range(K):
            m = cand.max(axis=0, keepdims=True)
            first = jnp.where(cand == m, flat, float(K * K)).min(axis=0, keepdims=True)
            cand = jnp.where(flat == first, -jnp.inf, cand)
            topk_scr[4, r:r + 1, :] = m
            topk_scr[5, r:r + 1, :] = first
        score = topk_scr[4]
        e = jnp.exp(score - score[0:1])
        pair = topk_scr[5]
        a = jnp.floor(pair * (1.0 / K))
        e1 = _select_rows(topk_scr[1], a)
        e2 = _select_rows(topk_scr[3], pair - a * K)
        code_scr[pl.ds(row0, K), pl.ds(off, chunk_tokens)] = e1 * float(PEER_N_KEYS) + e2
        gate_scr[pl.ds(row0, K), pl.ds(off, chunk_tokens)] = e / e.sum(axis=0, keepdims=True)
        return carry

    lax.fori_loop(0, q.shape[0] // chunk_tokens, chunk, 0)

    @pl.when(h == PEER_HEADS - 1)
    def _():
        code_ref[...] = code_scr[...].T.astype(jnp.int32)
        gate_ref[...] = gate_scr[...].T


def peer_route(q, sub_keys, tt=1024, chunk_tokens=1024):
    n = q.shape[0]
    tt = min(tt, n)
    chunk_tokens = min(chunk_tokens, tt)
    qd = 2 * PEER_HALF
    picks = PEER_HEADS * PEER_TOPK
    out_spec = pl.BlockSpec((tt, picks), lambda i, h: (i, 0))
    return pl.pallas_call(
        functools.partial(_peer_route_kernel, chunk_tokens=chunk_tokens),
        grid=(n // tt, PEER_HEADS),
        in_specs=[
            pl.BlockSpec((tt, qd), lambda i, h: (i, h)),
            pl.BlockSpec((1, 2, PEER_N_KEYS, PEER_HALF), lambda i, h: (h, 0, 0, 0)),
        ],
        out_specs=[out_spec, out_spec],
        out_shape=[jax.ShapeDtypeStruct((n, picks), jnp.int32),
                   jax.ShapeDtypeStruct((n, picks), F32)],
        scratch_shapes=[pltpu.VMEM((PEER_N_KEYS, tt), F32), pltpu.VMEM((PEER_N_KEYS, tt), F32),
                        pltpu.VMEM((picks, tt), F32), pltpu.VMEM((picks, tt), F32),
                        pltpu.VMEM((6, PEER_TOPK, chunk_tokens), F32)],
        compiler_params=_params(("parallel", "arbitrary")), name="peer_route",
    )(q, sub_keys)


W_ROW_PITCH = PEER_N_KEYS + SUBLANES


def _peer_weights_kernel(code_ref, gate_ref, o_ref, w_scr, *, unroll):
    nk = PEER_N_KEYS
    tt, picks = code_ref.shape
    sub = lax.broadcasted_iota(jnp.int32, (nk, picks), 0)
    zero = jnp.zeros((nk, picks), BF16)

    def one_hots(t):
        c = code_ref[pl.ds(t, 1), :]
        g = gate_ref[pl.ds(t, 1), :]
        e1 = lax.shift_right_logical(c, 7)
        e2 = lax.bitwise_and(c, nk - 1)
        at = jnp.where(sub == e1, g, 0.0).astype(BF16)
        bt = jnp.where(sub == e2, 1.0, 0.0).astype(BF16)
        return at, bt

    def body(p, carry):
        t = 2 * p
        at0, bt0 = one_hots(t)
        at1, bt1 = one_hots(t + 1)
        at = jnp.concatenate([at0, at1], axis=1)
        bt = jnp.concatenate([jnp.concatenate([bt0, zero], axis=1),
                              jnp.concatenate([zero, bt1], axis=1)], axis=0)
        w = lax.dot_general(at, bt, NT_DIMS, preferred_element_type=F32)
        row = pl.multiple_of(t * W_ROW_PITCH, SUBLANES)
        w_scr[pl.ds(row, nk), :] = w[:, :nk]
        w_scr[pl.ds(row + W_ROW_PITCH, nk), :] = w[:, nk:]
        return carry

    lax.fori_loop(0, tt // 2, body, 0, unroll=unroll)
    for e1 in range(nk):
        o_ref[:, e1 * nk:(e1 + 1) * nk] = w_scr[pl.ds(e1, tt, stride=W_ROW_PITCH), :].astype(o_ref.dtype)


def peer_dense_weights(code, gate, tt=128, unroll=64):
    n, picks = code.shape
    tt = min(tt, n)
    nk = PEER_N_KEYS
    kern = functools.partial(_peer_weights_kernel, unroll=unroll)
    return pl.pallas_call(
        kern,
        grid=(n // tt,),
        in_specs=[pl.BlockSpec((tt, picks), lambda i: (i, 0)),
                  pl.BlockSpec((tt, picks), lambda i: (i, 0))],
        out_specs=pl.BlockSpec((tt, nk * nk), lambda i: (i, 0)),
        out_shape=jax.ShapeDtypeStruct((n, nk * nk), BF16),
        scratch_shapes=[pltpu.VMEM((tt * W_ROW_PITCH, nk), F32)],
        compiler_params=_params(("parallel",)), name="peer_weights",
    )(code, gate)


def _row_rms_scale(t, g):
    return t * lax.rsqrt(jnp.mean(t * t, axis=-1, keepdims=True) + NORM_EPS) * g


def _peer_dense_kernel(x_ref, u_ref, v_ref, w_ref, h_ref, gout_ref, o_ref, acc_scr, *, n_edge, n_main):
    j = pl.program_id(1)
    edge_rows = h_ref.shape[0]

    @pl.when(j == 0)
    def _():
        acc_scr[...] = jnp.zeros(acc_scr.shape, F32)

    @pl.when(j < n_main)
    def _():
        a = lax.dot_general(x_ref[...], u_ref[...], NT_DIMS, preferred_element_type=F32)
        act = 0.5 * a * (1.0 + lax.erf(a * math.sqrt(0.5))) * w_ref[...].astype(F32)
        acc_scr[...] += jnp.dot(act.astype(BF16), v_ref[...], preferred_element_type=F32)

    @pl.when(j >= n_main)
    def _():
        r = j - n_main
        rows = pl.ds(pl.multiple_of(r * edge_rows, edge_rows), edge_rows)
        o_ref[...] = _row_rms_scale(h_ref[...] + acc_scr[rows, :], gout_ref[...])


def peer_dense_final(x, u, v, w, h, g_out, tn=1024, te=512, edge_rows=128):
    n, d = h.shape
    ne = u.shape[0]
    tn, te = min(tn, n), min(te, ne)
    edge_rows = min(edge_rows, tn)
    n_edge, n_main = tn // edge_rows, ne // te

    def main_step(j):
        return jnp.minimum(j, n_main - 1)

    def edge_chunk(i, j):
        return (i * n_edge + jnp.maximum(j - n_main, 0), 0)

    kern = functools.partial(_peer_dense_kernel, n_edge=n_edge, n_main=n_main)
    return pl.pallas_call(
        kern,
        grid=(n // tn, n_main + n_edge),
        in_specs=[
            pl.BlockSpec((tn, d), lambda i, j: (i, 0), pipeline_mode=pl.Buffered(1)),
            pl.BlockSpec((te, d), lambda i, j: (main_step(j), 0)),
            pl.BlockSpec((te, d), lambda i, j: (main_step(j), 0)),
            pl.BlockSpec((tn, te), lambda i, j: (i, main_step(j))),
            pl.BlockSpec((edge_rows, d), edge_chunk),
            pl.BlockSpec((1, d), lambda i, j: (0, 0)),
        ],
        out_specs=pl.BlockSpec((edge_rows, d), edge_chunk),
        out_shape=jax.ShapeDtypeStruct((n, d), F32),
        scratch_shapes=[pltpu.VMEM((tn, d), F32)],
        compiler_params=_params(("parallel", "arbitrary")), name="peer_dense",
    )(x, u, v, w, h, g_out.reshape(1, d))


def kernel(x, norm_mix, w_in, conv_w, conv_b, w_br_attn, w_br_conv, b_gate, rel_bias,
           w_out, norm_ffn, peer_w_q, peer_sub_keys, peer_u, peer_v, norm_final):
    batch, seq, d = x.shape
    n = batch * seq
    assert norm_mix.shape[0] == 1, "single-layer problem"
    h = x.reshape(n, d)

    hn = rmsnorm(h, norm_mix[0], BF16)
    proj, wq_bf16 = matmul(hn, w_in[0], BF16, side_table=peer_w_q[0], name="in_proj")
    attn, (u_bf16, v_bf16) = moba_attention(
        proj, rel_bias, batch, seq, q_col=0, k_col=ATTN_WIDTH // HEAD_DIM, v_col=2 * ATTN_WIDTH // HEAD_DIM,
        side_tables=(peer_u[0], peer_v[0]))
    cw = 512
    c0 = 3 * ATTN_WIDTH // cw
    conv = short_conv(proj, conv_w[0].reshape(CONV_K, CONV_WIDTH), conv_b[0].reshape(1, CONV_WIDTH),
                      batch, seq, cb_col=c0, cc_col=c0 + CONV_WIDTH // cw, cu_col=c0 + 2 * CONV_WIDTH // cw)
    tn = 1024
    g0 = (3 * ATTN_WIDTH + 3 * CONV_WIDTH) // tn
    merged = branch_merge(attn, conv, w_br_attn[0], w_br_conv[0],
                          proj, b_gate[0], ga_col=g0, gc_col=g0 + d // tn, tn=tn)
    h = matmul(merged, w_out[0], F32, residual=h, tm=1024, tn=1024, name="out_proj")

    q, hn2 = norm_matmul(h, norm_ffn[0], wq_bf16, F32, name="peer_query")
    code, gate = peer_route(q, peer_sub_keys[0])
    w = peer_dense_weights(code, gate)
    out = peer_dense_final(hn2, u_bf16, v_bf16, w, h, norm_final)
    return out.reshape(batch, seq, d)
```

```python
import functools
import math

import numpy as np
import jax
import jax.numpy as jnp
from jax import lax
from jax.experimental import pallas as pl
from jax.experimental.pallas import tpu as pltpu

F32 = jnp.float32
BF16 = jnp.bfloat16

D_MODEL = 4096
ATTN_HEADS = 16
HEAD_DIM = 128
ATTN_WIDTH = ATTN_HEADS * HEAD_DIM
MOBA_BLOCK = 256
MOBA_TOPK = 3
REL_BUCKETS = 32
REL_MAX_DIST = 128
CONV_WIDTH = 2048
CONV_K = 3
PEER_HEADS = 8
PEER_N_KEYS = 128
PEER_HALF = 128
PEER_TOPK = 16
NORM_EPS = 1e-6

V7X_VMEM_BYTES = 64 * 1024 * 1024
VMEM_LIMIT = V7X_VMEM_BYTES - 2 * 1024 * 1024
LANES = 128
SUBLANES = 8
NT_DIMS = (((1,), (1,)), ((), ()))
LOG2_E = math.log2(math.e)


def _params(semantics):
    return pltpu.CompilerParams(dimension_semantics=semantics, vmem_limit_bytes=VMEM_LIMIT)


def _rmsnorm_kernel(x_ref, g_ref, o_ref):
    x = x_ref[...]
    ms = jnp.mean(x * x, axis=-1, keepdims=True)
    o_ref[...] = (x * lax.rsqrt(ms + NORM_EPS) * g_ref[...]).astype(o_ref.dtype)


def rmsnorm(x, g, out_dtype, rows=256):
    n, d = x.shape
    rows = min(rows, n)
    row_spec = pl.BlockSpec((rows, d), lambda i: (i, 0))
    return pl.pallas_call(
        _rmsnorm_kernel, grid=(n // rows,),
        in_specs=[row_spec, pl.BlockSpec((1, d), lambda i: (0, 0))], out_specs=row_spec,
        out_shape=jax.ShapeDtypeStruct((n, d), out_dtype),
        compiler_params=_params(("parallel",)), name="rmsnorm",
    )(x, g.reshape(1, d))


def _matmul_kernel(a_ref, b_ref, *rest, n_col_blocks, has_residual, n_side_blocks):
    rest = list(rest)
    b_scr = rest.pop()
    r_ref = rest.pop(0) if has_residual else None
    side_in = rest.pop(0) if n_side_blocks else None
    o_ref = rest.pop(0)
    jj = pl.program_id(0)
    i = pl.program_id(1)
    kc = b_ref.shape[0]

    if n_side_blocks:
        @pl.when(jj * pl.num_programs(1) + i < n_side_blocks)
        def _():
            rest[0][...] = side_in[...].astype(BF16)

    def round_chunk():
        rows = pl.ds(pl.multiple_of(i * kc, kc), kc)
        b_scr[jj % 2, rows, :] = b_ref[...].astype(BF16)

    def multiply():
        acc = jnp.dot(a_ref[...], b_scr[(jj + 1) % 2], preferred_element_type=F32)
        if has_residual:
            acc = r_ref[...] + acc
        o_ref[...] = acc.astype(o_ref.dtype)

    pl.when(jj == 0)(round_chunk)
    pl.when(jj == n_col_blocks)(multiply)

    @pl.when((jj > 0) & (jj < n_col_blocks))
    def _():
        multiply()
        round_chunk()


SIDE_BLOCK_ROWS = 32


def matmul(a, b, out_dtype, residual=None, side_table=None, tm=1024, tn=1024, name="matmul"):
    m, k = a.shape
    _, n = b.shape
    tm, tn = min(tm, m), min(tn, n)
    ni, nj = m // tm, n // tn
    assert k % ni == 0
    kc = k // ni

    def out_index(jj, i):
        return (jnp.where(jj == 0, 0, i), jnp.maximum(jj - 1, 0))

    a_spec = pl.BlockSpec((tm, k), lambda jj, i: (jnp.where(jj == 0, 0, i), 0))
    b_spec = pl.BlockSpec((kc, tn), lambda jj, i: (jnp.where(jj < nj, i, ni - 1), jnp.minimum(jj, nj - 1)))
    o_spec = pl.BlockSpec((tm, tn), out_index)
    ins, specs = (a, b), [a_spec, b_spec]
    out_specs, out_shape = o_spec, jax.ShapeDtypeStruct((m, n), out_dtype)
    if residual is not None:
        ins, specs = ins + (residual,), specs + [o_spec]
    n_side_blocks = 0
    if side_table is not None:
        rows, cols = side_table.shape
        n_side_blocks = rows // SIDE_BLOCK_ROWS
        assert rows % SIDE_BLOCK_ROWS == 0 and n_side_blocks <= (nj + 1) * ni
        side_spec = pl.BlockSpec((SIDE_BLOCK_ROWS, cols),
                                 lambda jj, i: (jnp.minimum(jj * ni + i, n_side_blocks - 1), 0))
        ins, specs = ins + (side_table,), specs + [side_spec]
        out_specs = [o_spec, side_spec]
        out_shape = [out_shape, jax.ShapeDtypeStruct(side_table.shape, BF16)]
    kern = functools.partial(_matmul_kernel, n_col_blocks=nj, has_residual=residual is not None,
                             n_side_blocks=n_side_blocks)
    return pl.pallas_call(
        kern, grid=(nj + 1, ni), in_specs=specs, out_specs=out_specs, out_shape=out_shape,
        scratch_shapes=[pltpu.VMEM((2, k, tn), BF16)],
        compiler_params=_params(("arbitrary", "arbitrary")), name=name,
    )(*ins)


NORM_CHUNK_ROWS = 64


def _rmsnorm_chunks(load_rows, g_ref, dst_ref, unrolled=False):
    def body(c, carry):
        start = c * NORM_CHUNK_ROWS
        rows = pl.ds(start if unrolled else pl.multiple_of(start, NORM_CHUNK_ROWS), NORM_CHUNK_ROWS)
        x = load_rows(rows)
        ms = jnp.mean(x * x, axis=-1, keepdims=True)
        dst_ref[rows, :] = (x * lax.rsqrt(ms + NORM_EPS) * g_ref[...]).astype(dst_ref.dtype)
        return carry

    n_chunks = dst_ref.shape[0] // NORM_CHUNK_ROWS
    if unrolled:
        for c in range(n_chunks):
            body(c, 0)
    else:
        lax.fori_loop(0, n_chunks, body, 0)


def _norm_matmul_kernel(x_ref, g_ref, b_ref, o_ref, hn_ref, hn_scr, *, n_row_tiles):
    s = pl.program_id(0)

    def normalise():
        _rmsnorm_chunks(lambda rows: x_ref[rows, :], g_ref, hn_scr.at[s % 2], unrolled=True)
        hn_ref[...] = hn_scr[s % 2]

    def multiply():
        o_ref[...] = jnp.dot(hn_scr[(s + 1) % 2], b_ref[...], preferred_element_type=F32).astype(o_ref.dtype)

    @pl.when(s == 0)
    def _():
        normalise()

    @pl.when((s > 0) & (s < n_row_tiles))
    def _():
        multiply()
        normalise()

    @pl.when(s == n_row_tiles)
    def _():
        multiply()


def norm_matmul(x, g, b, out_dtype, tm=256, name="norm_matmul"):
    m, k = x.shape
    _, n = b.shape
    tm = min(tm, m)
    ni = m // tm
    kern = functools.partial(_norm_matmul_kernel, n_row_tiles=ni)
    row_tile = lambda s: (jnp.minimum(s, ni - 1), 0)
    return pl.pallas_call(
        kern, grid=(ni + 1,),
        in_specs=[pl.BlockSpec((tm, k), row_tile),
                  pl.BlockSpec((1, k), lambda s: (0, 0)),
                  pl.BlockSpec((k, n), lambda s: (0, 0), pipeline_mode=pl.Buffered(1))],
        out_specs=[pl.BlockSpec((tm, n), lambda s: (jnp.maximum(s - 1, 0), 0)),
                   pl.BlockSpec((tm, k), row_tile)],
        out_shape=[jax.ShapeDtypeStruct((m, n), out_dtype), jax.ShapeDtypeStruct((m, k), BF16)],
        scratch_shapes=[pltpu.VMEM((2, tm, k), BF16)],
        compiler_params=_params(("arbitrary",)), name=name,
    )(x, g.reshape(1, k), b)


def _t5_bucket_table(max_dist):
    dist = np.arange(max_dist, dtype=np.int32)
    max_exact = REL_BUCKETS // 2
    d32 = np.maximum(dist, 1).astype(np.float32)
    ratio = (np.log(d32 / np.float32(max_exact)) / np.float32(math.log(REL_MAX_DIST / max_exact))
             * np.float32(REL_BUCKETS - max_exact))
    large = max_exact + ratio.astype(np.int32)
    large = np.minimum(large, REL_BUCKETS - 1)
    return np.where(dist < max_exact, dist, large).astype(np.int32)


def _moba_bucket_tile():
    L = MOBA_BLOCK
    key = np.arange(2 * L)[:, None]
    qi = np.arange(L)[None, :]
    dist = np.maximum(qi - key + L, 0)
    return _t5_bucket_table(2 * L)[dist]


def _moba_kernel(tab_ref, q_ref, k_ref, v_ref, bucket_ref, *rest, n_blocks, n_side, steps_per_table):
    side_in, o_ref, side_out, bias_scr = rest[:n_side], rest[n_side], rest[n_side + 1:-1], rest[-1]
    L = MOBA_BLOCK
    h = pl.program_id(0)
    b = pl.program_id(1)

    step = h * pl.num_programs(1) + b
    for t in range(n_side):
        @pl.when((step >= t * steps_per_table) & (step < (t + 1) * steps_per_table))
        def _(t=t):
            def body(c, carry):
                rows = pl.ds(pl.multiple_of(c * NORM_CHUNK_ROWS, NORM_CHUNK_ROWS), NORM_CHUNK_ROWS)
                side_out[t][rows, :] = side_in[t][rows, :].astype(BF16)
                return carry
            lax.fori_loop(0, side_in[t].shape[0] // NORM_CHUNK_ROWS, body, 0)

    @pl.when(b == 0)
    def _():
        bk = bucket_ref[...]
        acc = jnp.zeros(bk.shape, F32)
        for kk in range(REL_BUCKETS):
            acc = jnp.where(bk == kk, tab_ref[kk, h], acc)
        bias_scr[...] = acc * LOG2_E

    neg_inf = jnp.float32(-jnp.inf)
    far = tab_ref[REL_BUCKETS - 1, h] * LOG2_E
    scale = HEAD_DIM ** -0.5 * LOG2_E
    k_means = [jnp.mean(k_ref[n * L:(n + 1) * L, :].astype(F32), axis=0, keepdims=True)
               for n in range(n_blocks)]
    pad = jnp.zeros((SUBLANES - n_blocks % SUBLANES, HEAD_DIM), F32) if n_blocks % SUBLANES else None
    k_mean = jnp.concatenate(k_means + ([pad] if pad is not None else []), axis=0)
    v_t = v_ref[...].astype(F32).T.astype(BF16)
    key_i = lax.broadcasted_iota(jnp.int32, (L, L), 0)
    qry_i = lax.broadcasted_iota(jnp.int32, (L, L), 1)

    for qb in range(n_blocks):
        q = q_ref[qb * L:(qb + 1) * L, :]
        sel = None
        if qb > MOBA_TOPK:
            gate = lax.dot_general(k_mean, q.astype(F32), NT_DIMS,
                                   precision=lax.Precision.HIGHEST, preferred_element_type=F32)
            g = [gate[m:m + 1, :] for m in range(qb)]
            sel = []
            for n in range(qb):
                rank = jnp.zeros((1, L), F32)
                for m in range(qb):
                    if m != n:
                        beats = (g[m] > g[n]) | (g[m] == g[n]) if m < n else (g[m] > g[n])
                        rank = rank + jnp.where(beats, 1.0, 0.0)
                sel.append(jnp.where(rank < MOBA_TOPK, 1.0, 0.0))
        logits = []
        for n in range(qb + 1):
            s = lax.dot_general(k_ref[n * L:(n + 1) * L, :], q, NT_DIMS, preferred_element_type=F32)
            if n == qb:
                logits.append(jnp.where(qry_i >= key_i, s * scale + bias_scr[L:2 * L, :], neg_inf))
            else:
                lg = s * scale + (bias_scr[0:L, :] if n == qb - 1 else far)
                logits.append(lg if sel is None else jnp.where(sel[n] > 0.5, lg, neg_inf))
        mx = logits[0].max(axis=0, keepdims=True)
        for lg in logits[1:]:
            mx = jnp.maximum(mx, lg.max(axis=0, keepdims=True))
        denom = jnp.zeros((1, L), F32)
        acc = jnp.zeros((HEAD_DIM, L), F32)
        for n, lg in enumerate(logits):
            p = jnp.exp2(lg - mx)
            denom = denom + p.sum(axis=0, keepdims=True)
            acc = acc + jnp.dot(v_t[:, n * L:(n + 1) * L], p.astype(BF16), preferred_element_type=F32)
        o_ref[qb * L:(qb + 1) * L, :] = (acc / denom).T.astype(o_ref.dtype)


def moba_attention(proj, rel_bias, batch, seq, q_col, k_col, v_col, side_tables=()):
    L = MOBA_BLOCK
    n_blocks = seq // L
    bucket = jnp.asarray(_moba_bucket_tile())
    n_side = len(side_tables)
    n_steps = ATTN_HEADS * batch
    steps_per_table = n_steps // max(n_side, 1)
    side_specs, side_shapes = [], []
    for t, tab in enumerate(side_tables):
        rows, cols = tab.shape
        assert rows % steps_per_table == 0
        def index_map(h, b, t=t):
            return (jnp.clip(h * batch + b - t * steps_per_table, 0, steps_per_table - 1), 0)
        side_specs.append(pl.BlockSpec((rows // steps_per_table, cols), index_map))
        side_shapes.append(jax.ShapeDtypeStruct(tab.shape, BF16))
    kern = functools.partial(_moba_kernel, n_blocks=n_blocks, n_side=n_side, steps_per_table=steps_per_table)
    outs = pl.pallas_call(
        kern,
        grid=(ATTN_HEADS, batch),
        in_specs=[
            pl.BlockSpec(memory_space=pltpu.SMEM),
            pl.BlockSpec((seq, HEAD_DIM), lambda h, b: (b, q_col + h)),
            pl.BlockSpec((seq, HEAD_DIM), lambda h, b: (b, k_col + h)),
            pl.BlockSpec((seq, HEAD_DIM), lambda h, b: (b, v_col + h)),
            pl.BlockSpec((2 * L, L), lambda h, b: (0, 0)),
        ] + side_specs,
        out_specs=[pl.BlockSpec((seq, HEAD_DIM), lambda h, b: (b, h))] + side_specs,
        out_shape=[jax.ShapeDtypeStruct((batch * seq, ATTN_WIDTH), BF16)] + side_shapes,
        scratch_shapes=[pltpu.VMEM((2 * L, L), F32)],
        compiler_params=_params(("arbitrary", "arbitrary")), name="moba",
    )(rel_bias, proj, proj, proj, bucket, *side_tables)
    return outs[0], tuple(outs[1:])


def _conv_kernel(cb_ref, cc_ref, cu_ref, w_ref, b_ref, o_ref):
    g = cc_ref[...].astype(F32) * cu_ref[...].astype(F32)
    row = lax.broadcasted_iota(jnp.int32, g.shape, 0)
    y = w_ref[CONV_K - 1:CONV_K, :] * g
    for s in range(1, CONV_K):
        shifted = jnp.where(row >= s, pltpu.roll(g, s, axis=0), 0.0)
        y = y + w_ref[CONV_K - 1 - s:CONV_K - s, :] * shifted
    o_ref[...] = (cb_ref[...].astype(F32) * (y + b_ref[...])).astype(o_ref.dtype)


def short_conv(proj, conv_w, conv_b, batch, seq, cb_col, cc_col, cu_col, cw=512):
    width = conv_w.shape[1]
    return pl.pallas_call(
        _conv_kernel,
        grid=(batch, width // cw),
        in_specs=[
            pl.BlockSpec((seq, cw), lambda b, c: (b, cb_col + c)),
            pl.BlockSpec((seq, cw), lambda b, c: (b, cc_col + c)),
            pl.BlockSpec((seq, cw), lambda b, c: (b, cu_col + c)),
            pl.BlockSpec((CONV_K, cw), lambda b, c: (0, c)),
            pl.BlockSpec((1, cw), lambda b, c: (0, c)),
        ],
        out_specs=pl.BlockSpec((seq, cw), lambda b, c: (b, c)),
        out_shape=jax.ShapeDtypeStruct((batch * seq, width), BF16),
        compiler_params=_params(("parallel", "parallel")), name="short_conv",
    )(proj, proj, proj, conv_w, conv_b)


def _merge_kernel(attn_ref, conv_ref, wa_ref, wc_ref, ga_ref, gc_ref, bg_ref, o_ref, wa_scr, wc_scr,
                  *, n_col_blocks):
    jj = pl.program_id(0)
    i = pl.program_id(1)

    def round_chunks():
        for w_ref, w_scr in ((wa_ref, wa_scr), (wc_ref, wc_scr)):
            kc = w_ref.shape[0]
            w_scr[jj % 2, pl.ds(pl.multiple_of(i * kc, kc), kc), :] = w_ref[...].astype(BF16)

    def multiply():
        za = jnp.dot(attn_ref[...], wa_scr[(jj + 1) % 2], preferred_element_type=F32)
        zc = jnp.dot(conv_ref[...], wc_scr[(jj + 1) % 2], preferred_element_type=F32)
        ga = jax.nn.sigmoid(ga_ref[...].astype(F32) + bg_ref[0:1, :])
        gc = jax.nn.sigmoid(gc_ref[...].astype(F32) + bg_ref[1:2, :])
        o_ref[...] = (ga * za + gc * zc).astype(o_ref.dtype)

    pl.when(jj == 0)(round_chunks)
    pl.when(jj == n_col_blocks)(multiply)

    @pl.when((jj > 0) & (jj < n_col_blocks))
    def _():
        multiply()
        round_chunks()


def branch_merge(attn, conv, wa, wc, proj, b_gate, ga_col, gc_col, tm=1024, tn=1024):
    m, ka = attn.shape
    kc = conv.shape[1]
    n = wa.shape[1]
    tm, tn = min(tm, m), min(tn, n)
    ni, nj = m // tm, n // tn
    assert ka % ni == 0 and kc % ni == 0

    def rows(jj, i):
        return jnp.where(jj == 0, 0, i)

    def col(jj):
        return jnp.maximum(jj - 1, 0)

    def weight_index(jj, i):
        return (jnp.where(jj < nj, i, ni - 1), jnp.minimum(jj, nj - 1))

    kern = functools.partial(_merge_kernel, n_col_blocks=nj)
    return pl.pallas_call(
        kern,
        grid=(nj + 1, ni),
        in_specs=[
            pl.BlockSpec((tm, ka), lambda jj, i: (rows(jj, i), 0)),
            pl.BlockSpec((tm, kc), lambda jj, i: (rows(jj, i), 0)),
            pl.BlockSpec((ka // ni, tn), weight_index),
            pl.BlockSpec((kc // ni, tn), weight_index),
            pl.BlockSpec((tm, tn), lambda jj, i: (rows(jj, i), ga_col + col(jj))),
            pl.BlockSpec((tm, tn), lambda jj, i: (rows(jj, i), gc_col + col(jj))),
            pl.BlockSpec((2, tn), lambda jj, i: (0, col(jj))),
        ],
        out_specs=pl.BlockSpec((tm, tn), lambda jj, i: (rows(jj, i), col(jj))),
        out_shape=jax.ShapeDtypeStruct((m, n), BF16),
        scratch_shapes=[pltpu.VMEM((2, ka, tn), BF16), pltpu.VMEM((2, kc, tn), BF16)],
        compiler_params=_params(("arbitrary", "arbitrary")), name="branch_merge",
    )(attn, conv, wa, wc, proj, proj, b_gate)


def _topk_rows(s, k, vals_ref, rows_ref):
    nrows = s.shape[0]
    rowi = lax.broadcasted_iota(jnp.int32, s.shape, 0).astype(F32)
    for r in range(k):
        m = s.max(axis=0, keepdims=True)
        idx = jnp.where(s == m, rowi, float(nrows)).min(axis=0, keepdims=True)
        s = jnp.where(rowi == idx, -jnp.inf, s)
        vals_ref[r:r + 1, :] = m
        rows_ref[r:r + 1, :] = idx


def _topk_rows_untied(s, k, vals_ref, rows_ref):
    nrows, ncols = s.shape
    probe_rows = 2 * SUBLANES
    lane = lax.broadcasted_iota(jnp.int32, (probe_rows, nrows), 1).astype(F32)
    sub = lax.broadcasted_iota(jnp.int32, (probe_rows, nrows), 0)
    probe = jnp.where(sub == 0, lane, jnp.where(sub == 1, 1.0, 0.0)).astype(BF16)
    worst = jnp.zeros((1, ncols), F32)
    for r in range(k):
        m = s.max(axis=0, keepdims=True)
        hit = s == m
        s = jnp.where(hit, -jnp.inf, s)
        stats = jnp.dot(probe, jnp.where(hit, 1.0, 0.0).astype(BF16), preferred_element_type=F32)
        vals_ref[r:r + 1, :] = m
        rows_ref[r:r + 1, :] = stats[0:1]
        worst = jnp.maximum(worst, stats[1:2])
    return worst


def _pair_candidates(v1, v2):
    K = PEER_TOPK
    ncols = v1.shape[1]
    sub8 = lax.broadcasted_iota(jnp.int32, (SUBLANES, ncols), 0)
    sub16 = lax.broadcasted_iota(jnp.int32, (K, ncols), 0)
    big = float(K * K)
    vals = [v1[0:1] + v2]
    flat = [sub16.astype(F32)]
    for a in range(1, SUBLANES):
        ok = sub8 < K // (a + 1)
        vals.append(jnp.where(ok, v1[a:a + 1] + v2[0:SUBLANES], -jnp.inf))
        flat.append(jnp.where(ok, (sub8 + a * K).astype(F32), big))
    vals.append(v1[SUBLANES:K] + v2[0:1])
    flat.append(((sub8 + SUBLANES) * K).astype(F32))
    return jnp.concatenate(vals, axis=0), jnp.concatenate(flat, axis=0)


def _select_rows(table, which):
    out = jnp.zeros(which.shape, F32)
    for a in range(table.shape[0]):
        out = jnp.where(which == float(a), table[a:a + 1], out)
    return out


def _peer_route_kernel(q_ref, keys_ref, code_ref, gate_ref, s1_scr, s2_scr, code_scr, gate_scr, topk_scr,
                       *, chunk_tokens):
    K = PEER_TOPK
    h = pl.program_id(1)
    q = q_ref[...]
    s1_scr[...] = lax.dot_general(keys_ref[0, 0], q[:, :PEER_HALF], NT_DIMS,
                                  precision=lax.Precision.HIGHEST, preferred_element_type=F32)
    s2_scr[...] = lax.dot_general(keys_ref[0, 1], q[:, PEER_HALF:], NT_DIMS,
                                  precision=lax.Precision.HIGHEST, preferred_element_type=F32)
    row0 = pl.multiple_of(h * K, K)

    def chunk(c, carry):
        off = pl.multiple_of(c * chunk_tokens, chunk_tokens)
        cols = pl.ds(off, chunk_tokens)
        t1 = _topk_rows_untied(s1_scr[:, cols], K, topk_scr.at[0], topk_scr.at[1])
        t2 = _topk_rows_untied(s2_scr[:, cols], K, topk_scr.at[2], topk_scr.at[3])

        @pl.when(jnp.max(jnp.maximum(t1, t2)) > 1.5)
        def _():
            _topk_rows(s1_scr[:, cols], K, topk_scr.at[0], topk_scr.at[1])
            _topk_rows(s2_scr[:, cols], K, topk_scr.at[2], topk_scr.at[3])

        cand, flat = _pair_candidates(topk_scr[0], topk_scr[2])
        for r in range(K):
            m = cand.max(axis=0, keepdims=True)
            first = jnp.where(cand == m, flat, float(K * K)).min(axis=0, keepdims=True)
            cand = jnp.where(flat == first, -jnp.inf, cand)
            topk_scr[4, r:r + 1, :] = m
            topk_scr[5, r:r + 1, :] = first
        score = topk_scr[4]
        e = jnp.exp(score - score[0:1])
        pair = topk_scr[5]
        a = jnp.floor(pair * (1.0 / K))
        e1 = _select_rows(topk_scr[1], a)
        e2 = _select_rows(topk_scr[3], pair - a * K)
        code_scr[pl.ds(row0, K), pl.ds(off, chunk_tokens)] = e1 * float(PEER_N_KEYS) + e2
        gate_scr[pl.ds(row0, K), pl.ds(off, chunk_tokens)] = e / e.sum(axis=0, keepdims=True)
        return carry

    lax.fori_loop(0, q.shape[0] // chunk_tokens, chunk, 0)

    @pl.when(h == PEER_HEADS - 1)
    def _():
        code_ref[...] = code_scr[...].T.astype(jnp.int32)
        gate_ref[...] = gate_scr[...].T


def peer_route(q, sub_keys, tt=1024, chunk_tokens=1024):
    n = q.shape[0]
    tt = min(tt, n)
    chunk_tokens = min(chunk_tokens, tt)
    qd = 2 * PEER_HALF
    picks = PEER_HEADS * PEER_TOPK
    out_spec = pl.BlockSpec((tt, picks), lambda i, h: (i, 0))
    return pl.pallas_call(
        functools.partial(_peer_route_kernel, chunk_tokens=chunk_tokens),
        grid=(n // tt, PEER_HEADS),
        in_specs=[
            pl.BlockSpec((tt, qd), lambda i, h: (i, h)),
            pl.BlockSpec((1, 2, PEER_N_KEYS, PEER_HALF), lambda i, h: (h, 0, 0, 0)),
        ],
        out_specs=[out_spec, out_spec],
        out_shape=[jax.ShapeDtypeStruct((n, picks), jnp.int32),
                   jax.ShapeDtypeStruct((n, picks), F32)],
        scratch_shapes=[pltpu.VMEM((PEER_N_KEYS, tt), F32), pltpu.VMEM((PEER_N_KEYS, tt), F32),
                        pltpu.VMEM((picks, tt), F32), pltpu.VMEM((picks, tt), F32),
                        pltpu.VMEM((6, PEER_TOPK, chunk_tokens), F32)],
        compiler_params=_params(("parallel", "arbitrary")), name="peer_route",
    )(q, sub_keys)


W_ROW_PITCH = PEER_N_KEYS + SUBLANES


def _peer_weights_kernel(code_ref, gate_ref, o_ref, w_scr, *, unroll):
    nk = PEER_N_KEYS
    tt, picks = code_ref.shape
    sub = lax.broadcasted_iota(jnp.int32, (nk, picks), 0)
    zero = jnp.zeros((nk, picks), BF16)

    def one_hots(t):
        c = code_ref[pl.ds(t, 1), :]
        g = gate_ref[pl.ds(t, 1), :]
        e1 = lax.shift_right_logical(c, 7)
        e2 = lax.bitwise_and(c, nk - 1)
        at = jnp.where(sub == e1, g, 0.0).astype(BF16)
        bt = jnp.where(sub == e2, 1.0, 0.0).astype(BF16)
        return at, bt

    def body(p, carry):
        t = 2 * p
        at0, bt0 = one_hots(t)
        at1, bt1 = one_hots(t + 1)
        at = jnp.concatenate([at0, at1], axis=1)
        bt = jnp.concatenate([jnp.concatenate([bt0, zero], axis=1),
                              jnp.concatenate([zero, bt1], axis=1)], axis=0)
        w = lax.dot_general(at, bt, NT_DIMS, preferred_element_type=F32)
        row = pl.multiple_of(t * W_ROW_PITCH, SUBLANES)
        w_scr[pl.ds(row, nk), :] = w[:, :nk]
        w_scr[pl.ds(row + W_ROW_PITCH, nk), :] = w[:, nk:]
        return carry

    lax.fori_loop(0, tt // 2, body, 0, unroll=unroll)
    for e1 in range(nk):
        o_ref[:, e1 * nk:(e1 + 1) * nk] = w_scr[pl.ds(e1, tt, stride=W_ROW_PITCH), :].astype(o_ref.dtype)


def peer_dense_weights(code, gate, tt=128, unroll=64):
    n, picks = code.shape
    tt = min(tt, n)
    nk = PEER_N_KEYS
    kern = functools.partial(_peer_weights_kernel, unroll=unroll)
    return pl.pallas_call(
        kern,
        grid=(n // tt,),
        in_specs=[pl.BlockSpec((tt, picks), lambda i: (i, 0)),
                  pl.BlockSpec((tt, picks), lambda i: (i, 0))],
        out_specs=pl.BlockSpec((tt, nk * nk), lambda i: (i, 0)),
        out_shape=jax.ShapeDtypeStruct((n, nk * nk), BF16),
        scratch_shapes=[pltpu.VMEM((tt * W_ROW_PITCH, nk), F32)],
        compiler_params=_params(("parallel",)), name="peer_weights",
    )(code, gate)


def _row_rms_scale(t, g):
    return t * lax.rsqrt(jnp.mean(t * t, axis=-1, keepdims=True) + NORM_EPS) * g


def _peer_dense_kernel(x_ref, u_ref, v_ref, w_ref, h_ref, gout_ref, o_ref, acc_scr, *, n_edge, n_main):
    j = pl.program_id(1)
    edge_rows = h_ref.shape[0]

    @pl.when(j == 0)
    def _():
        acc_scr[...] = jnp.zeros(acc_scr.shape, F32)

    @pl.when(j < n_main)
    def _():
        a = lax.dot_general(x_ref[...], u_ref[...], NT_DIMS, preferred_element_type=F32)
        act = 0.5 * a * (1.0 + lax.erf(a * math.sqrt(0.5))) * w_ref[...].astype(F32)
        acc_scr[...] += jnp.dot(act.astype(BF16), v_ref[...], preferred_element_type=F32)

    @pl.when(j >= n_main)
    def _():
        r = j - n_main
        rows = pl.ds(pl.multiple_of(r * edge_rows, edge_rows), edge_rows)
        o_ref[...] = _row_rms_scale(h_ref[...] + acc_scr[rows, :], gout_ref[...])


def peer_dense_final(x, u, v, w, h, g_out, tn=1024, te=512, edge_rows=128):
    n, d = h.shape
    ne = u.shape[0]
    tn, te = min(tn, n), min(te, ne)
    edge_rows = min(edge_rows, tn)
    n_edge, n_main = tn // edge_rows, ne // te

    def main_step(j):
        return jnp.minimum(j, n_main - 1)

    def edge_chunk(i, j):
        return (i * n_edge + jnp.maximum(j - n_main, 0), 0)

    kern = functools.partial(_peer_dense_kernel, n_edge=n_edge, n_main=n_main)
    return pl.pallas_call(
        kern,
        grid=(n // tn, n_main + n_edge),
        in_specs=[
            pl.BlockSpec((tn, d), lambda i, j: (i, 0), pipeline_mode=pl.Buffered(1)),
            pl.BlockSpec((te, d), lambda i, j: (main_step(j), 0)),
            pl.BlockSpec((te, d), lambda i, j: (main_step(j), 0)),
            pl.BlockSpec((tn, te), lambda i, j: (i, main_step(j))),
            pl.BlockSpec((edge_rows, d), edge_chunk),
            pl.BlockSpec((1, d), lambda i, j: (0, 0)),
        ],
        out_specs=pl.BlockSpec((edge_rows, d), edge_chunk),
        out_shape=jax.ShapeDtypeStruct((n, d), F32),
        scratch_shapes=[pltpu.VMEM((tn, d), F32)],
        compiler_params=_params(("parallel", "arbitrary")), name="peer_dense",
    )(x, u, v, w, h, g_out.reshape(1, d))


def kernel(x, norm_mix, w_in, conv_w, conv_b, w_br_attn, w_br_conv, b_gate, rel_bias,
           w_out, norm_ffn, peer_w_q, peer_sub_keys, peer_u, peer_v, norm_final):
    batch, seq, d = x.shape
    n = batch * seq
    assert norm_mix.shape[0] == 1, "single-layer problem"
    h = x.reshape(n, d)

    hn = rmsnorm(h, norm_mix[0], BF16)
    proj, wq_bf16 = matmul(hn, w_in[0], BF16, side_table=peer_w_q[0], tm=512, tn=2048, name="in_proj")
    attn, (u_bf16, v_bf16) = moba_attention(
        proj, rel_bias, batch, seq, q_col=0, k_col=ATTN_WIDTH // HEAD_DIM, v_col=2 * ATTN_WIDTH // HEAD_DIM,
        side_tables=(peer_u[0], peer_v[0]))
    cw = 512
    c0 = 3 * ATTN_WIDTH // cw
    conv = short_conv(proj, conv_w[0].reshape(CONV_K, CONV_WIDTH), conv_b[0].reshape(1, CONV_WIDTH),
                      batch, seq, cb_col=c0, cc_col=c0 + CONV_WIDTH // cw, cu_col=c0 + 2 * CONV_WIDTH // cw)
    tn = 1024
    g0 = (3 * ATTN_WIDTH + 3 * CONV_WIDTH) // tn
    merged = branch_merge(attn, conv, w_br_attn[0], w_br_conv[0],
                          proj, b_gate[0], ga_col=g0, gc_col=g0 + d // tn, tn=tn)
    h = matmul(merged, w_out[0], F32, residual=h, tm=1024, tn=1024, name="out_proj")

    q, hn2 = norm_matmul(h, norm_ffn[0], wq_bf16, F32, name="peer_query")
    code, gate = peer_route(q, peer_sub_keys[0])
    w = peer_dense_weights(code, gate)
    out = peer_dense_final(hn2, u_bf16, v_bf16, w, h, norm_final)
    return out.reshape(batch, seq, d)
```

```python
import functools
import math

import numpy as np
import jax
import jax.numpy as jnp
from jax import lax
from jax.experimental import pallas as pl
from jax.experimental.pallas import tpu as pltpu

F32 = jnp.float32
BF16 = jnp.bfloat16

D_MODEL = 4096
ATTN_HEADS = 16
HEAD_DIM = 128
ATTN_WIDTH = ATTN_HEADS * HEAD_DIM
MOBA_BLOCK = 256
MOBA_TOPK = 3
REL_BUCKETS = 32
REL_MAX_DIST = 128
CONV_WIDTH = 2048
CONV_K = 3
PEER_HEADS = 8
PEER_N_KEYS = 128
PEER_HALF = 128
PEER_TOPK = 16
NORM_EPS = 1e-6

V7X_VMEM_BYTES = 64 * 1024 * 1024
VMEM_LIMIT = V7X_VMEM_BYTES - 2 * 1024 * 1024
LANES = 128
SUBLANES = 8
NT_DIMS = (((1,), (1,)), ((), ()))
LOG2_E = math.log2(math.e)


def _params(semantics):
    return pltpu.CompilerParams(dimension_semantics=semantics, vmem_limit_bytes=VMEM_LIMIT)


def _rmsnorm_kernel(x_ref, g_ref, o_ref):
    x = x_ref[...]
    ms = jnp.mean(x * x, axis=-1, keepdims=True)
    o_ref[...] = (x * lax.rsqrt(ms + NORM_EPS) * g_ref[...]).astype(o_ref.dtype)


def rmsnorm(x, g, out_dtype, rows=256):
    n, d = x.shape
    rows = min(rows, n)
    row_spec = pl.BlockSpec((rows, d), lambda i: (i, 0))
    return pl.pallas_call(
        _rmsnorm_kernel, grid=(n // rows,),
        in_specs=[row_spec, pl.BlockSpec((1, d), lambda i: (0, 0))], out_specs=row_spec,
        out_shape=jax.ShapeDtypeStruct((n, d), out_dtype),
        compiler_params=_params(("parallel",)), name="rmsnorm",
    )(x, g.reshape(1, d))


def _matmul_kernel(a_ref, b_ref, *rest, n_col_blocks, has_residual, n_side_blocks):
    rest = list(rest)
    b_scr = rest.pop()
    r_ref = rest.pop(0) if has_residual else None
    side_in = rest.pop(0) if n_side_blocks else None
    o_ref = rest.pop(0)
    jj = pl.program_id(0)
    i = pl.program_id(1)
    kc = b_ref.shape[0]

    if n_side_blocks:
        @pl.when(jj * pl.num_programs(1) + i < n_side_blocks)
        def _():
            rest[0][...] = side_in[...].astype(BF16)

    def round_chunk():
        rows = pl.ds(pl.multiple_of(i * kc, kc), kc)
        b_scr[jj % 2, rows, :] = b_ref[...].astype(BF16)

    def multiply():
        acc = jnp.dot(a_ref[...], b_scr[(jj + 1) % 2], preferred_element_type=F32)
        if has_residual:
            acc = r_ref[...] + acc
        o_ref[...] = acc.astype(o_ref.dtype)

    pl.when(jj == 0)(round_chunk)
    pl.when(jj == n_col_blocks)(multiply)

    @pl.when((jj > 0) & (jj < n_col_blocks))
    def _():
        multiply()
        round_chunk()


SIDE_BLOCK_ROWS = 32


def matmul(a, b, out_dtype, residual=None, side_table=None, tm=1024, tn=1024, name="matmul"):
    m, k = a.shape
    _, n = b.shape
    tm, tn = min(tm, m), min(tn, n)
    ni, nj = m // tm, n // tn
    assert k % ni == 0
    kc = k // ni

    def out_index(jj, i):
        return (jnp.where(jj == 0, 0, i), jnp.maximum(jj - 1, 0))

    a_spec = pl.BlockSpec((tm, k), lambda jj, i: (jnp.where(jj == 0, 0, i), 0))
    b_spec = pl.BlockSpec((kc, tn), lambda jj, i: (jnp.where(jj < nj, i, ni - 1), jnp.minimum(jj, nj - 1)))
    o_spec = pl.BlockSpec((tm, tn), out_index)
    ins, specs = (a, b), [a_spec, b_spec]
    out_specs, out_shape = o_spec, jax.ShapeDtypeStruct((m, n), out_dtype)
    if residual is not None:
        ins, specs = ins + (residual,), specs + [o_spec]
    n_side_blocks = 0
    if side_table is not None:
        rows, cols = side_table.shape
        n_side_blocks = rows // SIDE_BLOCK_ROWS
        assert rows % SIDE_BLOCK_ROWS == 0 and n_side_blocks <= (nj + 1) * ni
        side_spec = pl.BlockSpec((SIDE_BLOCK_ROWS, cols),
                                 lambda jj, i: (jnp.minimum(jj * ni + i, n_side_blocks - 1), 0))
        ins, specs = ins + (side_table,), specs + [side_spec]
        out_specs = [o_spec, side_spec]
        out_shape = [out_shape, jax.ShapeDtypeStruct(side_table.shape, BF16)]
    kern = functools.partial(_matmul_kernel, n_col_blocks=nj, has_residual=residual is not None,
                             n_side_blocks=n_side_blocks)
    return pl.pallas_call(
        kern, grid=(nj + 1, ni), in_specs=specs, out_specs=out_specs, out_shape=out_shape,
        scratch_shapes=[pltpu.VMEM((2, k, tn), BF16)],
        compiler_params=_params(("arbitrary", "arbitrary")), name=name,
    )(*ins)


NORM_CHUNK_ROWS = 64


def _rmsnorm_chunks(load_rows, g_ref, dst_ref, unrolled=False):
    def body(c, carry):
        start = c * NORM_CHUNK_ROWS
        rows = pl.ds(start if unrolled else pl.multiple_of(start, NORM_CHUNK_ROWS), NORM_CHUNK_ROWS)
        x = load_rows(rows)
        ms = jnp.mean(x * x, axis=-1, keepdims=True)
        dst_ref[rows, :] = (x * lax.rsqrt(ms + NORM_EPS) * g_ref[...]).astype(dst_ref.dtype)
        return carry

    n_chunks = dst_ref.shape[0] // NORM_CHUNK_ROWS
    if unrolled:
        for c in range(n_chunks):
            body(c, 0)
    else:
        lax.fori_loop(0, n_chunks, body, 0)


def _norm_matmul_kernel(x_ref, g_ref, b_ref, o_ref, hn_ref, hn_scr, *, n_row_tiles):
    s = pl.program_id(0)

    def normalise():
        _rmsnorm_chunks(lambda rows: x_ref[rows, :], g_ref, hn_scr.at[s % 2], unrolled=True)
        hn_ref[...] = hn_scr[s % 2]

    def multiply():
        o_ref[...] = jnp.dot(hn_scr[(s + 1) % 2], b_ref[...], preferred_element_type=F32).astype(o_ref.dtype)

    @pl.when(s == 0)
    def _():
        normalise()

    @pl.when((s > 0) & (s < n_row_tiles))
    def _():
        multiply()
        normalise()

    @pl.when(s == n_row_tiles)
    def _():
        multiply()


def norm_matmul(x, g, b, out_dtype, tm=256, name="norm_matmul"):
    m, k = x.shape
    _, n = b.shape
    tm = min(tm, m)
    ni = m // tm
    kern = functools.partial(_norm_matmul_kernel, n_row_tiles=ni)
    row_tile = lambda s: (jnp.minimum(s, ni - 1), 0)
    return pl.pallas_call(
        kern, grid=(ni + 1,),
        in_specs=[pl.BlockSpec((tm, k), row_tile),
                  pl.BlockSpec((1, k), lambda s: (0, 0)),
                  pl.BlockSpec((k, n), lambda s: (0, 0), pipeline_mode=pl.Buffered(1))],
        out_specs=[pl.BlockSpec((tm, n), lambda s: (jnp.maximum(s - 1, 0), 0)),
                   pl.BlockSpec((tm, k), row_tile)],
        out_shape=[jax.ShapeDtypeStruct((m, n), out_dtype), jax.ShapeDtypeStruct((m, k), BF16)],
        scratch_shapes=[pltpu.VMEM((2, tm, k), BF16)],
        compiler_params=_params(("arbitrary",)), name=name,
    )(x, g.reshape(1, k), b)


def _t5_bucket_table(max_dist):
    dist = np.arange(max_dist, dtype=np.int32)
    max_exact = REL_BUCKETS // 2
    d32 = np.maximum(dist, 1).astype(np.float32)
    ratio = (np.log(d32 / np.float32(max_exact)) / np.float32(math.log(REL_MAX_DIST / max_exact))
             * np.float32(REL_BUCKETS - max_exact))
    large = max_exact + ratio.astype(np.int32)
    large = np.minimum(large, REL_BUCKETS - 1)
    return np.where(dist < max_exact, dist, large).astype(np.int32)


def _moba_bucket_tile():
    L = MOBA_BLOCK
    key = np.arange(2 * L)[:, None]
    qi = np.arange(L)[None, :]
    dist = np.maximum(qi - key + L, 0)
    return _t5_bucket_table(2 * L)[dist]


def _moba_kernel(tab_ref, q_ref, k_ref, v_ref, bucket_ref, *rest, n_blocks, n_side, steps_per_table):
    conv_in, rest = rest[:5], rest[5:]
    side_in, o_ref, conv_o, side_out, bias_scr = (rest[:n_side], rest[n_side], rest[n_side + 1],
                                                  rest[n_side + 2:-1], rest[-1])
    _conv_kernel(*conv_in, conv_o)
    L = MOBA_BLOCK
    h = pl.program_id(0)
    b = pl.program_id(1)

    step = h * pl.num_programs(1) + b
    for t in range(n_side):
        @pl.when((step >= t * steps_per_table) & (step < (t + 1) * steps_per_table))
        def _(t=t):
            def body(c, carry):
                rows = pl.ds(pl.multiple_of(c * NORM_CHUNK_ROWS, NORM_CHUNK_ROWS), NORM_CHUNK_ROWS)
                side_out[t][rows, :] = side_in[t][rows, :].astype(BF16)
                return carry
            lax.fori_loop(0, side_in[t].shape[0] // NORM_CHUNK_ROWS, body, 0)

    @pl.when(b == 0)
    def _():
        bk = bucket_ref[...]
        acc = jnp.zeros(bk.shape, F32)
        for kk in range(REL_BUCKETS):
            acc = jnp.where(bk == kk, tab_ref[kk, h], acc)
        bias_scr[...] = acc * LOG2_E

    neg_inf = jnp.float32(-jnp.inf)
    far = tab_ref[REL_BUCKETS - 1, h] * LOG2_E
    scale = HEAD_DIM ** -0.5 * LOG2_E
    k_means = [jnp.mean(k_ref[n * L:(n + 1) * L, :].astype(F32), axis=0, keepdims=True)
               for n in range(n_blocks)]
    pad = jnp.zeros((SUBLANES - n_blocks % SUBLANES, HEAD_DIM), F32) if n_blocks % SUBLANES else None
    k_mean = jnp.concatenate(k_means + ([pad] if pad is not None else []), axis=0)
    v_t = v_ref[...].astype(F32).T.astype(BF16)
    key_i = lax.broadcasted_iota(jnp.int32, (L, L), 0)
    qry_i = lax.broadcasted_iota(jnp.int32, (L, L), 1)

    for qb in range(n_blocks):
        q = q_ref[qb * L:(qb + 1) * L, :]
        sel = None
        if qb > MOBA_TOPK:
            gate = lax.dot_general(k_mean, q.astype(F32), NT_DIMS,
                                   precision=lax.Precision.HIGHEST, preferred_element_type=F32)
            g = [gate[m:m + 1, :] for m in range(qb)]
            sel = []
            for n in range(qb):
                rank = jnp.zeros((1, L), F32)
                for m in range(qb):
                    if m != n:
                        beats = (g[m] > g[n]) | (g[m] == g[n]) if m < n else (g[m] > g[n])
                        rank = rank + jnp.where(beats, 1.0, 0.0)
                sel.append(jnp.where(rank < MOBA_TOPK, 1.0, 0.0))
        logits = []
        for n in range(qb + 1):
            s = lax.dot_general(k_ref[n * L:(n + 1) * L, :], q, NT_DIMS, preferred_element_type=F32)
            if n == qb:
                logits.append(jnp.where(qry_i >= key_i, s * scale + bias_scr[L:2 * L, :], neg_inf))
            else:
                lg = s * scale + (bias_scr[0:L, :] if n == qb - 1 else far)
                logits.append(lg if sel is None else jnp.where(sel[n] > 0.5, lg, neg_inf))
        mx = logits[0].max(axis=0, keepdims=True)
        for lg in logits[1:]:
            mx = jnp.maximum(mx, lg.max(axis=0, keepdims=True))
        denom = jnp.zeros((1, L), F32)
        acc = jnp.zeros((HEAD_DIM, L), F32)
        for n, lg in enumerate(logits):
            p = jnp.exp2(lg - mx)
            denom = denom + p.sum(axis=0, keepdims=True)
            acc = acc + jnp.dot(v_t[:, n * L:(n + 1) * L], p.astype(BF16), preferred_element_type=F32)
        o_ref[qb * L:(qb + 1) * L, :] = (acc / denom).T.astype(o_ref.dtype)


def moba_attention(proj, rel_bias, batch, seq, q_col, k_col, v_col, conv_cols, conv_w, conv_b, side_tables=()):
    L = MOBA_BLOCK
    n_blocks = seq // L
    bucket = jnp.asarray(_moba_bucket_tile())
    n_side = len(side_tables)
    n_steps = ATTN_HEADS * batch
    steps_per_table = n_steps // max(n_side, 1)
    side_specs, side_shapes = [], []
    for t, tab in enumerate(side_tables):
        rows, cols = tab.shape
        assert rows % steps_per_table == 0
        def index_map(h, b, t=t):
            return (jnp.clip(h * batch + b - t * steps_per_table, 0, steps_per_table - 1), 0)
        side_specs.append(pl.BlockSpec((rows // steps_per_table, cols), index_map))
        side_shapes.append(jax.ShapeDtypeStruct(tab.shape, BF16))
    assert conv_w.shape[1] == ATTN_HEADS * HEAD_DIM
    head_block = pl.BlockSpec((seq, HEAD_DIM), lambda h, b: (b, h))
    kern = functools.partial(_moba_kernel, n_blocks=n_blocks, n_side=n_side, steps_per_table=steps_per_table)
    outs = pl.pallas_call(
        kern,
        grid=(ATTN_HEADS, batch),
        in_specs=[
            pl.BlockSpec(memory_space=pltpu.SMEM),
            pl.BlockSpec((seq, HEAD_DIM), lambda h, b: (b, q_col + h)),
            pl.BlockSpec((seq, HEAD_DIM), lambda h, b: (b, k_col + h)),
            pl.BlockSpec((seq, HEAD_DIM), lambda h, b: (b, v_col + h)),
            pl.BlockSpec((2 * L, L), lambda h, b: (0, 0)),
        ] + [pl.BlockSpec((seq, HEAD_DIM), lambda h, b, c=c: (b, c + h)) for c in conv_cols] + [
            pl.BlockSpec((CONV_K, HEAD_DIM), lambda h, b: (0, h)),
            pl.BlockSpec((1, HEAD_DIM), lambda h, b: (0, h)),
        ] + side_specs,
        out_specs=[head_block, head_block] + side_specs,
        out_shape=[jax.ShapeDtypeStruct((batch * seq, ATTN_WIDTH), BF16),
                   jax.ShapeDtypeStruct((batch * seq, conv_w.shape[1]), BF16)] + side_shapes,
        scratch_shapes=[pltpu.VMEM((2 * L, L), F32)],
        compiler_params=_params(("arbitrary", "arbitrary")), name="moba",
    )(rel_bias, proj, proj, proj, bucket, proj, proj, proj, conv_w, conv_b, *side_tables)
    return outs[0], outs[1], tuple(outs[2:])


def _conv_kernel(cb_ref, cc_ref, cu_ref, w_ref, b_ref, o_ref):
    g = cc_ref[...].astype(F32) * cu_ref[...].astype(F32)
    row = lax.broadcasted_iota(jnp.int32, g.shape, 0)
    y = w_ref[CONV_K - 1:CONV_K, :] * g
    for s in range(1, CONV_K):
        shifted = jnp.where(row >= s, pltpu.roll(g, s, axis=0), 0.0)
        y = y + w_ref[CONV_K - 1 - s:CONV_K - s, :] * shifted
    o_ref[...] = (cb_ref[...].astype(F32) * (y + b_ref[...])).astype(o_ref.dtype)


def short_conv(proj, conv_w, conv_b, batch, seq, cb_col, cc_col, cu_col, cw=512):
    width = conv_w.shape[1]
    return pl.pallas_call(
        _conv_kernel,
        grid=(batch, width // cw),
        in_specs=[
            pl.BlockSpec((seq, cw), lambda b, c: (b, cb_col + c)),
            pl.BlockSpec((seq, cw), lambda b, c: (b, cc_col + c)),
            pl.BlockSpec((seq, cw), lambda b, c: (b, cu_col + c)),
            pl.BlockSpec((CONV_K, cw), lambda b, c: (0, c)),
            pl.BlockSpec((1, cw), lambda b, c: (0, c)),
        ],
        out_specs=pl.BlockSpec((seq, cw), lambda b, c: (b, c)),
        out_shape=jax.ShapeDtypeStruct((batch * seq, width), BF16),
        compiler_params=_params(("parallel", "parallel")), name="short_conv",
    )(proj, proj, proj, conv_w, conv_b)


def _merge_kernel(attn_ref, conv_ref, wa_ref, wc_ref, ga_ref, gc_ref, bg_ref, o_ref, wa_scr, wc_scr,
                  *, n_col_blocks):
    jj = pl.program_id(0)
    i = pl.program_id(1)

    def round_chunks():
        for w_ref, w_scr in ((wa_ref, wa_scr), (wc_ref, wc_scr)):
            kc = w_ref.shape[0]
            w_scr[jj % 2, pl.ds(pl.multiple_of(i * kc, kc), kc), :] = w_ref[...].astype(BF16)

    def multiply():
        za = jnp.dot(attn_ref[...], wa_scr[(jj + 1) % 2], preferred_element_type=F32)
        zc = jnp.dot(conv_ref[...], wc_scr[(jj + 1) % 2], preferred_element_type=F32)
        ga = jax.nn.sigmoid(ga_ref[...].astype(F32) + bg_ref[0:1, :])
        gc = jax.nn.sigmoid(gc_ref[...].astype(F32) + bg_ref[1:2, :])
        o_ref[...] = (ga * za + gc * zc).astype(o_ref.dtype)

    pl.when(jj == 0)(round_chunks)
    pl.when(jj == n_col_blocks)(multiply)

    @pl.when((jj > 0) & (jj < n_col_blocks))
    def _():
        multiply()
        round_chunks()


def branch_merge(attn, conv, wa, wc, proj, b_gate, ga_col, gc_col, tm=1024, tn=1024):
    m, ka = attn.shape
    kc = conv.shape[1]
    n = wa.shape[1]
    tm, tn = min(tm, m), min(tn, n)
    ni, nj = m // tm, n // tn
    assert ka % ni == 0 and kc % ni == 0

    def rows(jj, i):
        return jnp.where(jj == 0, 0, i)

    def col(jj):
        return jnp.maximum(jj - 1, 0)

    def weight_index(jj, i):
        return (jnp.where(jj < nj, i, ni - 1), jnp.minimum(jj, nj - 1))

    kern = functools.partial(_merge_kernel, n_col_blocks=nj)
    return pl.pallas_call(
        kern,
        grid=(nj + 1, ni),
        in_specs=[
            pl.BlockSpec((tm, ka), lambda jj, i: (rows(jj, i), 0)),
            pl.BlockSpec((tm, kc), lambda jj, i: (rows(jj, i), 0)),
            pl.BlockSpec((ka // ni, tn), weight_index),
            pl.BlockSpec((kc // ni, tn), weight_index),
            pl.BlockSpec((tm, tn), lambda jj, i: (rows(jj, i), ga_col + col(jj))),
            pl.BlockSpec((tm, tn), lambda jj, i: (rows(jj, i), gc_col + col(jj))),
            pl.BlockSpec((2, tn), lambda jj, i: (0, col(jj))),
        ],
        out_specs=pl.BlockSpec((tm, tn), lambda jj, i: (rows(jj, i), col(jj))),
        out_shape=jax.ShapeDtypeStruct((m, n), BF16),
        scratch_shapes=[pltpu.VMEM((2, ka, tn), BF16), pltpu.VMEM((2, kc, tn), BF16)],
        compiler_params=_params(("arbitrary", "arbitrary")), name="branch_merge",
    )(attn, conv, wa, wc, proj, proj, b_gate)


def _topk_rows(s, k, vals_ref, rows_ref):
    nrows = s.shape[0]
    rowi = lax.broadcasted_iota(jnp.int32, s.shape, 0).astype(F32)
    for r in range(k):
        m = s.max(axis=0, keepdims=True)
        idx = jnp.where(s == m, rowi, float(nrows)).min(axis=0, keepdims=True)
        s = jnp.where(rowi == idx, -jnp.inf, s)
        vals_ref[r:r + 1, :] = m
        rows_ref[r:r + 1, :] = idx


def _topk_rows_untied(s, k, vals_ref, rows_ref):
    nrows, ncols = s.shape
    probe_rows = 2 * SUBLANES
    lane = lax.broadcasted_iota(jnp.int32, (probe_rows, nrows), 1).astype(F32)
    sub = lax.broadcasted_iota(jnp.int32, (probe_rows, nrows), 0)
    probe = jnp.where(sub == 0, lane, jnp.where(sub == 1, 1.0, 0.0)).astype(BF16)
    worst = jnp.zeros((1, ncols), F32)
    for r in range(k):
        m = s.max(axis=0, keepdims=True)
        hit = s == m
        s = jnp.where(hit, -jnp.inf, s)
        stats = jnp.dot(probe, jnp.where(hit, 1.0, 0.0).astype(BF16), preferred_element_type=F32)
        vals_ref[r:r + 1, :] = m
        rows_ref[r:r + 1, :] = stats[0:1]
        worst = jnp.maximum(worst, stats[1:2])
    return worst


def _pair_candidates(v1, v2):
    K = PEER_TOPK
    ncols = v1.shape[1]
    sub8 = lax.broadcasted_iota(jnp.int32, (SUBLANES, ncols), 0)
    sub16 = lax.broadcasted_iota(jnp.int32, (K, ncols), 0)
    big = float(K * K)
    vals = [v1[0:1] + v2]
    flat = [sub16.astype(F32)]
    for a in range(1, SUBLANES):
        ok = sub8 < K // (a + 1)
        vals.append(jnp.where(ok, v1[a:a + 1] + v2[0:SUBLANES], -jnp.inf))
        flat.append(jnp.where(ok, (sub8 + a * K).astype(F32), big))
    vals.append(v1[SUBLANES:K] + v2[0:1])
    flat.append(((sub8 + SUBLANES) * K).astype(F32))
    return jnp.concatenate(vals, axis=0), jnp.concatenate(flat, axis=0)


def _select_rows(table, which):
    out = jnp.zeros(which.shape, F32)
    for a in range(table.shape[0]):
        out = jnp.where(which == float(a), table[a:a + 1], out)
    return out


def _peer_route_kernel(q_ref, keys_ref, code_ref, gate_ref, s1_scr, s2_scr, code_scr, gate_scr, topk_scr,
                       *, chunk_tokens):
    K = PEER_TOPK
    h = pl.program_id(1)
    q = q_ref[...]
    s1_scr[...] = lax.dot_general(keys_ref[0, 0], q[:, :PEER_HALF], NT_DIMS,
                                  precision=lax.Precision.HIGHEST, preferred_element_type=F32)
    s2_scr[...] = lax.dot_general(keys_ref[0, 1], q[:, PEER_HALF:], NT_DIMS,
                                  precision=lax.Precision.HIGHEST, preferred_element_type=F32)
    row0 = pl.multiple_of(h * K, K)

    def chunk(c, carry):
        off = pl.multiple_of(c * chunk_tokens, chunk_tokens)
        cols = pl.ds(off, chunk_tokens)
        t1 = _topk_rows_untied(s1_scr[:, cols], K, topk_scr.at[0], topk_scr.at[1])
        t2 = _topk_rows_untied(s2_scr[:, cols], K, topk_scr.at[2], topk_scr.at[3])

        @pl.when(jnp.max(jnp.maximum(t1, t2)) > 1.5)
        def _():
            _topk_rows(s1_scr[:, cols], K, topk_scr.at[0], topk_scr.at[1])
            _topk_rows(s2_scr[:, cols], K, topk_scr.at[2], topk_scr.at[3])

        cand, flat = _pair_candidates(topk_scr[0], topk_scr[2])
        for r in range(K):
            m = cand.max(axis=0, keepdims=True)
            first = jnp.where(cand == m, flat, float(K * K)).min(axis=0, keepdims=True)
            cand = jnp.where(flat == first, -jnp.inf, cand)
            topk_scr[4, r:r + 1, :] = m
            topk_scr[5, r:r + 1, :] = first
        score = topk_scr[4]
        e = jnp.exp(score - score[0:1])
        pair = topk_scr[5]
        a = jnp.floor(pair * (1.0 / K))
        e1 = _select_rows(topk_scr[1], a)
        e2 = _select_rows(topk_scr[3], pair - a * K)
        code_scr[pl.ds(row0, K), pl.ds(off, chunk_tokens)] = e1 * float(PEER_N_KEYS) + e2
        gate_scr[pl.ds(row0, K), pl.ds(off, chunk_tokens)] = e / e.sum(axis=0, keepdims=True)
        return carry

    lax.fori_loop(0, q.shape[0] // chunk_tokens, chunk, 0)

    @pl.when(h == PEER_HEADS - 1)
    def _():
        code_ref[...] = code_scr[...].T.astype(jnp.int32)
        gate_ref[...] = gate_scr[...].T


def peer_route(q, sub_keys, tt=1024, chunk_tokens=1024):
    n = q.shape[0]
    tt = min(tt, n)
    chunk_tokens = min(chunk_tokens, tt)
    qd = 2 * PEER_HALF
    picks = PEER_HEADS * PEER_TOPK
    out_spec = pl.BlockSpec((tt, picks), lambda i, h: (i, 0))
    return pl.pallas_call(
        functools.partial(_peer_route_kernel, chunk_tokens=chunk_tokens),
        grid=(n // tt, PEER_HEADS),
        in_specs=[
            pl.BlockSpec((tt, qd), lambda i, h: (i, h)),
            pl.BlockSpec((1, 2, PEER_N_KEYS, PEER_HALF), lambda i, h: (h, 0, 0, 0)),
        ],
        out_specs=[out_spec, out_spec],
        out_shape=[jax.ShapeDtypeStruct((n, picks), jnp.int32),
                   jax.ShapeDtypeStruct((n, picks), F32)],
        scratch_shapes=[pltpu.VMEM((PEER_N_KEYS, tt), F32), pltpu.VMEM((PEER_N_KEYS, tt), F32),
                        pltpu.VMEM((picks, tt), F32), pltpu.VMEM((picks, tt), F32),
                        pltpu.VMEM((6, PEER_TOPK, chunk_tokens), F32)],
        compiler_params=_params(("parallel", "arbitrary")), name="peer_route",
    )(q, sub_keys)


W_ROW_PITCH = PEER_N_KEYS + SUBLANES


def _peer_weights_kernel(code_ref, gate_ref, o_ref, w_scr, *, unroll):
    nk = PEER_N_KEYS
    tt, picks = code_ref.shape
    sub = lax.broadcasted_iota(jnp.int32, (nk, picks), 0)
    zero = jnp.zeros((nk, picks), BF16)

    def one_hots(t):
        c = code_ref[pl.ds(t, 1), :]
        g = gate_ref[pl.ds(t, 1), :]
        e1 = lax.shift_right_logical(c, 7)
        e2 = lax.bitwise_and(c, nk - 1)
        at = jnp.where(sub == e1, g, 0.0).astype(BF16)
        bt = jnp.where(sub == e2, 1.0, 0.0).astype(BF16)
        return at, bt

    def body(p, carry):
        t = 2 * p
        at0, bt0 = one_hots(t)
        at1, bt1 = one_hots(t + 1)
        at = jnp.concatenate([at0, at1], axis=1)
        bt = jnp.concatenate([jnp.concatenate([bt0, zero], axis=1),
                              jnp.concatenate([zero, bt1], axis=1)], axis=0)
        w = lax.dot_general(at, bt, NT_DIMS, preferred_element_type=F32)
        row = pl.multiple_of(t * W_ROW_PITCH, SUBLANES)
        w_scr[pl.ds(row, nk), :] = w[:, :nk]
        w_scr[pl.ds(row + W_ROW_PITCH, nk), :] = w[:, nk:]
        return carry

    lax.fori_loop(0, tt // 2, body, 0, unroll=unroll)
    for e1 in range(nk):
        o_ref[:, e1 * nk:(e1 + 1) * nk] = w_scr[pl.ds(e1, tt, stride=W_ROW_PITCH), :].astype(o_ref.dtype)


def peer_dense_weights(code, gate, tt=128, unroll=64):
    n, picks = code.shape
    tt = min(tt, n)
    nk = PEER_N_KEYS
    kern = functools.partial(_peer_weights_kernel, unroll=unroll)
    return pl.pallas_call(
        kern,
        grid=(n // tt,),
        in_specs=[pl.BlockSpec((tt, picks), lambda i: (i, 0)),
                  pl.BlockSpec((tt, picks), lambda i: (i, 0))],
        out_specs=pl.BlockSpec((tt, nk * nk), lambda i: (i, 0)),
        out_shape=jax.ShapeDtypeStruct((n, nk * nk), BF16),
        scratch_shapes=[pltpu.VMEM((tt * W_ROW_PITCH, nk), F32)],
        compiler_params=_params(("parallel",)), name="peer_weights",
    )(code, gate)


def _row_rms_scale(t, g):
    return t * lax.rsqrt(jnp.mean(t * t, axis=-1, keepdims=True) + NORM_EPS) * g


def _peer_dense_kernel(x_ref, u_ref, v_ref, w_ref, h_ref, gout_ref, o_ref, acc_scr, *, n_edge, n_main):
    j = pl.program_id(1)
    edge_rows = h_ref.shape[0]

    @pl.when(j == 0)
    def _():
        acc_scr[...] = jnp.zeros(acc_scr.shape, F32)

    @pl.when(j < n_main)
    def _():
        a = lax.dot_general(x_ref[...], u_ref[...], NT_DIMS, preferred_element_type=F32)
        act = 0.5 * a * (1.0 + lax.erf(a * math.sqrt(0.5))) * w_ref[...].astype(F32)
        acc_scr[...] += jnp.dot(act.astype(BF16), v_ref[...], preferred_element_type=F32)

    @pl.when(j >= n_main)
    def _():
        r = j - n_main
        rows = pl.ds(pl.multiple_of(r * edge_rows, edge_rows), edge_rows)
        o_ref[...] = _row_rms_scale(h_ref[...] + acc_scr[rows, :], gout_ref[...])


def peer_dense_final(x, u, v, w, h, g_out, tn=1024, te=512, edge_rows=128):
    n, d = h.shape
    ne = u.shape[0]
    tn, te = min(tn, n), min(te, ne)
    edge_rows = min(edge_rows, tn)
    n_edge, n_main = tn // edge_rows, ne // te

    def main_step(j):
        return jnp.minimum(j, n_main - 1)

    def edge_chunk(i, j):
        return (i * n_edge + jnp.maximum(j - n_main, 0), 0)

    kern = functools.partial(_peer_dense_kernel, n_edge=n_edge, n_main=n_main)
    return pl.pallas_call(
        kern,
        grid=(n // tn, n_main + n_edge),
        in_specs=[
            pl.BlockSpec((tn, d), lambda i, j: (i, 0), pipeline_mode=pl.Buffered(1)),
            pl.BlockSpec((te, d), lambda i, j: (main_step(j), 0)),
            pl.BlockSpec((te, d), lambda i, j: (main_step(j), 0)),
            pl.BlockSpec((tn, te), lambda i, j: (i, main_step(j))),
            pl.BlockSpec((edge_rows, d), edge_chunk),
            pl.BlockSpec((1, d), lambda i, j: (0, 0)),
        ],
        out_specs=pl.BlockSpec((edge_rows, d), edge_chunk),
        out_shape=jax.ShapeDtypeStruct((n, d), F32),
        scratch_shapes=[pltpu.VMEM((tn, d), F32)],
        compiler_params=_params(("parallel", "arbitrary")), name="peer_dense",
    )(x, u, v, w, h, g_out.reshape(1, d))


def kernel(x, norm_mix, w_in, conv_w, conv_b, w_br_attn, w_br_conv, b_gate, rel_bias,
           w_out, norm_ffn, peer_w_q, peer_sub_keys, peer_u, peer_v, norm_final):
    batch, seq, d = x.shape
    n = batch * seq
    assert norm_mix.shape[0] == 1, "single-layer problem"
    h = x.reshape(n, d)

    hn = rmsnorm(h, norm_mix[0], BF16)
    proj, wq_bf16 = matmul(hn, w_in[0], BF16, side_table=peer_w_q[0], name="in_proj")
    c0 = 3 * ATTN_WIDTH // HEAD_DIM
    conv_cols = (c0, c0 + CONV_WIDTH // HEAD_DIM, c0 + 2 * CONV_WIDTH // HEAD_DIM)
    attn, conv, (u_bf16, v_bf16) = moba_attention(
        proj, rel_bias, batch, seq, q_col=0, k_col=ATTN_WIDTH // HEAD_DIM, v_col=2 * ATTN_WIDTH // HEAD_DIM,
        conv_cols=conv_cols, conv_w=conv_w[0].reshape(CONV_K, CONV_WIDTH), conv_b=conv_b[0].reshape(1, CONV_WIDTH),
        side_tables=(peer_u[0], peer_v[0]))
    tn = 1024
    g0 = (3 * ATTN_WIDTH + 3 * CONV_WIDTH) // tn
    merged = branch_merge(attn, conv, w_br_attn[0], w_br_conv[0],
                          proj, b_gate[0], ga_col=g0, gc_col=g0 + d // tn, tn=tn)
    h = matmul(merged, w_out[0], F32, residual=h, tm=1024, tn=1024, name="out_proj")

    q, hn2 = norm_matmul(h, norm_ffn[0], wq_bf16, F32, name="peer_query")
    code, gate = peer_route(q, peer_sub_keys[0])
    w = peer_dense_weights(code, gate)
    out = peer_dense_final(hn2, u_bf16, v_bf16, w, h, norm_final)
    return out.reshape(batch, seq, d)
```

```python
import functools
import math

import numpy as np
import jax
import jax.numpy as jnp
from jax import lax
from jax.experimental import pallas as pl
from jax.experimental.pallas import tpu as pltpu

F32 = jnp.float32
BF16 = jnp.bfloat16

D_MODEL = 4096
ATTN_HEADS = 16
HEAD_DIM = 128
ATTN_WIDTH = ATTN_HEADS * HEAD_DIM
MOBA_BLOCK = 256
MOBA_TOPK = 3
REL_BUCKETS = 32
REL_MAX_DIST = 128
CONV_WIDTH = 2048
CONV_K = 3
PEER_HEADS = 8
PEER_N_KEYS = 128
PEER_HALF = 128
PEER_TOPK = 16
NORM_EPS = 1e-6

V7X_VMEM_BYTES = 64 * 1024 * 1024
VMEM_LIMIT = V7X_VMEM_BYTES - 2 * 1024 * 1024
LANES = 128
SUBLANES = 8
NT_DIMS = (((1,), (1,)), ((), ()))
LOG2_E = math.log2(math.e)


def _params(semantics):
    return pltpu.CompilerParams(dimension_semantics=semantics, vmem_limit_bytes=VMEM_LIMIT)


def _rmsnorm_kernel(x_ref, g_ref, o_ref):
    x = x_ref[...]
    ms = jnp.mean(x * x, axis=-1, keepdims=True)
    o_ref[...] = (x * lax.rsqrt(ms + NORM_EPS) * g_ref[...]).astype(o_ref.dtype)


def rmsnorm(x, g, out_dtype, rows=256):
    n, d = x.shape
    rows = min(rows, n)
    row_spec = pl.BlockSpec((rows, d), lambda i: (i, 0))
    return pl.pallas_call(
        _rmsnorm_kernel, grid=(n // rows,),
        in_specs=[row_spec, pl.BlockSpec((1, d), lambda i: (0, 0))], out_specs=row_spec,
        out_shape=jax.ShapeDtypeStruct((n, d), out_dtype),
        compiler_params=_params(("parallel",)), name="rmsnorm",
    )(x, g.reshape(1, d))


def _matmul_kernel(a_ref, b_ref, *rest, n_col_blocks, has_residual, n_side_blocks):
    rest = list(rest)
    b_scr = rest.pop()
    r_ref = rest.pop(0) if has_residual else None
    side_in = [rest.pop(0) for _ in n_side_blocks]
    o_ref = rest.pop(0)
    jj = pl.program_id(0)
    i = pl.program_id(1)
    kc = b_ref.shape[0]

    for t, n_blocks in enumerate(n_side_blocks):
        @pl.when(jj * pl.num_programs(1) + i < n_blocks)
        def _(t=t):
            rest[t][...] = side_in[t][...].astype(BF16)

    def round_chunk():
        rows = pl.ds(pl.multiple_of(i * kc, kc), kc)
        b_scr[jj % 2, rows, :] = b_ref[...].astype(BF16)

    def multiply():
        acc = jnp.dot(a_ref[...], b_scr[(jj + 1) % 2], preferred_element_type=F32)
        if has_residual:
            acc = r_ref[...] + acc
        o_ref[...] = acc.astype(o_ref.dtype)

    pl.when(jj == 0)(round_chunk)
    pl.when(jj == n_col_blocks)(multiply)

    @pl.when((jj > 0) & (jj < n_col_blocks))
    def _():
        multiply()
        round_chunk()


SIDE_BLOCK_ROWS = 32
PEER_SIDE_ROWS = 128


def matmul(a, b, out_dtype, residual=None, side_tables=(), tm=1024, tn=1024, name="matmul"):
    m, k = a.shape
    _, n = b.shape
    tm, tn = min(tm, m), min(tn, n)
    ni, nj = m // tm, n // tn
    assert k % ni == 0
    kc = k // ni

    def out_index(jj, i):
        return (jnp.where(jj == 0, 0, i), jnp.maximum(jj - 1, 0))

    a_spec = pl.BlockSpec((tm, k), lambda jj, i: (jnp.where(jj == 0, 0, i), 0))
    b_spec = pl.BlockSpec((kc, tn), lambda jj, i: (jnp.where(jj < nj, i, ni - 1), jnp.minimum(jj, nj - 1)))
    o_spec = pl.BlockSpec((tm, tn), out_index)
    ins, specs = (a, b), [a_spec, b_spec]
    out_specs, out_shape = o_spec, jax.ShapeDtypeStruct((m, n), out_dtype)
    if residual is not None:
        ins, specs = ins + (residual,), specs + [o_spec]
    n_side_blocks = []
    if side_tables:
        out_specs, out_shape = [out_specs], [out_shape]
    for table, block_rows in side_tables:
        rows, cols = table.shape
        n_blocks = rows // block_rows
        assert rows % block_rows == 0 and n_blocks <= (nj + 1) * ni
        side_spec = pl.BlockSpec((block_rows, cols),
                                 lambda jj, i, n_blocks=n_blocks: (jnp.minimum(jj * ni + i, n_blocks - 1), 0))
        ins, specs = ins + (table,), specs + [side_spec]
        out_specs.append(side_spec)
        out_shape.append(jax.ShapeDtypeStruct(table.shape, BF16))
        n_side_blocks.append(n_blocks)
    kern = functools.partial(_matmul_kernel, n_col_blocks=nj, has_residual=residual is not None,
                             n_side_blocks=tuple(n_side_blocks))
    return pl.pallas_call(
        kern, grid=(nj + 1, ni), in_specs=specs, out_specs=out_specs, out_shape=out_shape,
        scratch_shapes=[pltpu.VMEM((2, k, tn), BF16)],
        compiler_params=_params(("arbitrary", "arbitrary")), name=name,
    )(*ins)


NORM_CHUNK_ROWS = 64


def _rmsnorm_chunks(load_rows, g_ref, dst_ref, unrolled=False):
    def body(c, carry):
        start = c * NORM_CHUNK_ROWS
        rows = pl.ds(start if unrolled else pl.multiple_of(start, NORM_CHUNK_ROWS), NORM_CHUNK_ROWS)
        x = load_rows(rows)
        ms = jnp.mean(x * x, axis=-1, keepdims=True)
        dst_ref[rows, :] = (x * lax.rsqrt(ms + NORM_EPS) * g_ref[...]).astype(dst_ref.dtype)
        return carry

    n_chunks = dst_ref.shape[0] // NORM_CHUNK_ROWS
    if unrolled:
        for c in range(n_chunks):
            body(c, 0)
    else:
        lax.fori_loop(0, n_chunks, body, 0)


def _norm_matmul_kernel(x_ref, g_ref, b_ref, o_ref, hn_ref, hn_scr, *, n_row_tiles):
    s = pl.program_id(0)

    def normalise():
        _rmsnorm_chunks(lambda rows: x_ref[rows, :], g_ref, hn_scr.at[s % 2], unrolled=True)
        hn_ref[...] = hn_scr[s % 2]

    def multiply():
        o_ref[...] = jnp.dot(hn_scr[(s + 1) % 2], b_ref[...], preferred_element_type=F32).astype(o_ref.dtype)

    @pl.when(s == 0)
    def _():
        normalise()

    @pl.when((s > 0) & (s < n_row_tiles))
    def _():
        multiply()
        normalise()

    @pl.when(s == n_row_tiles)
    def _():
        multiply()


def norm_matmul(x, g, b, out_dtype, tm=256, name="norm_matmul"):
    m, k = x.shape
    _, n = b.shape
    tm = min(tm, m)
    ni = m // tm
    kern = functools.partial(_norm_matmul_kernel, n_row_tiles=ni)
    row_tile = lambda s: (jnp.minimum(s, ni - 1), 0)
    return pl.pallas_call(
        kern, grid=(ni + 1,),
        in_specs=[pl.BlockSpec((tm, k), row_tile),
                  pl.BlockSpec((1, k), lambda s: (0, 0)),
                  pl.BlockSpec((k, n), lambda s: (0, 0), pipeline_mode=pl.Buffered(1))],
        out_specs=[pl.BlockSpec((tm, n), lambda s: (jnp.maximum(s - 1, 0), 0)),
                   pl.BlockSpec((tm, k), row_tile)],
        out_shape=[jax.ShapeDtypeStruct((m, n), out_dtype), jax.ShapeDtypeStruct((m, k), BF16)],
        scratch_shapes=[pltpu.VMEM((2, tm, k), BF16)],
        compiler_params=_params(("arbitrary",)), name=name,
    )(x, g.reshape(1, k), b)


def _t5_bucket_table(max_dist):
    dist = np.arange(max_dist, dtype=np.int32)
    max_exact = REL_BUCKETS // 2
    d32 = np.maximum(dist, 1).astype(np.float32)
    ratio = (np.log(d32 / np.float32(max_exact)) / np.float32(math.log(REL_MAX_DIST / max_exact))
             * np.float32(REL_BUCKETS - max_exact))
    large = max_exact + ratio.astype(np.int32)
    large = np.minimum(large, REL_BUCKETS - 1)
    return np.where(dist < max_exact, dist, large).astype(np.int32)


def _moba_bucket_tile():
    L = MOBA_BLOCK
    key = np.arange(2 * L)[:, None]
    qi = np.arange(L)[None, :]
    dist = np.maximum(qi - key + L, 0)
    return _t5_bucket_table(2 * L)[dist]


def _moba_kernel(tab_ref, q_ref, k_ref, v_ref, bucket_ref, *rest, n_blocks, n_side, steps_per_table):
    conv_in, rest = rest[:5], rest[5:]
    side_in, o_ref, conv_o, side_out, bias_scr = (rest[:n_side], rest[n_side], rest[n_side + 1],
                                                  rest[n_side + 2:-1], rest[-1])
    _conv_kernel(*conv_in, conv_o)
    L = MOBA_BLOCK
    h = pl.program_id(0)
    b = pl.program_id(1)

    step = h * pl.num_programs(1) + b
    for t in range(n_side):
        @pl.when((step >= t * steps_per_table) & (step < (t + 1) * steps_per_table))
        def _(t=t):
            def body(c, carry):
                rows = pl.ds(pl.multiple_of(c * NORM_CHUNK_ROWS, NORM_CHUNK_ROWS), NORM_CHUNK_ROWS)
                side_out[t][rows, :] = side_in[t][rows, :].astype(BF16)
                return carry
            lax.fori_loop(0, side_in[t].shape[0] // NORM_CHUNK_ROWS, body, 0)

    @pl.when(b == 0)
    def _():
        bk = bucket_ref[...]
        acc = jnp.zeros(bk.shape, F32)
        for kk in range(REL_BUCKETS):
            acc = jnp.where(bk == kk, tab_ref[kk, h], acc)
        bias_scr[...] = acc * LOG2_E

    neg_inf = jnp.float32(-jnp.inf)
    far = tab_ref[REL_BUCKETS - 1, h] * LOG2_E
    scale = HEAD_DIM ** -0.5 * LOG2_E
    k_means = [jnp.mean(k_ref[n * L:(n + 1) * L, :].astype(F32), axis=0, keepdims=True)
               for n in range(n_blocks)]
    pad = jnp.zeros((SUBLANES - n_blocks % SUBLANES, HEAD_DIM), F32) if n_blocks % SUBLANES else None
    k_mean = jnp.concatenate(k_means + ([pad] if pad is not None else []), axis=0)
    v_t = v_ref[...].astype(F32).T.astype(BF16)
    key_i = lax.broadcasted_iota(jnp.int32, (L, L), 0)
    qry_i = lax.broadcasted_iota(jnp.int32, (L, L), 1)

    for qb in range(n_blocks):
        q = q_ref[qb * L:(qb + 1) * L, :]
        sel = None
        if qb > MOBA_TOPK:
            gate = lax.dot_general(k_mean, q.astype(F32), NT_DIMS,
                                   precision=lax.Precision.HIGHEST, preferred_element_type=F32)
            g = [gate[m:m + 1, :] for m in range(qb)]
            sel = []
            for n in range(qb):
                rank = jnp.zeros((1, L), F32)
                for m in range(qb):
                    if m != n:
                        beats = (g[m] > g[n]) | (g[m] == g[n]) if m < n else (g[m] > g[n])
                        rank = rank + jnp.where(beats, 1.0, 0.0)
                sel.append(jnp.where(rank < MOBA_TOPK, 1.0, 0.0))
        logits = []
        for n in range(qb + 1):
            s = lax.dot_general(k_ref[n * L:(n + 1) * L, :], q, NT_DIMS, preferred_element_type=F32)
            if n == qb:
                logits.append(jnp.where(qry_i >= key_i, s * scale + bias_scr[L:2 * L, :], neg_inf))
            else:
                lg = s * scale + (bias_scr[0:L, :] if n == qb - 1 else far)
                logits.append(lg if sel is None else jnp.where(sel[n] > 0.5, lg, neg_inf))
        mx = logits[0].max(axis=0, keepdims=True)
        for lg in logits[1:]:
            mx = jnp.maximum(mx, lg.max(axis=0, keepdims=True))
        denom = jnp.zeros((1, L), F32)
        acc = jnp.zeros((HEAD_DIM, L), F32)
        for n, lg in enumerate(logits):
            p = jnp.exp2(lg - mx)
            denom = denom + p.sum(axis=0, keepdims=True)
            acc = acc + jnp.dot(v_t[:, n * L:(n + 1) * L], p.astype(BF16), preferred_element_type=F32)
        o_ref[qb * L:(qb + 1) * L, :] = (acc / denom).T.astype(o_ref.dtype)


def moba_attention(proj, rel_bias, batch, seq, q_col, k_col, v_col, conv_cols, conv_w, conv_b, side_tables=()):
    L = MOBA_BLOCK
    n_blocks = seq // L
    bucket = jnp.asarray(_moba_bucket_tile())
    n_side = len(side_tables)
    n_steps = ATTN_HEADS * batch
    steps_per_table = n_steps // max(n_side, 1)
    side_specs, side_shapes = [], []
    for t, tab in enumerate(side_tables):
        rows, cols = tab.shape
        assert rows % steps_per_table == 0
        def index_map(h, b, t=t):
            return (jnp.clip(h * batch + b - t * steps_per_table, 0, steps_per_table - 1), 0)
        side_specs.append(pl.BlockSpec((rows // steps_per_table, cols), index_map))
        side_shapes.append(jax.ShapeDtypeStruct(tab.shape, BF16))
    assert conv_w.shape[1] == ATTN_HEADS * HEAD_DIM
    head_block = pl.BlockSpec((seq, HEAD_DIM), lambda h, b: (b, h))
    kern = functools.partial(_moba_kernel, n_blocks=n_blocks, n_side=n_side, steps_per_table=steps_per_table)
    outs = pl.pallas_call(
        kern,
        grid=(ATTN_HEADS, batch),
        in_specs=[
            pl.BlockSpec(memory_space=pltpu.SMEM),
            pl.BlockSpec((seq, HEAD_DIM), lambda h, b: (b, q_col + h)),
            pl.BlockSpec((seq, HEAD_DIM), lambda h, b: (b, k_col + h)),
            pl.BlockSpec((seq, HEAD_DIM), lambda h, b: (b, v_col + h)),
            pl.BlockSpec((2 * L, L), lambda h, b: (0, 0)),
        ] + [pl.BlockSpec((seq, HEAD_DIM), lambda h, b, c=c: (b, c + h)) for c in conv_cols] + [
            pl.BlockSpec((CONV_K, HEAD_DIM), lambda h, b: (0, h)),
            pl.BlockSpec((1, HEAD_DIM), lambda h, b: (0, h)),
        ] + side_specs,
        out_specs=[head_block, head_block] + side_specs,
        out_shape=[jax.ShapeDtypeStruct((batch * seq, ATTN_WIDTH), BF16),
                   jax.ShapeDtypeStruct((batch * seq, conv_w.shape[1]), BF16)] + side_shapes,
        scratch_shapes=[pltpu.VMEM((2 * L, L), F32)],
        compiler_params=_params(("arbitrary", "arbitrary")), name="moba",
    )(rel_bias, proj, proj, proj, bucket, proj, proj, proj, conv_w, conv_b, *side_tables)
    return outs[0], outs[1], tuple(outs[2:])


def _conv_kernel(cb_ref, cc_ref, cu_ref, w_ref, b_ref, o_ref):
    g = cc_ref[...].astype(F32) * cu_ref[...].astype(F32)
    row = lax.broadcasted_iota(jnp.int32, g.shape, 0)
    y = w_ref[CONV_K - 1:CONV_K, :] * g
    for s in range(1, CONV_K):
        shifted = jnp.where(row >= s, pltpu.roll(g, s, axis=0), 0.0)
        y = y + w_ref[CONV_K - 1 - s:CONV_K - s, :] * shifted
    o_ref[...] = (cb_ref[...].astype(F32) * (y + b_ref[...])).astype(o_ref.dtype)


def short_conv(proj, conv_w, conv_b, batch, seq, cb_col, cc_col, cu_col, cw=512):
    width = conv_w.shape[1]
    return pl.pallas_call(
        _conv_kernel,
        grid=(batch, width // cw),
        in_specs=[
            pl.BlockSpec((seq, cw), lambda b, c: (b, cb_col + c)),
            pl.BlockSpec((seq, cw), lambda b, c: (b, cc_col + c)),
            pl.BlockSpec((seq, cw), lambda b, c: (b, cu_col + c)),
            pl.BlockSpec((CONV_K, cw), lambda b, c: (0, c)),
            pl.BlockSpec((1, cw), lambda b, c: (0, c)),
        ],
        out_specs=pl.BlockSpec((seq, cw), lambda b, c: (b, c)),
        out_shape=jax.ShapeDtypeStruct((batch * seq, width), BF16),
        compiler_params=_params(("parallel", "parallel")), name="short_conv",
    )(proj, proj, proj, conv_w, conv_b)


def _merge_kernel(attn_ref, conv_ref, wa_ref, wc_ref, ga_ref, gc_ref, bg_ref, o_ref, wa_scr, wc_scr,
                  *, n_col_blocks):
    jj = pl.program_id(0)
    i = pl.program_id(1)

    def round_chunks():
        for w_ref, w_scr in ((wa_ref, wa_scr), (wc_ref, wc_scr)):
            kc = w_ref.shape[0]
            w_scr[jj % 2, pl.ds(pl.multiple_of(i * kc, kc), kc), :] = w_ref[...].astype(BF16)

    def multiply():
        za = jnp.dot(attn_ref[...], wa_scr[(jj + 1) % 2], preferred_element_type=F32)
        zc = jnp.dot(conv_ref[...], wc_scr[(jj + 1) % 2], preferred_element_type=F32)
        ga = jax.nn.sigmoid(ga_ref[...].astype(F32) + bg_ref[0:1, :])
        gc = jax.nn.sigmoid(gc_ref[...].astype(F32) + bg_ref[1:2, :])
        o_ref[...] = (ga * za + gc * zc).astype(o_ref.dtype)

    pl.when(jj == 0)(round_chunks)
    pl.when(jj == n_col_blocks)(multiply)

    @pl.when((jj > 0) & (jj < n_col_blocks))
    def _():
        multiply()
        round_chunks()


def branch_merge(attn, conv, wa, wc, proj, b_gate, ga_col, gc_col, tm=1024, tn=1024):
    m, ka = attn.shape
    kc = conv.shape[1]
    n = wa.shape[1]
    tm, tn = min(tm, m), min(tn, n)
    ni, nj = m // tm, n // tn
    assert ka % ni == 0 and kc % ni == 0

    def rows(jj, i):
        return jnp.where(jj == 0, 0, i)

    def col(jj):
        return jnp.maximum(jj - 1, 0)

    def weight_index(jj, i):
        return (jnp.where(jj < nj, i, ni - 1), jnp.minimum(jj, nj - 1))

    kern = functools.partial(_merge_kernel, n_col_blocks=nj)
    return pl.pallas_call(
        kern,
        grid=(nj + 1, ni),
        in_specs=[
            pl.BlockSpec((tm, ka), lambda jj, i: (rows(jj, i), 0)),
            pl.BlockSpec((tm, kc), lambda jj, i: (rows(jj, i), 0)),
            pl.BlockSpec((ka // ni, tn), weight_index),
            pl.BlockSpec((kc // ni, tn), weight_index),
            pl.BlockSpec((tm, tn), lambda jj, i: (rows(jj, i), ga_col + col(jj))),
            pl.BlockSpec((tm, tn), lambda jj, i: (rows(jj, i), gc_col + col(jj))),
            pl.BlockSpec((2, tn), lambda jj, i: (0, col(jj))),
        ],
        out_specs=pl.BlockSpec((tm, tn), lambda jj, i: (rows(jj, i), col(jj))),
        out_shape=jax.ShapeDtypeStruct((m, n), BF16),
        scratch_shapes=[pltpu.VMEM((2, ka, tn), BF16), pltpu.VMEM((2, kc, tn), BF16)],
        compiler_params=_params(("arbitrary", "arbitrary")), name="branch_merge",
    )(attn, conv, wa, wc, proj, proj, b_gate)


def _topk_rows(s, k, vals_ref, rows_ref):
    nrows = s.shape[0]
    rowi = lax.broadcasted_iota(jnp.int32, s.shape, 0).astype(F32)
    for r in range(k):
        m = s.max(axis=0, keepdims=True)
        idx = jnp.where(s == m, rowi, float(nrows)).min(axis=0, keepdims=True)
        s = jnp.where(rowi == idx, -jnp.inf, s)
        vals_ref[r:r + 1, :] = m
        rows_ref[r:r + 1, :] = idx


def _topk_rows_untied(s, k, vals_ref, rows_ref):
    nrows, ncols = s.shape
    probe_rows = 2 * SUBLANES
    lane = lax.broadcasted_iota(jnp.int32, (probe_rows, nrows), 1).astype(F32)
    sub = lax.broadcasted_iota(jnp.int32, (probe_rows, nrows), 0)
    probe = jnp.where(sub == 0, lane, jnp.where(sub == 1, 1.0, 0.0)).astype(BF16)
    worst = jnp.zeros((1, ncols), F32)
    for r in range(k):
        m = s.max(axis=0, keepdims=True)
        hit = s == m
        s = jnp.where(hit, -jnp.inf, s)
        stats = jnp.dot(probe, jnp.where(hit, 1.0, 0.0).astype(BF16), preferred_element_type=F32)
        vals_ref[r:r + 1, :] = m
        rows_ref[r:r + 1, :] = stats[0:1]
        worst = jnp.maximum(worst, stats[1:2])
    return worst


def _pair_candidates(v1, v2):
    K = PEER_TOPK
    ncols = v1.shape[1]
    sub8 = lax.broadcasted_iota(jnp.int32, (SUBLANES, ncols), 0)
    sub16 = lax.broadcasted_iota(jnp.int32, (K, ncols), 0)
    big = float(K * K)
    vals = [v1[0:1] + v2]
    flat = [sub16.astype(F32)]
    for a in range(1, SUBLANES):
        ok = sub8 < K // (a + 1)
        vals.append(jnp.where(ok, v1[a:a + 1] + v2[0:SUBLANES], -jnp.inf))
        flat.append(jnp.where(ok, (sub8 + a * K).astype(F32), big))
    vals.append(v1[SUBLANES:K] + v2[0:1])
    flat.append(((sub8 + SUBLANES) * K).astype(F32))
    return jnp.concatenate(vals, axis=0), jnp.concatenate(flat, axis=0)


def _select_rows(table, which):
    out = jnp.zeros(which.shape, F32)
    for a in range(table.shape[0]):
        out = jnp.where(which == float(a), table[a:a + 1], out)
    return out


def _peer_route_kernel(q_ref, keys_ref, code_ref, gate_ref, s1_scr, s2_scr, code_scr, gate_scr, topk_scr,
                       *, chunk_tokens):
    K = PEER_TOPK
    h = pl.program_id(1)
    q = q_ref[...]
    s1_scr[...] = lax.dot_general(keys_ref[0, 0], q[:, :PEER_HALF], NT_DIMS,
                                  precision=lax.Precision.HIGHEST, preferred_element_type=F32)
    s2_scr[...] = lax.dot_general(keys_ref[0, 1], q[:, PEER_HALF:], NT_DIMS,
                                  precision=lax.Precision.HIGHEST, preferred_element_type=F32)
    row0 = pl.multiple_of(h * K, K)

    def chunk(c, carry):
        off = pl.multiple_of(c * chunk_tokens, chunk_tokens)
        cols = pl.ds(off, chunk_tokens)
        t1 = _topk_rows_untied(s1_scr[:, cols], K, topk_scr.at[0], topk_scr.at[1])
        t2 = _topk_rows_untied(s2_scr[:, cols], K, topk_scr.at[2], topk_scr.at[3])

        @pl.when(jnp.max(jnp.maximum(t1, t2)) > 1.5)
        def _():
            _topk_rows(s1_scr[:, cols], K, topk_scr.at[0], topk_scr.at[1])
            _topk_rows(s2_scr[:, cols], K, topk_scr.at[2], topk_scr.at[3])

        cand, flat = _pair_candidates(topk_scr[0], topk_scr[2])
        for r in range(K):
            m = cand.max(axis=0, keepdims=True)
            first = jnp.where(cand == m, flat, float(K * K)).min(axis=0, keepdims=True)
            cand = jnp.where(flat == first, -jnp.inf, cand)
            topk_scr[4, r:r + 1, :] = m
            topk_scr[5, r:r + 1, :] = first
        score = topk_scr[4]
        e = jnp.exp(score - score[0:1])
        pair = topk_scr[5]
        a = jnp.floor(pair * (1.0 / K))
        e1 = _select_rows(topk_scr[1], a)
        e2 = _select_rows(topk_scr[3], pair - a * K)
        code_scr[pl.ds(row0, K), pl.ds(off, chunk_tokens)] = e1 * float(PEER_N_KEYS) + e2
        gate_scr[pl.ds(row0, K), pl.ds(off, chunk_tokens)] = e / e.sum(axis=0, keepdims=True)
        return carry

    lax.fori_loop(0, q.shape[0] // chunk_tokens, chunk, 0)

    @pl.when(h == PEER_HEADS - 1)
    def _():
        code_ref[...] = code_scr[...].T.astype(jnp.int32)
        gate_ref[...] = gate_scr[...].T


def peer_route(q, sub_keys, tt=1024, chunk_tokens=1024):
    n = q.shape[0]
    tt = min(tt, n)
    chunk_tokens = min(chunk_tokens, tt)
    qd = 2 * PEER_HALF
    picks = PEER_HEADS * PEER_TOPK
    out_spec = pl.BlockSpec((tt, picks), lambda i, h: (i, 0))
    return pl.pallas_call(
        functools.partial(_peer_route_kernel, chunk_tokens=chunk_tokens),
        grid=(n // tt, PEER_HEADS),
        in_specs=[
            pl.BlockSpec((tt, qd), lambda i, h: (i, h)),
            pl.BlockSpec((1, 2, PEER_N_KEYS, PEER_HALF), lambda i, h: (h, 0, 0, 0)),
        ],
        out_specs=[out_spec, out_spec],
        out_shape=[jax.ShapeDtypeStruct((n, picks), jnp.int32),
                   jax.ShapeDtypeStruct((n, picks), F32)],
        scratch_shapes=[pltpu.VMEM((PEER_N_KEYS, tt), F32), pltpu.VMEM((PEER_N_KEYS, tt), F32),
                        pltpu.VMEM((picks, tt), F32), pltpu.VMEM((picks, tt), F32),
                        pltpu.VMEM((6, PEER_TOPK, chunk_tokens), F32)],
        compiler_params=_params(("parallel", "arbitrary")), name="peer_route",
    )(q, sub_keys)


W_ROW_PITCH = PEER_N_KEYS + SUBLANES


def _peer_weights_kernel(code_ref, gate_ref, o_ref, w_scr, *, unroll):
    nk = PEER_N_KEYS
    tt, picks = code_ref.shape
    sub = lax.broadcasted_iota(jnp.int32, (nk, picks), 0)
    zero = jnp.zeros((nk, picks), BF16)

    def one_hots(t):
        c = code_ref[pl.ds(t, 1), :]
        g = gate_ref[pl.ds(t, 1), :]
        e1 = lax.shift_right_logical(c, 7)
        e2 = lax.bitwise_and(c, nk - 1)
        at = jnp.where(sub == e1, g, 0.0).astype(BF16)
        bt = jnp.where(sub == e2, 1.0, 0.0).astype(BF16)
        return at, bt

    def body(p, carry):
        t = 2 * p
        at0, bt0 = one_hots(t)
        at1, bt1 = one_hots(t + 1)
        at = jnp.concatenate([at0, at1], axis=1)
        bt = jnp.concatenate([jnp.concatenate([bt0, zero], axis=1),
                              jnp.concatenate([zero, bt1], axis=1)], axis=0)
        w = lax.dot_general(at, bt, NT_DIMS, preferred_element_type=F32)
        row = pl.multiple_of(t * W_ROW_PITCH, SUBLANES)
        w_scr[pl.ds(row, nk), :] = w[:, :nk]
        w_scr[pl.ds(row + W_ROW_PITCH, nk), :] = w[:, nk:]
        return carry

    lax.fori_loop(0, tt // 2, body, 0, unroll=unroll)
    for e1 in range(nk):
        o_ref[:, e1 * nk:(e1 + 1) * nk] = w_scr[pl.ds(e1, tt, stride=W_ROW_PITCH), :].astype(o_ref.dtype)


def peer_dense_weights(code, gate, tt=128, unroll=64):
    n, picks = code.shape
    tt = min(tt, n)
    nk = PEER_N_KEYS
    kern = functools.partial(_peer_weights_kernel, unroll=unroll)
    return pl.pallas_call(
        kern,
        grid=(n // tt,),
        in_specs=[pl.BlockSpec((tt, picks), lambda i: (i, 0)),
                  pl.BlockSpec((tt, picks), lambda i: (i, 0))],
        out_specs=pl.BlockSpec((tt, nk * nk), lambda i: (i, 0)),
        out_shape=jax.ShapeDtypeStruct((n, nk * nk), BF16),
        scratch_shapes=[pltpu.VMEM((tt * W_ROW_PITCH, nk), F32)],
        compiler_params=_params(("parallel",)), name="peer_weights",
    )(code, gate)


def _row_rms_scale(t, g):
    return t * lax.rsqrt(jnp.mean(t * t, axis=-1, keepdims=True) + NORM_EPS) * g


def _peer_dense_kernel(x_ref, u_ref, v_ref, w_ref, h_ref, gout_ref, o_ref, acc_scr, *, n_edge, n_main):
    j = pl.program_id(1)
    edge_rows = h_ref.shape[0]

    @pl.when(j == 0)
    def _():
        acc_scr[...] = jnp.zeros(acc_scr.shape, F32)

    @pl.when(j < n_main)
    def _():
        a = lax.dot_general(x_ref[...], u_ref[...], NT_DIMS, preferred_element_type=F32)
        act = 0.5 * a * (1.0 + lax.erf(a * math.sqrt(0.5))) * w_ref[...].astype(F32)
        acc_scr[...] += jnp.dot(act.astype(BF16), v_ref[...], preferred_element_type=F32)

    @pl.when(j >= n_main)
    def _():
        r = j - n_main
        rows = pl.ds(pl.multiple_of(r * edge_rows, edge_rows), edge_rows)
        o_ref[...] = _row_rms_scale(h_ref[...] + acc_scr[rows, :], gout_ref[...])


def peer_dense_final(x, u, v, w, h, g_out, tn=1024, te=512, edge_rows=128):
    n, d = h.shape
    ne = u.shape[0]
    tn, te = min(tn, n), min(te, ne)
    edge_rows = min(edge_rows, tn)
    n_edge, n_main = tn // edge_rows, ne // te

    def main_step(j):
        return jnp.minimum(j, n_main - 1)

    def edge_chunk(i, j):
        return (i * n_edge + jnp.maximum(j - n_main, 0), 0)

    kern = functools.partial(_peer_dense_kernel, n_edge=n_edge, n_main=n_main)
    return pl.pallas_call(
        kern,
        grid=(n // tn, n_main + n_edge),
        in_specs=[
            pl.BlockSpec((tn, d), lambda i, j: (i, 0), pipeline_mode=pl.Buffered(1)),
            pl.BlockSpec((te, d), lambda i, j: (main_step(j), 0)),
            pl.BlockSpec((te, d), lambda i, j: (main_step(j), 0)),
            pl.BlockSpec((tn, te), lambda i, j: (i, main_step(j))),
            pl.BlockSpec((edge_rows, d), edge_chunk),
            pl.BlockSpec((1, d), lambda i, j: (0, 0)),
        ],
        out_specs=pl.BlockSpec((edge_rows, d), edge_chunk),
        out_shape=jax.ShapeDtypeStruct((n, d), F32),
        scratch_shapes=[pltpu.VMEM((tn, d), F32)],
        compiler_params=_params(("parallel", "arbitrary")), name="peer_dense",
    )(x, u, v, w, h, g_out.reshape(1, d))


def kernel(x, norm_mix, w_in, conv_w, conv_b, w_br_attn, w_br_conv, b_gate, rel_bias,
           w_out, norm_ffn, peer_w_q, peer_sub_keys, peer_u, peer_v, norm_final):
    batch, seq, d = x.shape
    n = batch * seq
    assert norm_mix.shape[0] == 1, "single-layer problem"
    h = x.reshape(n, d)

    hn = rmsnorm(h, norm_mix[0], BF16)
    proj, wq_bf16, u_bf16, v_bf16 = matmul(
        hn, w_in[0], BF16, name="in_proj",
        side_tables=((peer_w_q[0], SIDE_BLOCK_ROWS), (peer_u[0], PEER_SIDE_ROWS), (peer_v[0], PEER_SIDE_ROWS)))
    c0 = 3 * ATTN_WIDTH // HEAD_DIM
    conv_cols = (c0, c0 + CONV_WIDTH // HEAD_DIM, c0 + 2 * CONV_WIDTH // HEAD_DIM)
    attn, conv, _ = moba_attention(
        proj, rel_bias, batch, seq, q_col=0, k_col=ATTN_WIDTH // HEAD_DIM, v_col=2 * ATTN_WIDTH // HEAD_DIM,
        conv_cols=conv_cols, conv_w=conv_w[0].reshape(CONV_K, CONV_WIDTH), conv_b=conv_b[0].reshape(1, CONV_WIDTH))
    tn = 1024
    g0 = (3 * ATTN_WIDTH + 3 * CONV_WIDTH) // tn
    merged = branch_merge(attn, conv, w_br_attn[0], w_br_conv[0],
                          proj, b_gate[0], ga_col=g0, gc_col=g0 + d // tn, tn=tn)
    h = matmul(merged, w_out[0], F32, residual=h, tm=1024, tn=1024, name="out_proj")

    q, hn2 = norm_matmul(h, norm_ffn[0], wq_bf16, F32, name="peer_query")
    code, gate = peer_route(q, peer_sub_keys[0])
    w = peer_dense_weights(code, gate)
    out = peer_dense_final(hn2, u_bf16, v_bf16, w, h, norm_final)
    return out.reshape(batch, seq, d)
```
